```python
import math
import jax, jax.numpy as jnp
from jax import lax
import numpy as np

D_MODEL = 1024
BATCH = 32
SEQ = 2048
DEPTH = 4

CTX_LEN = 256
GRID_W = 64
HEAD_DIM = 64
N_Q_HEADS = 8
N_KV_HEADS = 2
Q_PER_KV = N_Q_HEADS // N_KV_HEADS
ATTN_WIDTH = N_Q_HEADS * HEAD_DIM
KV_WIDTH = N_KV_HEADS * HEAD_DIM
WINDOW = 128
Q_BLOCK = 128
ROPE_BASE = 10000.0
ROPE_PAIRS_PER_AXIS = HEAD_DIM // 4
CONV_WIDTH = 256
CONV_K = 3
SSM_WIDTH = 256
SSM_GROUP = 16
SSM_GROUPS = SSM_WIDTH // SSM_GROUP
SSM_STATE = 64
MIX_WIDTH = ATTN_WIDTH + CONV_WIDTH + SSM_WIDTH
IN_WIDTH = ATTN_WIDTH + 2 * KV_WIDTH + 3 * CONV_WIDTH + SSM_WIDTH
D_FF = 4 * D_MODEL
N_MOD = 6
EPS = 1e-6
NEG_INF = -1e30

kernel_name = 'hybrid_dit_parallel_groups'


def rms_norm(t, g):
    tf = t.astype(jnp.float32)
    y = tf * lax.rsqrt(jnp.mean(tf * tf, axis=-1, keepdims=True) + EPS)
    return y.astype(t.dtype) * g


def modulate(t, shift, scale):
    return t * (1 + scale) + shift


def split_in_proj(p):
    sizes = [ATTN_WIDTH, KV_WIDTH, KV_WIDTH, CONV_WIDTH, CONV_WIDTH, CONV_WIDTH]
    return jnp.split(p, list(np.cumsum(sizes)), axis=-1)


def axial_rope_tables(n_tokens):
    rows = n_tokens // GRID_W
    row = jnp.broadcast_to(jnp.arange(rows)[:, None], (rows, GRID_W)).reshape(-1)
    col = jnp.broadcast_to(jnp.arange(GRID_W)[None, :], (rows, GRID_W)).reshape(-1)
    freqs = ROPE_BASE ** (-jnp.arange(ROPE_PAIRS_PER_AXIS, dtype=jnp.float32) / ROPE_PAIRS_PER_AXIS)
    ang = jnp.concatenate([row[:, None].astype(jnp.float32) * freqs,
                           col[:, None].astype(jnp.float32) * freqs], axis=-1)
    return jnp.cos(ang), jnp.sin(ang)


def _rotate(t, c, s):
    t1, t2 = jnp.split(t, 2, axis=-1)
    return jnp.concatenate([t1 * c - t2 * s, t1 * s + t2 * c], axis=-1)


def apply_axial_rope(t, cos, sin):
    c = cos[:, None, :].astype(t.dtype)
    s = sin[:, None, :].astype(t.dtype)
    n = ROPE_PAIRS_PER_AXIS
    half = HEAD_DIM // 2
    return jnp.concatenate([_rotate(t[..., :half], c[..., :n], s[..., :n]),
                            _rotate(t[..., half:], c[..., n:], s[..., n:])], axis=-1)


def windowed_latent_attention(q, k, v, kc, vc, sink):
    bsz, n_lat = q.shape[0], q.shape[1]
    n_ctx = kc.shape[1]
    n_blocks = n_lat // Q_BLOCK
    span = Q_BLOCK + 2 * WINDOW
    scale = HEAD_DIM ** -0.5
    pad = ((0, 0), (WINDOW, WINDOW), (0, 0), (0, 0))
    kp = jnp.pad(k, pad)
    vp = jnp.pad(v, pad)
    s_ctx_all = None

    def block(i):
        start = i * Q_BLOCK
        qb = lax.dynamic_slice_in_dim(q, start, Q_BLOCK, axis=1)
        kb = lax.dynamic_slice_in_dim(kp, start, span, axis=1)
        vb = lax.dynamic_slice_in_dim(vp, start, span, axis=1)
        qpos = start + jnp.arange(Q_BLOCK)
        kpos = start - WINDOW + jnp.arange(span)
        mask = (jnp.abs(qpos[:, None] - kpos[None, :]) <= WINDOW) & (kpos >= 0) & (kpos < n_lat)
        s_lat = jnp.einsum('bqhgd,bkhd->bhgqk', qb, kb).astype(jnp.float32) * scale
        s_lat = jnp.where(mask, s_lat, NEG_INF)
        s_ctx = jnp.einsum('bqhgd,bkhd->bhgqk', qb, kc).astype(jnp.float32) * scale
        s_sink = jnp.broadcast_to(sink.astype(jnp.float32)[None, :, :, None, None],
                                  s_ctx.shape[:-1] + (1,))
        p = jax.nn.softmax(jnp.concatenate([s_lat, s_ctx, s_sink], axis=-1), axis=-1)
        p_lat = p[..., :span].astype(v.dtype)
        p_ctx = p[..., span:span + n_ctx].astype(v.dtype)
        return (jnp.einsum('bhgqk,bkhd->bqhgd', p_lat, vb)
                + jnp.einsum('bhgqk,bkhd->bqhgd', p_ctx, vc))

    o = lax.map(block, jnp.arange(n_blocks))
    return jnp.moveaxis(o, 0, 1).reshape(bsz, n_lat, ATTN_WIDTH)


def context_attention(qc, kc, vc, sink):
    bsz, n_ctx = qc.shape[0], qc.shape[1]
    s = jnp.einsum('bqhgd,bkhd->bhgqk', qc, kc).astype(jnp.float32) * (HEAD_DIM ** -0.5)
    s_sink = jnp.broadcast_to(sink.astype(jnp.float32)[None, :, :, None, None], s.shape[:-1] + (1,))
    p = jax.nn.softmax(jnp.concatenate([s, s_sink], axis=-1), axis=-1)[..., :n_ctx]
    o = jnp.einsum('bhgqk,bkhd->bqhgd', p.astype(vc.dtype), vc)
    return o.reshape(bsz, n_ctx, ATTN_WIDTH)


def centred_conv3(z, w):
    zp = jnp.pad(z, ((0, 0), (1, 1), (0, 0)))
    return zp[:, :-2] * w[0] + zp[:, 1:-1] * w[1] + zp[:, 2:] * w[2]


def diag_scan(lam_bar, drive, h0, reverse):
    if h0 is not None:
        edge = -1 if reverse else 0
        drive = drive.at[:, edge].add(lam_bar * h0)
    decay = jnp.broadcast_to(lam_bar, (1, drive.shape[1]) + lam_bar.shape)

    def combine(left, right):
        a_l, b_l = left
        a_r, b_r = right
        return a_l * a_r, a_r * b_l + b_r

    _, h = lax.associative_scan(combine, (decay, drive), reverse=reverse, axis=1)
    return h


def s5_bidirectional(u, uc, lam_re, lam_im, log_dt, b_re, b_im, c_re, c_im, d_skip, w_glu, b_glu,
                     with_ctx_out):
    f32 = jnp.float32
    lam = lax.complex(lam_re.astype(f32), lam_im.astype(f32))
    dt = jnp.exp(log_dt.astype(f32))[..., None]
    lam_bar = jnp.exp(lam * dt)
    b_bar = ((lam_bar - 1) / lam)[..., None] * lax.complex(b_re.astype(f32), b_im.astype(f32))
    c_mat = lax.complex(c_re.astype(f32), c_im.astype(f32))

    def drive(t, direction):
        tg = t.astype(f32).reshape(t.shape[0], t.shape[1], SSM_GROUPS, SSM_GROUP).astype(jnp.complex64)
        return jnp.einsum('blgi,gpi->blgp', tg, b_bar[direction])

    h_ctx_f = diag_scan(lam_bar[0], drive(uc, 0), None, False)
    h_ctx_b = diag_scan(lam_bar[1], drive(uc, 1), None, True)
    h_lat_f = diag_scan(lam_bar[0], drive(u, 0), h_ctx_f[:, -1], False)
    h_lat_b = diag_scan(lam_bar[1], drive(u, 1), h_ctx_b[:, 0], True)

    def readout(t, hf, hb):
        y = jnp.real(jnp.einsum('blgp,gip->blgi', hf, c_mat[0])
                     + jnp.einsum('blgp,gip->blgi', hb, c_mat[1])).reshape(t.shape)
        y = (y + d_skip.astype(f32) * t.astype(f32)).astype(t.dtype)
        g = jax.nn.gelu(y)
        return g * jax.nn.sigmoid(g @ w_glu + b_glu)

    out_ctx = readout(uc, h_ctx_f, h_ctx_b) if with_ctx_out else None
    return readout(u, h_lat_f, h_lat_b), out_ctx


def hybrid_mixer(h_lat, h_ctx, w_in, conv_w, sink, lam_re, lam_im, log_dt, b_re, b_im, c_re, c_im,
                 d_skip, w_glu, b_glu, w_out, cos, sin, with_ctx_out):
    bsz, n_lat, _ = h_lat.shape
    n_ctx = h_ctx.shape[1]
    q, k, v, cb, cc, cx, u = split_in_proj(h_lat @ w_in)
    qc, kc, vc, cbc, ccc, cxc, uc = split_in_proj(h_ctx @ w_in)
    sink = sink.reshape(N_KV_HEADS, Q_PER_KV)
    q = apply_axial_rope(q.reshape(bsz, n_lat, N_Q_HEADS, HEAD_DIM), cos, sin)
    q = q.reshape(bsz, n_lat, N_KV_HEADS, Q_PER_KV, HEAD_DIM)
    k = apply_axial_rope(k.reshape(bsz, n_lat, N_KV_HEADS, HEAD_DIM), cos, sin)
    v = v.reshape(bsz, n_lat, N_KV_HEADS, HEAD_DIM)
    kc = kc.reshape(bsz, n_ctx, N_KV_HEADS, HEAD_DIM)
    vc = vc.reshape(bsz, n_ctx, N_KV_HEADS, HEAD_DIM)
    attn = windowed_latent_attention(q, k, v, kc, vc, sink)
    conv = cb * centred_conv3(cc * cx, conv_w)
    ssm, ssm_c = s5_bidirectional(u, uc, lam_re, lam_im, log_dt, b_re, b_im, c_re, c_im, d_skip,
                                  w_glu, b_glu, with_ctx_out)
    out_lat = jnp.concatenate([attn, conv, ssm], axis=-1) @ w_out
    if not with_ctx_out:
        return out_lat, None
    qc = qc.reshape(bsz, n_ctx, N_KV_HEADS, Q_PER_KV, HEAD_DIM)
    attn_c = context_attention(qc, kc, vc, sink)
    conv_c = cbc * centred_conv3(ccc * cxc, conv_w)
    out_ctx = jnp.concatenate([attn_c, conv_c, ssm_c], axis=-1) @ w_out
    return out_lat, out_ctx


def squared_relu_mlp(t, w1, w2):
    return jnp.square(jax.nn.relu(t @ w1)) @ w2


def setup_inputs(seed: int = 0) -> dict:
    key = jax.random.key(seed)
    ks = jax.random.split(key, 24)
    f32 = jnp.float32
    nrm = lambda k, shape, s: jax.random.normal(k, shape, f32) * s
    lam_im_base = jnp.pi * jnp.arange(SSM_STATE, dtype=f32)
    return {
        'x': nrm(ks[0], (BATCH, SEQ, D_MODEL), 1.0),
        'c': nrm(ks[1], (BATCH, D_MODEL), 1.0),
        'ctx': nrm(ks[2], (BATCH, CTX_LEN, D_MODEL), 1.0),
        'c_ctx': nrm(ks[3], (D_MODEL,), 1.0),
        'w_ada': nrm(ks[4], (DEPTH, D_MODEL, N_MOD * D_MODEL), 0.5 * D_MODEL ** -0.5),
        'b_ada': nrm(ks[5], (DEPTH, N_MOD * D_MODEL), 0.02),
        'norm_g': 1.0 + nrm(ks[6], (DEPTH, 4, D_MODEL), 0.02),
        'w_in': nrm(ks[7], (DEPTH, D_MODEL, IN_WIDTH), D_MODEL ** -0.5),
        'conv_w': nrm(ks[8], (DEPTH, CONV_K, CONV_WIDTH), CONV_K ** -0.5),
        'attn_sink': nrm(ks[9], (DEPTH, N_Q_HEADS), 0.5),
        'ssm_lam_re': -0.5 + nrm(ks[10], (DEPTH, 2, SSM_GROUPS, SSM_STATE), 0.01),
        'ssm_lam_im': lam_im_base + nrm(ks[11], (DEPTH, 2, SSM_GROUPS, SSM_STATE), 0.01),
        'ssm_log_dt': jax.random.uniform(ks[12], (DEPTH, 2, SSM_GROUPS), f32,
                                         minval=math.log(1e-3), maxval=math.log(1e-1)),
        'ssm_b_re': nrm(ks[13], (DEPTH, 2, SSM_GROUPS, SSM_STATE, SSM_GROUP), (2 * SSM_GROUP) ** -0.5),
        'ssm_b_im': nrm(ks[14], (DEPTH, 2, SSM_GROUPS, SSM_STATE, SSM_GROUP), (2 * SSM_GROUP) ** -0.5),
        'ssm_c_re': nrm(ks[15], (DEPTH, 2, SSM_GROUPS, SSM_GROUP, SSM_STATE), (2 * SSM_STATE) ** -0.5),
        'ssm_c_im': nrm(ks[16], (DEPTH, 2, SSM_GROUPS, SSM_GROUP, SSM_STATE), (2 * SSM_STATE) ** -0.5),
        'ssm_d': nrm(ks[17], (DEPTH, SSM_WIDTH), 1.0),
        'w_glu': nrm(ks[18], (DEPTH, SSM_WIDTH, SSM_WIDTH), SSM_WIDTH ** -0.5),
        'b_glu': nrm(ks[19], (DEPTH, SSM_WIDTH), 0.02),
        'w_out': nrm(ks[20], (DEPTH, MIX_WIDTH, D_MODEL), MIX_WIDTH ** -0.5),
        'w_mlp_in': nrm(ks[21], (DEPTH, D_MODEL, D_FF), D_MODEL ** -0.5),
        'w_mlp_out': nrm(ks[22], (DEPTH, D_FF, D_MODEL), D_FF ** -0.5),
    }


def reference(x, c, ctx, c_ctx, w_ada, b_ada, norm_g, w_in, conv_w, attn_sink, ssm_lam_re, ssm_lam_im,
              ssm_log_dt, ssm_b_re, ssm_b_im, ssm_c_re, ssm_c_im, ssm_d, w_glu, b_glu, w_out,
              w_mlp_in, w_mlp_out):
    cos, sin = axial_rope_tables(x.shape[1])
    c_act = jax.nn.silu(c)
    c_ctx_act = jax.nn.silu(c_ctx)
    h, hc = x, ctx
    for l in range(DEPTH):
        with_ctx_out = l < DEPTH - 1
        mod = (c_act @ w_ada[l] + b_ada[l])[:, None, :]
        mod_c = c_ctx_act @ w_ada[l] + b_ada[l]
        sh1, sc1, g1, sh2, sc2, g2 = jnp.split(mod, N_MOD, axis=-1)
        sh1c, sc1c, g1c, sh2c, sc2c, g2c = jnp.split(mod_c, N_MOD, axis=-1)
        g_pre_mix, g_post_mix, g_pre_mlp, g_post_mlp = norm_g[l]
        a_lat = modulate(rms_norm(h, g_pre_mix), sh1, sc1)
        a_ctx = modulate(rms_norm(hc, g_pre_mix), sh1c, sc1c)
        m_lat, m_ctx = hybrid_mixer(a_lat, a_ctx, w_in[l], conv_w[l], attn_sink[l], ssm_lam_re[l],
                                    ssm_lam_im[l], ssm_log_dt[l], ssm_b_re[l], ssm_b_im[l], ssm_c_re[l],
                                    ssm_c_im[l], ssm_d[l], w_glu[l], b_glu[l], w_out[l], cos, sin,
                                    with_ctx_out)
        h = h + g1 * rms_norm(m_lat, g_post_mix)
        f_lat = squared_relu_mlp(modulate(rms_norm(h, g_pre_mlp), sh2, sc2), w_mlp_in[l], w_mlp_out[l])
        h = h + g2 * rms_norm(f_lat, g_post_mlp)
        if with_ctx_out:
            hc = hc + g1c * rms_norm(m_ctx, g_post_mix)
            f_ctx = squared_relu_mlp(modulate(rms_norm(hc, g_pre_mlp), sh2c, sc2c), w_mlp_in[l], w_mlp_out[l])
            hc = hc + g2c * rms_norm(f_ctx, g_post_mlp)
    return h
```

```python
import functools
import math

import numpy as np
import jax
import jax.numpy as jnp
from jax import lax
from jax.experimental import pallas as pl
from jax.experimental.pallas import tpu as pltpu

F32 = jnp.float32
BF16 = jnp.bfloat16

D_MODEL = 1024
DEPTH = 4
GRID_W = 64
HEAD_DIM = 64
N_Q_HEADS = 8
N_KV_HEADS = 2
Q_PER_KV = N_Q_HEADS // N_KV_HEADS
ATTN_WIDTH = N_Q_HEADS * HEAD_DIM
KV_WIDTH = N_KV_HEADS * HEAD_DIM
WINDOW = 128
Q_BLOCK = 128
ROPE_BASE = 10000.0
ROPE_PAIRS = HEAD_DIM // 4
CONV_WIDTH = 256
SSM_WIDTH = 256
SSM_GROUP = 16
SSM_GROUPS = SSM_WIDTH // SSM_GROUP
SSM_STATE = 64
N_STATE = SSM_GROUPS * SSM_STATE
MIX_WIDTH = ATTN_WIDTH + CONV_WIDTH + SSM_WIDTH
IN_WIDTH = ATTN_WIDTH + 2 * KV_WIDTH + 3 * CONV_WIDTH + SSM_WIDTH
ROPE_WIDTH = ATTN_WIDTH + KV_WIDTH
D_FF = 4 * D_MODEL
N_MOD = 6
EPS = 1e-6
NEG_INF = -1e30

LANES = 128
SUBLANES = 8
VMEM_LIMIT = 56 * 1024 * 1024

ROW_TILE = 512
SCAN_CHUNK = 64
SCAN_COLS = 512
FF_CHUNK = 1024


def _const_spec(shape):
    nd = len(shape)
    return pl.BlockSpec(shape, lambda *_: (0,) * nd, pipeline_mode=pl.Buffered(1))


def _params(n_grid):
    return pltpu.CompilerParams(dimension_semantics=("arbitrary",) * n_grid,
                                vmem_limit_bytes=VMEM_LIMIT)


def _rms(x, g):
    ms = jnp.mean(x * x, axis=-1, keepdims=True)
    return x * lax.rsqrt(ms + EPS) * g


def _sigmoid(x):
    return 1.0 / (1.0 + jnp.exp(-x))


def _ada_kernel(c_ref, w_ref, b_ref, o_ref):
    c = c_ref[...]
    act = (c * _sigmoid(c)).astype(BF16)
    o_ref[0] = jnp.dot(act, w_ref[0].astype(BF16), preferred_element_type=F32) + b_ref[0]


def _ada_mods(cvec, w_ada, b_ada):
    rows = cvec.shape[0]
    tn = 1024
    n_out = N_MOD * D_MODEL
    return pl.pallas_call(
        _ada_kernel,
        grid=(DEPTH, n_out // tn),
        in_specs=[pl.BlockSpec((rows, D_MODEL), lambda l, j: (0, 0)),
                  pl.BlockSpec((1, D_MODEL, tn), lambda l, j: (l, 0, j)),
                  pl.BlockSpec((1, 1, tn), lambda l, j: (l, 0, j))],
        out_specs=pl.BlockSpec((1, rows, tn), lambda l, j: (l, 0, j)),
        out_shape=jax.ShapeDtypeStruct((DEPTH, rows, n_out), F32),
        compiler_params=_params(2),
        name="ada_mods",
    )(cvec, w_ada, b_ada.reshape(DEPTH, 1, n_out))


def _inproj_kernel(h_ref, mod_ref, g_ref, w_ref, cos_ref, sin_ref,
                   q_ref, k_ref, v_ref, cb_ref, z_ref, u_ref):
    mod = mod_ref[0]
    a = _rms(h_ref[0], g_ref[0:1, :]) * (1.0 + mod[1:2, :]) + mod[0:1, :]
    p = jnp.dot(a.astype(BF16), w_ref[...], preferred_element_type=F32)
    lane = lax.broadcasted_iota(jnp.int32, (1, LANES), 1)
    first_of_pair = (lane & ROPE_PAIRS) == 0
    for s in range(ROPE_WIDTH // LANES):
        sl = slice(s * LANES, (s + 1) * LANES)
        t = p[:, sl]
        partner = jnp.where(first_of_pair,
                            pltpu.roll(t, LANES - ROPE_PAIRS, axis=1),
                            pltpu.roll(t, ROPE_PAIRS, axis=1))
        r = (t * cos_ref[:, sl] + partner * sin_ref[:, sl]).astype(BF16)
        if s < ATTN_WIDTH // LANES:
            q_ref[0, :, sl] = r
        else:
            k_ref[0] = r
    o = ROPE_WIDTH
    v_ref[0] = p[:, o:o + KV_WIDTH].astype(BF16)
    o += KV_WIDTH
    cb_ref[0] = p[:, o:o + CONV_WIDTH]
    o += CONV_WIDTH
    z_ref[0] = p[:, o:o + CONV_WIDTH] * p[:, o + CONV_WIDTH:o + 2 * CONV_WIDTH]
    o += 2 * CONV_WIDTH
    u_ref[0] = p[:, o:o + SSM_WIDTH]


def _in_proj(h, mods, g, w_in, cos, sin, tm):
    b, n, _ = h.shape
    widths = [(ATTN_WIDTH, BF16), (KV_WIDTH, BF16), (KV_WIDTH, BF16),
              (CONV_WIDTH, F32), (CONV_WIDTH, F32), (SSM_WIDTH, F32)]
    return pl.pallas_call(
        _inproj_kernel,
        grid=(n // tm, b),
        in_specs=[pl.BlockSpec((1, tm, D_MODEL), lambda j, i: (i, j, 0)),
                  pl.BlockSpec((1, N_MOD, D_MODEL), lambda j, i: (i, 0, 0)),
                  _const_spec((4, D_MODEL)),
                  _const_spec((D_MODEL, IN_WIDTH)),
                  pl.BlockSpec((tm, ROPE_WIDTH), lambda j, i: (j, 0)),
                  pl.BlockSpec((tm, ROPE_WIDTH), lambda j, i: (j, 0))],
        out_specs=[pl.BlockSpec((1, tm, w), lambda j, i: (i, j, 0)) for w, _ in widths],
        out_shape=[jax.ShapeDtypeStruct((b, n, w), dt) for w, dt in widths],
        compiler_params=_params(2),
        name="in_proj",
    )(h, mods, g, w_in, cos, sin)


def _attn_kernel(*refs, tq, has_lat):
    if has_lat:
        (sink_ref, q_ref, kp_ref, kc_ref, kn_ref, vp_ref, vc_ref, vn_ref,
         kx_ref, vx_ref, o_ref) = refs
        kall = jnp.concatenate([kp_ref[0], kc_ref[0], kn_ref[0], kx_ref[0]], axis=0)
        vall = jnp.concatenate([vp_ref[0], vc_ref[0], vn_ref[0], vx_ref[0]], axis=0)
        n_lat = 3 * tq
    else:
        sink_ref, q_ref, kx_ref, vx_ref, o_ref = refs
        kall, vall = kx_ref[0], vx_ref[0]
        n_lat = 0
    n_keys = kall.shape[0]
    rows_all = Q_PER_KV * tq
    q = q_ref[0]

    if has_lat:
        i = pl.program_id(1)
        last = pl.num_programs(1) - 1
        r = lax.broadcasted_iota(jnp.int32, (rows_all, n_keys), 0) & (tq - 1)
        c = lax.broadcasted_iota(jnp.int32, (rows_all, n_keys), 1)
        first_col = jnp.where(i > 0, 0, tq)
        end_col = jnp.where(i < last, 3 * tq, 2 * tq)
        in_seq = (c >= first_col) & (c < end_col)
        mask = (c >= n_lat) | ((c >= r) & (c <= r + 2 * WINDOW) & in_seq)

    outs = []
    for hk in range(N_KV_HEADS):
        ksl = slice(hk * HEAD_DIM, (hk + 1) * HEAD_DIM)
        q4 = jnp.concatenate(
            [q[:, (hk * Q_PER_KV + g) * HEAD_DIM:(hk * Q_PER_KV + g + 1) * HEAD_DIM]
             for g in range(Q_PER_KV)], axis=0)
        s = lax.dot_general(q4, kall[:, ksl], (((1,), (1,)), ((), ())),
                            preferred_element_type=F32)
        if has_lat:
            s = jnp.where(mask, s, NEG_INF)
        sink = jnp.concatenate(
            [jnp.full((tq, 1), sink_ref[hk * Q_PER_KV + g], F32) for g in range(Q_PER_KV)],
            axis=0)
        m = jnp.maximum(jnp.max(s, axis=-1, keepdims=True), sink)
        p = jnp.exp(s - m)
        denom = jnp.sum(p, axis=-1, keepdims=True) + jnp.exp(sink - m)
        o = jnp.dot(p.astype(BF16), vall[:, ksl], preferred_element_type=F32) / denom
        outs += [o[g * tq:(g + 1) * tq, :] for g in range(Q_PER_KV)]
    o_ref[0] = jnp.concatenate(outs, axis=1).astype(BF16)


def _attention_lat(sink, q, k, v, kx, vx):
    b, n, _ = q.shape
    n_ctx = kx.shape[1]
    tq = Q_BLOCK
    nb = n // tq
    kv_spec = lambda f: pl.BlockSpec((1, tq, KV_WIDTH), f)
    prev = lambda i, j: (i, jnp.maximum(j - 1, 0), 0)
    cur = lambda i, j: (i, j, 0)
    nxt = lambda i, j: (i, jnp.minimum(j + 1, nb - 1), 0)
    ctx_spec = pl.BlockSpec((1, n_ctx, KV_WIDTH), lambda i, j: (i, 0, 0))
    return pl.pallas_call(
        functools.partial(_attn_kernel, tq=tq, has_lat=True),
        grid=(b, nb),
        in_specs=[pl.BlockSpec(memory_space=pltpu.SMEM),
                  pl.BlockSpec((1, tq, ATTN_WIDTH), cur),
                  kv_spec(prev), kv_spec(cur), kv_spec(nxt),
                  kv_spec(prev), kv_spec(cur), kv_spec(nxt),
                  ctx_spec, ctx_spec],
        out_specs=pl.BlockSpec((1, tq, ATTN_WIDTH), cur),
        out_shape=jax.ShapeDtypeStruct((b, n, ATTN_WIDTH), BF16),
        compiler_params=_params(2),
        name="attn_lat",
    )(sink, q, k, k, k, v, v, v, kx, vx)


def _attention_ctx(sink, q, kx, vx):
    b, n, _ = q.shape
    tq = n
    ctx_spec = pl.BlockSpec((1, n, KV_WIDTH), lambda i: (i, 0, 0))
    return pl.pallas_call(
        functools.partial(_attn_kernel, tq=tq, has_lat=False),
        grid=(b,),
        in_specs=[pl.BlockSpec(memory_space=pltpu.SMEM),
                  pl.BlockSpec((1, tq, ATTN_WIDTH), lambda i: (i, 0, 0)),
                  ctx_spec, ctx_spec],
        out_specs=pl.BlockSpec((1, tq, ATTN_WIDTH), lambda i: (i, 0, 0)),
        out_shape=jax.ShapeDtypeStruct((b, n, ATTN_WIDTH), BF16),
        compiler_params=_params(1),
        name="attn_ctx",
    )(sink, q, kx, vx)


def _s5_kernel(u_ref, wd_ref, lre_ref, lim_ref, wc_ref, y_ref, d_scr, state, *, tc):
    direction = pl.program_id(1)

    @pl.when(pl.program_id(2) == 0)
    def _():
        state[...] = jnp.zeros_like(state)

    rows = tc * SUBLANES
    u = u_ref[...].reshape(rows, SSM_WIDTH)
    d_scr[...] = jnp.dot(u.astype(BF16), wd_ref[0], preferred_element_type=F32)

    for cc in range(N_STATE // SCAN_COLS):
        re = slice(cc * SCAN_COLS, (cc + 1) * SCAN_COLS)
        im = slice(N_STATE + cc * SCAN_COLS, N_STATE + (cc + 1) * SCAN_COLS)
        ar = jnp.broadcast_to(lre_ref[0, :, re], (SUBLANES, SCAN_COLS))
        ai = jnp.broadcast_to(lim_ref[0, :, re], (SUBLANES, SCAN_COLS))

        def step(t, carry, re=re, im=im, ar=ar, ai=ai):
            hr, hi = carry
            ti = jnp.where(direction == 0, t, tc - 1 - t)
            r0 = pl.multiple_of(ti * SUBLANES, SUBLANES)
            nr = ar * hr - ai * hi + d_scr[pl.ds(r0, SUBLANES), re]
            ni = ar * hi + ai * hr + d_scr[pl.ds(r0, SUBLANES), im]
            d_scr[pl.ds(r0, SUBLANES), re] = nr
            d_scr[pl.ds(r0, SUBLANES), im] = ni
            return nr, ni

        hr, hi = lax.fori_loop(0, tc, step, (state[:, re], state[:, im]), unroll=4)
        state[:, re] = hr
        state[:, im] = hi

    y = jnp.dot(d_scr[...].astype(BF16), wc_ref[0], preferred_element_type=F32)
    y_ref[0] = y.reshape(tc, SUBLANES, SSM_WIDTH)


def _s5_scan(u_tm, n_ctx, wd, lre, lim, wc):
    n, b, _ = u_tm.shape
    tc = SCAN_CHUNK
    nc, ncc = n // tc, n_ctx // tc

    def chunk(d, c):
        back = jnp.where(c < ncc, ncc - 1 - c, nc + ncc - 1 - c)
        return jnp.where(d == 0, c, back)

    return pl.pallas_call(
        functools.partial(_s5_kernel, tc=tc),
        grid=(b // SUBLANES, 2, nc),
        in_specs=[pl.BlockSpec((tc, SUBLANES, SSM_WIDTH), lambda g, d, c: (chunk(d, c), g, 0)),
                  pl.BlockSpec((1, SSM_WIDTH, 2 * N_STATE), lambda g, d, c: (d, 0, 0)),
                  pl.BlockSpec((1, 1, N_STATE), lambda g, d, c: (d, 0, 0)),
                  pl.BlockSpec((1, 1, N_STATE), lambda g, d, c: (d, 0, 0)),
                  pl.BlockSpec((1, 2 * N_STATE, SSM_WIDTH), lambda g, d, c: (d, 0, 0))],
        out_specs=pl.BlockSpec((1, tc, SUBLANES, SSM_WIDTH),
                               lambda g, d, c: (d, chunk(d, c), g, 0)),
        out_shape=jax.ShapeDtypeStruct((2, n, b, SSM_WIDTH), F32),
        scratch_shapes=[pltpu.VMEM((tc * SUBLANES, 2 * N_STATE), F32),
                        pltpu.VMEM((SUBLANES, 2 * N_STATE), F32)],
        compiler_params=_params(3),
        name="s5_scan",
    )(u_tm, wd, lre, lim, wc)


def _s5_weights(lam_re, lam_im, log_dt, b_re, b_im, c_re, c_im):
    lam = lax.complex(lam_re.astype(F32), lam_im.astype(F32))
    dt = jnp.exp(log_dt.astype(F32))[..., None]
    lam_bar = jnp.exp(lam * dt)
    b_bar = ((lam_bar - 1) / lam)[..., None] * lax.complex(b_re.astype(F32), b_im.astype(F32))
    eye = jnp.eye(SSM_GROUPS, dtype=F32)
    bd = lambda m: jnp.einsum('dgpi,gh->dgihp', m, eye).reshape(2, SSM_WIDTH, N_STATE)
    wd = jnp.concatenate([bd(jnp.real(b_bar)), bd(jnp.imag(b_bar))], axis=-1)
    cd = lambda m: jnp.einsum('dgip,gh->dgphi', m, eye).reshape(2, N_STATE, SSM_WIDTH)
    wc = jnp.concatenate([cd(c_re.astype(F32)), -cd(c_im.astype(F32))], axis=1)
    lre = jnp.real(lam_bar).reshape(2, 1, N_STATE)
    lim = jnp.imag(lam_bar).reshape(2, 1, N_STATE)
    return wd.astype(BF16), lre, lim, wc.astype(BF16)


def _block_kernel(h_ref, attn_ref, cb_ref, z_ref, zp_ref, zn_ref, y_ref, u_ref, mod_ref, g_ref,
                  cw_ref, dsk_ref, wglu_ref, bglu_ref, wout_ref, w1_ref, w2_ref, o_ref, *, tm):
    j = pl.program_id(0)
    mod = mod_ref[0]
    z = z_ref[0]
    has_prev = jnp.where(j > 0, 1.0, 0.0)
    has_next = jnp.where(j < pl.num_programs(0) - 1, 1.0, 0.0)
    z_before = zp_ref[0][SUBLANES - 1:SUBLANES, :] * has_prev
    z_after = zn_ref[0][0:1, :] * has_next
    row = lax.broadcasted_iota(jnp.int32, (tm, 1), 0)
    z_dn = jnp.where(row == 0, z_before, pltpu.roll(z, 1, axis=0))
    z_up = jnp.where(row == tm - 1, z_after, pltpu.roll(z, tm - 1, axis=0))
    conv = cb_ref[0] * (z_dn * cw_ref[0:1, :] + z * cw_ref[1:2, :] + z_up * cw_ref[2:3, :])
    y = y_ref[0, 0] + y_ref[1, 0] + dsk_ref[...] * u_ref[0]
    gl = 0.5 * y * (1.0 + jnp.tanh(math.sqrt(2.0 / math.pi) * (y + 0.044715 * (y * y * y))))
    gate = jnp.dot(gl.astype(BF16), wglu_ref[...], preferred_element_type=F32) + bglu_ref[...]
    ssm = gl * _sigmoid(gate)
    a0, a1 = ATTN_WIDTH, ATTN_WIDTH + CONV_WIDTH
    m = (jnp.dot(attn_ref[0], wout_ref[0:a0, :], preferred_element_type=F32)
         + jnp.dot(conv.astype(BF16), wout_ref[a0:a1, :], preferred_element_type=F32)
         + jnp.dot(ssm.astype(BF16), wout_ref[a1:, :], preferred_element_type=F32))
    h1 = h_ref[0] + mod[2:3, :] * _rms(m, g_ref[1:2, :])
    a2 = (_rms(h1, g_ref[2:3, :]) * (1.0 + mod[4:5, :]) + mod[3:4, :]).astype(BF16)
    f = jnp.zeros((tm, D_MODEL), F32)
    for cidx in range(D_FF // FF_CHUNK):
        cs = slice(cidx * FF_CHUNK, (cidx + 1) * FF_CHUNK)
        t = jnp.maximum(jnp.dot(a2, w1_ref[:, cs], preferred_element_type=F32), 0.0)
        f = f + jnp.dot((t * t).astype(BF16), w2_ref[cs, :], preferred_element_type=F32)
    o_ref[0] = h1 + mod[5:6, :] * _rms(f, g_ref[3:4, :])


def _block(h, attn, cb, z, y, u, mods, g, conv_w, d_skip, w_glu, b_glu, w_out, w1, w2, tm):
    b, n, _ = h.shape
    nt = n // tm
    per = tm // SUBLANES
    n8 = n // SUBLANES
    row = lambda w: pl.BlockSpec((1, tm, w), lambda j, i: (i, j, 0))
    return pl.pallas_call(
        functools.partial(_block_kernel, tm=tm),
        grid=(nt, b),
        in_specs=[row(D_MODEL), row(ATTN_WIDTH), row(CONV_WIDTH), row(CONV_WIDTH),
                  pl.BlockSpec((1, SUBLANES, CONV_WIDTH),
                               lambda j, i: (i, jnp.maximum(j * per - 1, 0), 0)),
                  pl.BlockSpec((1, SUBLANES, CONV_WIDTH),
                               lambda j, i: (i, jnp.minimum((j + 1) * per, n8 - 1), 0)),
                  pl.BlockSpec((2, 1, tm, SSM_WIDTH), lambda j, i: (0, i, j, 0)),
                  row(SSM_WIDTH),
                  pl.BlockSpec((1, N_MOD, D_MODEL), lambda j, i: (i, 0, 0)),
                  _const_spec((4, D_MODEL)),
                  _const_spec((3, CONV_WIDTH)),
                  _const_spec((1, SSM_WIDTH)),
                  _const_spec((SSM_WIDTH, SSM_WIDTH)),
                  _const_spec((1, SSM_WIDTH)),
                  _const_spec((MIX_WIDTH, D_MODEL)),
                  _const_spec((D_MODEL, D_FF)),
                  _const_spec((D_FF, D_MODEL))],
        out_specs=row(D_MODEL),
        out_shape=jax.ShapeDtypeStruct((b, n, D_MODEL), F32),
        compiler_params=_params(2),
        name="mix_mlp_block",
    )(h, attn, cb, z, z, z, y, u, mods, g, conv_w, d_skip, w_glu, b_glu, w_out, w1, w2)


def _rope_tables(n_lat, n_ctx):
    rows = n_lat // GRID_W
    row = jnp.broadcast_to(jnp.arange(rows)[:, None], (rows, GRID_W)).reshape(-1)
    col = jnp.broadcast_to(jnp.arange(GRID_W)[None, :], (rows, GRID_W)).reshape(-1)
    freqs = ROPE_BASE ** (-jnp.arange(ROPE_PAIRS, dtype=F32) / ROPE_PAIRS)
    ang_r = row[:, None].astype(F32) * freqs
    ang_c = col[:, None].astype(F32) * freqs
    cos_h = jnp.concatenate([jnp.cos(ang_r)] * 2 + [jnp.cos(ang_c)] * 2, axis=-1)
    sin_h = jnp.concatenate([-jnp.sin(ang_r), jnp.sin(ang_r),
                             -jnp.sin(ang_c), jnp.sin(ang_c)], axis=-1)
    scale = jnp.concatenate([jnp.full((ATTN_WIDTH,), HEAD_DIM ** -0.5, F32),
                             jnp.ones((KV_WIDTH,), F32)])
    n_heads = N_Q_HEADS + N_KV_HEADS
    cos = jnp.tile(cos_h, (1, n_heads)) * scale
    sin = jnp.tile(sin_h, (1, n_heads)) * scale
    cos_c = jnp.broadcast_to(scale, (n_ctx, ROPE_WIDTH))
    sin_c = jnp.zeros((n_ctx, ROPE_WIDTH), F32)
    return cos, sin, cos_c, sin_c


def kernel(x, c, ctx, c_ctx, w_ada, b_ada, norm_g, w_in, conv_w, attn_sink, ssm_lam_re, ssm_lam_im,
           ssm_log_dt, ssm_b_re, ssm_b_im, ssm_c_re, ssm_c_im, ssm_d, w_glu, b_glu, w_out,
           w_mlp_in, w_mlp_out):
    bsz, n_lat, _ = x.shape
    n_ctx = ctx.shape[1]
    tm_lat = min(ROW_TILE, n_lat)
    tm_ctx = min(ROW_TILE, n_ctx)
    cos, sin, cos_c, sin_c = _rope_tables(n_lat, n_ctx)

    pad = (-(bsz + 1)) % SUBLANES
    cvec = jnp.concatenate([c, c_ctx[None, :], jnp.zeros((pad, D_MODEL), F32)], axis=0)
    mods_all = _ada_mods(cvec, w_ada, b_ada)

    h, hc = x, ctx
    for l in range(DEPTH):
        with_ctx_out = l < DEPTH - 1
        mods = mods_all[l, :bsz].reshape(bsz, N_MOD, D_MODEL)
        mods_c = jnp.broadcast_to(mods_all[l, bsz].reshape(1, N_MOD, D_MODEL),
                                  (bsz, N_MOD, D_MODEL))
        g = norm_g[l]
        w_in_l = w_in[l].astype(BF16)
        q, k, v, cb, z, u = _in_proj(h, mods, g, w_in_l, cos, sin, tm_lat)
        qc, kc, vc, cbc, zc, uc = _in_proj(hc, mods_c, g, w_in_l, cos_c, sin_c, tm_ctx)

        attn = _attention_lat(attn_sink[l], q, k, v, kc, vc)

        wd, lre, lim, wc = _s5_weights(ssm_lam_re[l], ssm_lam_im[l], ssm_log_dt[l], ssm_b_re[l],
                                       ssm_b_im[l], ssm_c_re[l], ssm_c_im[l])
        u_tm = jnp.concatenate([jnp.transpose(uc, (1, 0, 2)), jnp.transpose(u, (1, 0, 2))], axis=0)
        y_tm = _s5_scan(u_tm, n_ctx, wd, lre, lim, wc)
        y = jnp.transpose(y_tm[:, n_ctx:], (0, 2, 1, 3))

        shared = (g, conv_w[l], ssm_d[l].reshape(1, SSM_WIDTH), w_glu[l].astype(BF16),
                  b_glu[l].reshape(1, SSM_WIDTH), w_out[l].astype(BF16),
                  w_mlp_in[l].astype(BF16), w_mlp_out[l].astype(BF16))
        h = _block(h, attn, cb, z, y, u, mods, *shared, tm_lat)
        if with_ctx_out:
            attn_c = _attention_ctx(attn_sink[l], qc, kc, vc)
            y_c = jnp.transpose(y_tm[:, :n_ctx], (0, 2, 1, 3))
            hc = _block(hc, attn_c, cbc, zc, y_c, uc, mods_c, *shared, tm_ctx)
    return h
```

```python
import functools
import math

import numpy as np
import jax
import jax.numpy as jnp
from jax import lax
from jax.experimental import pallas as pl
from jax.experimental.pallas import tpu as pltpu

F32 = jnp.float32
BF16 = jnp.bfloat16

D_MODEL = 1024
DEPTH = 4
GRID_W = 64
HEAD_DIM = 64
N_Q_HEADS = 8
N_KV_HEADS = 2
Q_PER_KV = N_Q_HEADS // N_KV_HEADS
ATTN_WIDTH = N_Q_HEADS * HEAD_DIM
KV_WIDTH = N_KV_HEADS * HEAD_DIM
WINDOW = 128
Q_BLOCK = 128
ROPE_BASE = 10000.0
ROPE_PAIRS = HEAD_DIM // 4
CONV_WIDTH = 256
SSM_WIDTH = 256
SSM_GROUP = 16
SSM_GROUPS = SSM_WIDTH // SSM_GROUP
SSM_STATE = 64
N_STATE = SSM_GROUPS * SSM_STATE
MIX_WIDTH = ATTN_WIDTH + CONV_WIDTH + SSM_WIDTH
IN_WIDTH = ATTN_WIDTH + 2 * KV_WIDTH + 3 * CONV_WIDTH + SSM_WIDTH
ROPE_WIDTH = ATTN_WIDTH + KV_WIDTH
D_FF = 4 * D_MODEL
N_MOD = 6
EPS = 1e-6
NEG_INF = -1e30
LOG2E = math.log2(math.e)

LANES = 128
SUBLANES = 8
VMEM_LIMIT = 56 * 1024 * 1024

ROW_TILE = 512
SCAN_CHUNK = 64
FF_CHUNK = 1024


def _const_spec(shape):
    nd = len(shape)
    return pl.BlockSpec(shape, lambda *_: (0,) * nd, pipeline_mode=pl.Buffered(1))


def _params(n_grid):
    return pltpu.CompilerParams(dimension_semantics=("arbitrary",) * n_grid,
                                vmem_limit_bytes=VMEM_LIMIT)


def _rms(x, g):
    ms = jnp.mean(x * x, axis=-1, keepdims=True)
    return x * lax.rsqrt(ms + EPS) * g


def _sigmoid(x):
    return 1.0 / (1.0 + jnp.exp(-x))


def _ada_kernel(c_ref, w_ref, b_ref, o_ref):
    c = c_ref[...]
    act = (c * _sigmoid(c)).astype(BF16)
    o_ref[0] = jnp.dot(act, w_ref[0].astype(BF16), preferred_element_type=F32) + b_ref[0]


def _ada_mods(cvec, w_ada, b_ada):
    rows = cvec.shape[0]
    tn = 1024
    n_out = N_MOD * D_MODEL
    return pl.pallas_call(
        _ada_kernel,
        grid=(DEPTH, n_out // tn),
        in_specs=[pl.BlockSpec((rows, D_MODEL), lambda l, j: (0, 0)),
                  pl.BlockSpec((1, D_MODEL, tn), lambda l, j: (l, 0, j)),
                  pl.BlockSpec((1, 1, tn), lambda l, j: (l, 0, j))],
        out_specs=pl.BlockSpec((1, rows, tn), lambda l, j: (l, 0, j)),
        out_shape=jax.ShapeDtypeStruct((DEPTH, rows, n_out), F32),
        compiler_params=_params(2),
        name="ada_mods",
    )(cvec, w_ada, b_ada.reshape(DEPTH, 1, n_out))


def _inproj_kernel(h_ref, mod_ref, g_ref, w_ref, cos_ref, sin_ref,
                   q_ref, k_ref, v_ref, cb_ref, z_ref, u_ref):
    mod = mod_ref[0]
    a = _rms(h_ref[0], g_ref[0:1, :]) * (1.0 + mod[1:2, :]) + mod[0:1, :]
    p = jnp.dot(a.astype(BF16), w_ref[...], preferred_element_type=F32)
    lane = lax.broadcasted_iota(jnp.int32, (1, LANES), 1)
    first_of_pair = (lane & ROPE_PAIRS) == 0
    for s in range(ROPE_WIDTH // LANES):
        sl = slice(s * LANES, (s + 1) * LANES)
        t = p[:, sl]
        partner = jnp.where(first_of_pair,
                            pltpu.roll(t, LANES - ROPE_PAIRS, axis=1),
                            pltpu.roll(t, ROPE_PAIRS, axis=1))
        r = (t * cos_ref[:, sl] + partner * sin_ref[:, sl]).astype(BF16)
        if s < ATTN_WIDTH // LANES:
            q_ref[0, :, sl] = r
        else:
            k_ref[0] = r
    o = ROPE_WIDTH
    v_ref[0] = p[:, o:o + KV_WIDTH].astype(BF16)
    o += KV_WIDTH
    cb_ref[0] = p[:, o:o + CONV_WIDTH]
    o += CONV_WIDTH
    z_ref[0] = p[:, o:o + CONV_WIDTH] * p[:, o + CONV_WIDTH:o + 2 * CONV_WIDTH]
    o += 2 * CONV_WIDTH
    u_ref[0] = p[:, o:o + SSM_WIDTH]


def _in_proj(h, mods, g, w_in, cos, sin, tm):
    b, n, _ = h.shape
    widths = [(ATTN_WIDTH, BF16), (KV_WIDTH, BF16), (KV_WIDTH, BF16),
              (CONV_WIDTH, F32), (CONV_WIDTH, F32), (SSM_WIDTH, F32)]
    return pl.pallas_call(
        _inproj_kernel,
        grid=(n // tm, b),
        in_specs=[pl.BlockSpec((1, tm, D_MODEL), lambda j, i: (i, j, 0)),
                  pl.BlockSpec((1, N_MOD, D_MODEL), lambda j, i: (i, 0, 0)),
                  _const_spec((4, D_MODEL)),
                  _const_spec((D_MODEL, IN_WIDTH)),
                  pl.BlockSpec((tm, ROPE_WIDTH), lambda j, i: (j, 0)),
                  pl.BlockSpec((tm, ROPE_WIDTH), lambda j, i: (j, 0))],
        out_specs=[pl.BlockSpec((1, tm, w), lambda j, i: (i, j, 0)) for w, _ in widths],
        out_shape=[jax.ShapeDtypeStruct((b, n, w), dt) for w, dt in widths],
        compiler_params=_params(2),
        name="in_proj",
    )(h, mods, g, w_in, cos, sin)


def _attn_kernel(*refs, tq, has_lat):
    if has_lat:
        (sink_ref, q_ref, kp_ref, kc_ref, kn_ref, vp_ref, vc_ref, vn_ref,
         kx_ref, vx_ref, bias_ref, o_ref) = refs
        kall = jnp.concatenate([kp_ref[0], kc_ref[0], kn_ref[0], kx_ref[0]], axis=0)
        vall = jnp.concatenate([vp_ref[0], vc_ref[0], vn_ref[0], vx_ref[0]], axis=0)
    else:
        sink_ref, q_ref, kx_ref, vx_ref, o_ref = refs
        kall, vall = kx_ref[0], vx_ref[0]
    n_keys = kall.shape[0]
    q = q_ref[0]

    outs = []
    for hk in range(N_KV_HEADS):
        ksl = slice(hk * HEAD_DIM, (hk + 1) * HEAD_DIM)
        q4 = jnp.concatenate(
            [q[:, (hk * Q_PER_KV + g) * HEAD_DIM:(hk * Q_PER_KV + g + 1) * HEAD_DIM]
             for g in range(Q_PER_KV)], axis=0)
        s = lax.dot_general(q4, kall[:, ksl], (((1,), (1,)), ((), ())),
                            preferred_element_type=F32)
        if has_lat:
            s = (s.reshape(Q_PER_KV, tq, n_keys) + bias_ref[0][None]).reshape(Q_PER_KV * tq, n_keys)
        sink = jnp.concatenate(
            [jnp.full((tq, 1), sink_ref[hk * Q_PER_KV + g] * LOG2E, F32) for g in range(Q_PER_KV)],
            axis=0)
        m = jnp.maximum(jnp.max(s, axis=-1, keepdims=True), sink)
        p = jnp.exp2(s - m)
        denom = jnp.sum(p, axis=-1, keepdims=True) + jnp.exp2(sink - m)
        o = jnp.dot(p.astype(BF16), vall[:, ksl], preferred_element_type=F32) / denom
        outs += [o[g * tq:(g + 1) * tq, :] for g in range(Q_PER_KV)]
    o_ref[0] = jnp.concatenate(outs, axis=1).astype(BF16)


def _window_bias(tq, n_ctx):
    r = np.arange(tq)[:, None]
    c = np.arange(3 * tq + n_ctx)[None, :]
    band = (c >= r) & (c <= r + 2 * WINDOW)
    variants = []
    for first_col, end_col in ((tq, 3 * tq), (0, 3 * tq), (0, 2 * tq)):
        ok = (c >= 3 * tq) | (band & (c >= first_col) & (c < end_col))
        variants.append(np.where(ok, 0.0, NEG_INF))
    return jnp.asarray(np.stack(variants), F32)


def _attention_lat(sink, q, k, v, kx, vx):
    b, n, _ = q.shape
    n_ctx = kx.shape[1]
    tq = Q_BLOCK
    nb = n // tq
    assert tq == WINDOW and nb >= 2
    kv_spec = lambda f: pl.BlockSpec((1, tq, KV_WIDTH), f)
    prev = lambda i, j: (i, jnp.maximum(j - 1, 0), 0)
    cur = lambda i, j: (i, j, 0)
    nxt = lambda i, j: (i, jnp.minimum(j + 1, nb - 1), 0)
    ctx_spec = pl.BlockSpec((1, n_ctx, KV_WIDTH), lambda i, j: (i, 0, 0))
    bias_spec = pl.BlockSpec((1, tq, 3 * tq + n_ctx),
                             lambda i, j: (jnp.where(j == 0, 0, jnp.where(j == nb - 1, 2, 1)), 0, 0))
    return pl.pallas_call(
        functools.partial(_attn_kernel, tq=tq, has_lat=True),
        grid=(b, nb),
        in_specs=[pl.BlockSpec(memory_space=pltpu.SMEM),
                  pl.BlockSpec((1, tq, ATTN_WIDTH), cur),
                  kv_spec(prev), kv_spec(cur), kv_spec(nxt),
                  kv_spec(prev), kv_spec(cur), kv_spec(nxt),
                  ctx_spec, ctx_spec, bias_spec],
        out_specs=pl.BlockSpec((1, tq, ATTN_WIDTH), cur),
        out_shape=jax.ShapeDtypeStruct((b, n, ATTN_WIDTH), BF16),
        compiler_params=_params(2),
        name="attn_lat",
    )(sink, q, k, k, k, v, v, v, kx, vx, _window_bias(tq, n_ctx))


def _attention_ctx(sink, q, kx, vx):
    b, n, _ = q.shape
    tq = n
    ctx_spec = pl.BlockSpec((1, n, KV_WIDTH), lambda i: (i, 0, 0))
    return pl.pallas_call(
        functools.partial(_attn_kernel, tq=tq, has_lat=False),
        grid=(b,),
        in_specs=[pl.BlockSpec(memory_space=pltpu.SMEM),
                  pl.BlockSpec((1, tq, ATTN_WIDTH), lambda i: (i, 0, 0)),
                  ctx_spec, ctx_spec],
        out_specs=pl.BlockSpec((1, tq, ATTN_WIDTH), lambda i: (i, 0, 0)),
        out_shape=jax.ShapeDtypeStruct((b, n, ATTN_WIDTH), BF16),
        compiler_params=_params(1),
        name="attn_ctx",
    )(sink, q, kx, vx)


def _s5_kernel(*refs, tc, reverse, final):
    if final:
        (u_ref, us_ref, yb_ref, wd_ref, lre_ref, lim_ref, wc_ref, dsk_ref, wglu_ref, bglu_ref,
         o_ref, d_scr, state) = refs
    else:
        u_ref, wd_ref, lre_ref, lim_ref, wc_ref, o_ref, d_scr, state = refs
    c = pl.program_id(1)

    @pl.when(c == 0)
    def _():
        d_scr[...] = jnp.zeros_like(d_scr)
        state[...] = jnp.zeros_like(state)

    rows = tc * SUBLANES
    re, im = slice(0, N_STATE), slice(N_STATE, 2 * N_STATE)

    def step(slot_drive, slot_scan, slot_read):
        u = u_ref[...].reshape(rows, SSM_WIDTH)
        d_scr[slot_drive] = jnp.dot(u.astype(BF16), wd_ref[...], preferred_element_type=F32)

        ar = jnp.broadcast_to(lre_ref[...], (SUBLANES, N_STATE))
        ai = jnp.broadcast_to(lim_ref[...], (SUBLANES, N_STATE))
        hr, hi = state[:, re], state[:, im]
        for t in range(tc):
            r0 = (tc - 1 - t if reverse else t) * SUBLANES
            nr = ar * hr - ai * hi + d_scr[slot_scan, r0:r0 + SUBLANES, re]
            ni = ar * hi + ai * hr + d_scr[slot_scan, r0:r0 + SUBLANES, im]
            d_scr[slot_scan, r0:r0 + SUBLANES, re] = nr
            d_scr[slot_scan, r0:r0 + SUBLANES, im] = ni
            hr, hi = nr, ni
        state[:, re] = hr
        state[:, im] = hi

        y = jnp.dot(d_scr[slot_read].astype(BF16), wc_ref[...], preferred_element_type=F32)
        if final:
            y = (y + yb_ref[...].reshape(rows, SSM_WIDTH)
                 + dsk_ref[...] * us_ref[...].reshape(rows, SSM_WIDTH))
            gl = 0.5 * y * (1.0 + jnp.tanh(math.sqrt(2.0 / math.pi) * (y + 0.044715 * (y * y * y))))
            gate = (jnp.dot(gl.astype(BF16), wglu_ref[...], preferred_element_type=F32)
                    + bglu_ref[...])
            y = gl * _sigmoid(gate)
        o_ref[...] = y.reshape(tc, SUBLANES, SSM_WIDTH)

    for k in range(3):
        pl.when(c % 3 == k)(functools.partial(step, k, (k + 2) % 3, (k + 1) % 3))


def _s5_pass(u_tm, n_ctx, wd, lre, lim, wc, reverse, glu=None, y_other=None):
    n, b, _ = u_tm.shape
    tc = SCAN_CHUNK
    nc, ncc = n // tc, n_ctx // tc
    final = glu is not None

    def chunk(k):
        k = jnp.clip(k, 0, nc - 1)
        if not reverse:
            return k
        return jnp.where(k < ncc, ncc - 1 - k, nc + ncc - 1 - k)

    blk = lambda lag: pl.BlockSpec((tc, SUBLANES, SSM_WIDTH), lambda g, c: (chunk(c - lag), g, 0))
    in_specs = [blk(0)]
    args = [u_tm]
    if final:
        in_specs += [blk(2), blk(2)]
        args += [u_tm, y_other]
    in_specs += [_const_spec((SSM_WIDTH, 2 * N_STATE)), _const_spec((1, N_STATE)),
                 _const_spec((1, N_STATE)), _const_spec((2 * N_STATE, SSM_WIDTH))]
    args += [wd, lre, lim, wc]
    if final:
        in_specs += [_const_spec((1, SSM_WIDTH)), _const_spec((SSM_WIDTH, SSM_WIDTH)),
                     _const_spec((1, SSM_WIDTH))]
        args += list(glu)
    return pl.pallas_call(
        functools.partial(_s5_kernel, tc=tc, reverse=reverse, final=final),
        grid=(b // SUBLANES, nc + 2),
        in_specs=in_specs,
        out_specs=blk(2),
        out_shape=jax.ShapeDtypeStruct((n, b, SSM_WIDTH), F32),
        scratch_shapes=[pltpu.VMEM((3, tc * SUBLANES, 2 * N_STATE), F32),
                        pltpu.VMEM((SUBLANES, 2 * N_STATE), F32)],
        compiler_params=_params(2),
        name="s5_fwd_glu" if final else "s5_bwd",
    )(*args)


def _s5_weights(lam_re, lam_im, log_dt, b_re, b_im, c_re, c_im):
    lam_re, lam_im = lam_re.astype(F32), lam_im.astype(F32)
    dt = jnp.exp(log_dt.astype(F32))[..., None]
    mag = jnp.exp(lam_re * dt)
    lb_re, lb_im = mag * jnp.cos(lam_im * dt), mag * jnp.sin(lam_im * dt)
    den = lam_re * lam_re + lam_im * lam_im
    f_re = ((lb_re - 1) * lam_re + lb_im * lam_im) / den
    f_im = (lb_im * lam_re - (lb_re - 1) * lam_im) / den
    b_re, b_im = b_re.astype(F32), b_im.astype(F32)
    bb_re = f_re[..., None] * b_re - f_im[..., None] * b_im
    bb_im = f_re[..., None] * b_im + f_im[..., None] * b_re
    eye = jnp.eye(SSM_GROUPS, dtype=F32)
    bd = lambda m: jnp.einsum('dgpi,gh->dgihp', m, eye).reshape(2, SSM_WIDTH, N_STATE)
    wd = jnp.concatenate([bd(bb_re), bd(bb_im)], axis=-1)
    cd = lambda m: jnp.einsum('dgip,gh->dgphi', m, eye).reshape(2, N_STATE, SSM_WIDTH)
    wc = jnp.concatenate([cd(c_re.astype(F32)), -cd(c_im.astype(F32))], axis=1)
    lre = lb_re.reshape(2, 1, N_STATE)
    lim = lb_im.reshape(2, 1, N_STATE)
    return wd.astype(BF16), lre, lim, wc.astype(BF16)


def _block_kernel(h_ref, attn_ref, cb_ref, z_ref, zp_ref, zn_ref, ssm_ref, mod_ref, g_ref,
                  cw_ref, wout_ref, w1_ref, w2_ref, o_ref, *, tm):
    j = pl.program_id(0)
    mod = mod_ref[0]
    z = z_ref[0]
    has_prev = jnp.where(j > 0, 1.0, 0.0)
    has_next = jnp.where(j < pl.num_programs(0) - 1, 1.0, 0.0)
    z_before = zp_ref[0][SUBLANES - 1:SUBLANES, :] * has_prev
    z_after = zn_ref[0][0:1, :] * has_next
    row = lax.broadcasted_iota(jnp.int32, (tm, 1), 0)
    z_dn = jnp.where(row == 0, z_before, pltpu.roll(z, 1, axis=0))
    z_up = jnp.where(row == tm - 1, z_after, pltpu.roll(z, tm - 1, axis=0))
    conv = cb_ref[0] * (z_dn * cw_ref[0:1, :] + z * cw_ref[1:2, :] + z_up * cw_ref[2:3, :])
    a0, a1 = ATTN_WIDTH, ATTN_WIDTH + CONV_WIDTH
    m = (jnp.dot(attn_ref[0], wout_ref[0:a0, :], preferred_element_type=F32)
         + jnp.dot(conv.astype(BF16), wout_ref[a0:a1, :], preferred_element_type=F32)
         + jnp.dot(ssm_ref[0], wout_ref[a1:, :], preferred_element_type=F32))
    h1 = h_ref[0] + mod[2:3, :] * _rms(m, g_ref[1:2, :])
    a2 = (_rms(h1, g_ref[2:3, :]) * (1.0 + mod[4:5, :]) + mod[3:4, :]).astype(BF16)
    f = jnp.zeros((tm, D_MODEL), F32)
    for cidx in range(D_FF // FF_CHUNK):
        cs = slice(cidx * FF_CHUNK, (cidx + 1) * FF_CHUNK)
        t = jnp.maximum(jnp.dot(a2, w1_ref[:, cs], preferred_element_type=F32), 0.0)
        f = f + jnp.dot((t * t).astype(BF16), w2_ref[cs, :], preferred_element_type=F32)
    o_ref[0] = h1 + mod[5:6, :] * _rms(f, g_ref[3:4, :])


def _block(h, attn, cb, z, ssm, mods, g, conv_w, w_out, w1, w2, tm):
    b, n, _ = h.shape
    nt = n // tm
    per = tm // SUBLANES
    n8 = n // SUBLANES
    row = lambda w: pl.BlockSpec((1, tm, w), lambda j, i: (i, j, 0))
    return pl.pallas_call(
        functools.partial(_block_kernel, tm=tm),
        grid=(nt, b),
        in_specs=[row(D_MODEL), row(ATTN_WIDTH), row(CONV_WIDTH), row(CONV_WIDTH),
                  pl.BlockSpec((1, SUBLANES, CONV_WIDTH),
                               lambda j, i: (i, jnp.maximum(j * per - 1, 0), 0)),
                  pl.BlockSpec((1, SUBLANES, CONV_WIDTH),
                               lambda j, i: (i, jnp.minimum((j + 1) * per, n8 - 1), 0)),
                  row(SSM_WIDTH),
                  pl.BlockSpec((1, N_MOD, D_MODEL), lambda j, i: (i, 0, 0)),
                  _const_spec((4, D_MODEL)),
                  _const_spec((3, CONV_WIDTH)),
                  _const_spec((MIX_WIDTH, D_MODEL)),
                  _const_spec((D_MODEL, D_FF)),
                  _const_spec((D_FF, D_MODEL))],
        out_specs=row(D_MODEL),
        out_shape=jax.ShapeDtypeStruct((b, n, D_MODEL), F32),
        compiler_params=_params(2),
        name="mix_mlp_block",
    )(h, attn, cb, z, z, z, ssm, mods, g, conv_w, w_out, w1, w2)


def _rope_tables(n_lat, n_ctx):
    rows = n_lat // GRID_W
    row = jnp.broadcast_to(jnp.arange(rows)[:, None], (rows, GRID_W)).reshape(-1)
    col = jnp.broadcast_to(jnp.arange(GRID_W)[None, :], (rows, GRID_W)).reshape(-1)
    freqs = ROPE_BASE ** (-jnp.arange(ROPE_PAIRS, dtype=F32) / ROPE_PAIRS)
    ang_r = row[:, None].astype(F32) * freqs
    ang_c = col[:, None].astype(F32) * freqs
    cos_h = jnp.concatenate([jnp.cos(ang_r)] * 2 + [jnp.cos(ang_c)] * 2, axis=-1)
    sin_h = jnp.concatenate([-jnp.sin(ang_r), jnp.sin(ang_r),
                             -jnp.sin(ang_c), jnp.sin(ang_c)], axis=-1)
    scale = jnp.concatenate([jnp.full((ATTN_WIDTH,), LOG2E * HEAD_DIM ** -0.5, F32),
                             jnp.ones((KV_WIDTH,), F32)])
    n_heads = N_Q_HEADS + N_KV_HEADS
    cos = jnp.tile(cos_h, (1, n_heads)) * scale
    sin = jnp.tile(sin_h, (1, n_heads)) * scale
    cos_c = jnp.broadcast_to(scale, (n_ctx, ROPE_WIDTH))
    sin_c = jnp.zeros((n_ctx, ROPE_WIDTH), F32)
    return cos, sin, cos_c, sin_c


def kernel(x, c, ctx, c_ctx, w_ada, b_ada, norm_g, w_in, conv_w, attn_sink, ssm_lam_re, ssm_lam_im,
           ssm_log_dt, ssm_b_re, ssm_b_im, ssm_c_re, ssm_c_im, ssm_d, w_glu, b_glu, w_out,
           w_mlp_in, w_mlp_out):
    bsz, n_lat, _ = x.shape
    n_ctx = ctx.shape[1]
    tm_lat = min(ROW_TILE, n_lat)
    tm_ctx = min(ROW_TILE, n_ctx)
    cos, sin, cos_c, sin_c = _rope_tables(n_lat, n_ctx)

    pad = (-(bsz + 1)) % SUBLANES
    cvec = jnp.concatenate([c, c_ctx[None, :], jnp.zeros((pad, D_MODEL), F32)], axis=0)
    mods_all = _ada_mods(cvec, w_ada, b_ada)

    h, hc = x, ctx
    for l in range(DEPTH):
        with_ctx_out = l < DEPTH - 1
        mods = mods_all[l, :bsz].reshape(bsz, N_MOD, D_MODEL)
        mods_c = jnp.broadcast_to(mods_all[l, bsz].reshape(1, N_MOD, D_MODEL),
                                  (bsz, N_MOD, D_MODEL))
        g = norm_g[l]
        w_in_l = w_in[l].astype(BF16)
        q, k, v, cb, z, u = _in_proj(h, mods, g, w_in_l, cos, sin, tm_lat)
        qc, kc, vc, cbc, zc, uc = _in_proj(hc, mods_c, g, w_in_l, cos_c, sin_c, tm_ctx)

        attn = _attention_lat(attn_sink[l], q, k, v, kc, vc)

        wd, lre, lim, wc = _s5_weights(ssm_lam_re[l], ssm_lam_im[l], ssm_log_dt[l], ssm_b_re[l],
                                       ssm_b_im[l], ssm_c_re[l], ssm_c_im[l])
        u_tm = jnp.concatenate([jnp.transpose(uc, (1, 0, 2)), jnp.transpose(u, (1, 0, 2))], axis=0)
        y_bwd = _s5_pass(u_tm, n_ctx, wd[1], lre[1], lim[1], wc[1], reverse=True)
        glu = (ssm_d[l].reshape(1, SSM_WIDTH), w_glu[l].astype(BF16), b_glu[l].reshape(1, SSM_WIDTH))
        ssm_tm = _s5_pass(u_tm, n_ctx, wd[0], lre[0], lim[0], wc[0], reverse=False,
                          glu=glu, y_other=y_bwd)
        ssm = jnp.transpose(ssm_tm[n_ctx:], (1, 0, 2)).astype(BF16)

        shared = (g, conv_w[l], w_out[l].astype(BF16),
                  w_mlp_in[l].astype(BF16), w_mlp_out[l].astype(BF16))
        h = _block(h, attn, cb, z, ssm, mods, *shared, tm_lat)
        if with_ctx_out:
            attn_c = _attention_ctx(attn_sink[l], qc, kc, vc)
            ssm_c = jnp.transpose(ssm_tm[:n_ctx], (1, 0, 2)).astype(BF16)
            hc = _block(hc, attn_c, cbc, zc, ssm_c, mods_c, *shared, tm_ctx)
    return h
```

```python
import functools
import math

import numpy as np
import jax
import jax.numpy as jnp
from jax import lax
from jax.experimental import pallas as pl
from jax.experimental.pallas import tpu as pltpu

F32 = jnp.float32
BF16 = jnp.bfloat16

D_MODEL = 1024
DEPTH = 4
GRID_W = 64
HEAD_DIM = 64
N_Q_HEADS = 8
N_KV_HEADS = 2
Q_PER_KV = N_Q_HEADS // N_KV_HEADS
ATTN_WIDTH = N_Q_HEADS * HEAD_DIM
KV_WIDTH = N_KV_HEADS * HEAD_DIM
WINDOW = 128
Q_BLOCK = 128
ROPE_BASE = 10000.0
ROPE_PAIRS = HEAD_DIM // 4
CONV_WIDTH = 256
SSM_WIDTH = 256
SSM_GROUP = 16
SSM_GROUPS = SSM_WIDTH // SSM_GROUP
SSM_STATE = 64
N_STATE = SSM_GROUPS * SSM_STATE
MIX_WIDTH = ATTN_WIDTH + CONV_WIDTH + SSM_WIDTH
IN_WIDTH = ATTN_WIDTH + 2 * KV_WIDTH + 3 * CONV_WIDTH + SSM_WIDTH
ROPE_WIDTH = ATTN_WIDTH + KV_WIDTH
D_FF = 4 * D_MODEL
N_MOD = 6
EPS = 1e-6
NEG_INF = -1e30
LOG2E = math.log2(math.e)

LANES = 128
SUBLANES = 8
VMEM_LIMIT = 56 * 1024 * 1024

ROW_TILE = 512
S5_T = 16
S5_BLOCK_CHUNKS = 8
S5_PAIRS = SSM_GROUPS // 2
PAIR_IN = 2 * SSM_GROUP * S5_T
PAIR_STATE = 4 * SSM_STATE
FF_CHUNK = 1024


def _const_spec(shape):
    nd = len(shape)
    return pl.BlockSpec(shape, lambda *_: (0,) * nd, pipeline_mode=pl.Buffered(1))


def _params(n_grid):
    return pltpu.CompilerParams(dimension_semantics=("arbitrary",) * n_grid,
                                vmem_limit_bytes=VMEM_LIMIT)


def _rms(x, g):
    ms = jnp.mean(x * x, axis=-1, keepdims=True)
    return x * lax.rsqrt(ms + EPS) * g


def _sigmoid(x):
    return 1.0 / (1.0 + jnp.exp(-x))


def _ada_kernel(c_ref, w_ref, b_ref, o_ref):
    c = c_ref[...]
    act = (c * _sigmoid(c)).astype(BF16)
    o_ref[0] = jnp.dot(act, w_ref[0].astype(BF16), preferred_element_type=F32) + b_ref[0]


def _ada_mods(cvec, w_ada, b_ada):
    rows = cvec.shape[0]
    tn = 1024
    n_out = N_MOD * D_MODEL
    return pl.pallas_call(
        _ada_kernel,
        grid=(DEPTH, n_out // tn),
        in_specs=[pl.BlockSpec((rows, D_MODEL), lambda l, j: (0, 0)),
                  pl.BlockSpec((1, D_MODEL, tn), lambda l, j: (l, 0, j)),
                  pl.BlockSpec((1, 1, tn), lambda l, j: (l, 0, j))],
        out_specs=pl.BlockSpec((1, rows, tn), lambda l, j: (l, 0, j)),
        out_shape=jax.ShapeDtypeStruct((DEPTH, rows, n_out), F32),
        compiler_params=_params(2),
        name="ada_mods",
    )(cvec, w_ada, b_ada.reshape(DEPTH, 1, n_out))


def _inproj_kernel(h_ref, mod_ref, g_ref, w_ref, cos_ref, sin_ref,
                   q_ref, k_ref, v_ref, cb_ref, z_ref, u_ref):
    mod = mod_ref[0]
    a = _rms(h_ref[0], g_ref[0:1, :]) * (1.0 + mod[1:2, :]) + mod[0:1, :]
    p = jnp.dot(a.astype(BF16), w_ref[...], preferred_element_type=F32)
    lane = lax.broadcasted_iota(jnp.int32, (1, LANES), 1)
    first_of_pair = (lane & ROPE_PAIRS) == 0
    for s in range(ROPE_WIDTH // LANES):
        sl = slice(s * LANES, (s + 1) * LANES)
        t = p[:, sl]
        partner = jnp.where(first_of_pair,
                            pltpu.roll(t, LANES - ROPE_PAIRS, axis=1),
                            pltpu.roll(t, ROPE_PAIRS, axis=1))
        r = (t * cos_ref[:, sl] + partner * sin_ref[:, sl]).astype(BF16)
        if s < ATTN_WIDTH // LANES:
            q_ref[0, :, sl] = r
        else:
            k_ref[0] = r
    o = ROPE_WIDTH
    v_ref[0] = p[:, o:o + KV_WIDTH].astype(BF16)
    o += KV_WIDTH
    cb_ref[0] = p[:, o:o + CONV_WIDTH]
    o += CONV_WIDTH
    z_ref[0] = p[:, o:o + CONV_WIDTH] * p[:, o + CONV_WIDTH:o + 2 * CONV_WIDTH]
    o += 2 * CONV_WIDTH
    u_ref[0] = p[:, o:o + SSM_WIDTH]


def _in_proj(h, mods, g, w_in, cos, sin, tm):
    b, n, _ = h.shape
    widths = [(ATTN_WIDTH, BF16), (KV_WIDTH, BF16), (KV_WIDTH, BF16),
              (CONV_WIDTH, F32), (CONV_WIDTH, F32), (SSM_WIDTH, F32)]
    return pl.pallas_call(
        _inproj_kernel,
        grid=(n // tm, b),
        in_specs=[pl.BlockSpec((1, tm, D_MODEL), lambda j, i: (i, j, 0)),
                  pl.BlockSpec((1, N_MOD, D_MODEL), lambda j, i: (i, 0, 0)),
                  _const_spec((4, D_MODEL)),
                  _const_spec((D_MODEL, IN_WIDTH)),
                  pl.BlockSpec((tm, ROPE_WIDTH), lambda j, i: (j, 0)),
                  pl.BlockSpec((tm, ROPE_WIDTH), lambda j, i: (j, 0))],
        out_specs=[pl.BlockSpec((1, tm, w), lambda j, i: (i, j, 0)) for w, _ in widths],
        out_shape=[jax.ShapeDtypeStruct((b, n, w), dt) for w, dt in widths],
        compiler_params=_params(2),
        name="in_proj",
    )(h, mods, g, w_in, cos, sin)


def _attn_kernel(*refs, tq, has_lat):
    if has_lat:
        (sink_ref, q_ref, kp_ref, kc_ref, kn_ref, vp_ref, vc_ref, vn_ref,
         kx_ref, vx_ref, bias_ref, o_ref) = refs
        kall = jnp.concatenate([kp_ref[0], kc_ref[0], kn_ref[0], kx_ref[0]], axis=0)
        vall = jnp.concatenate([vp_ref[0], vc_ref[0], vn_ref[0], vx_ref[0]], axis=0)
    else:
        sink_ref, q_ref, kx_ref, vx_ref, o_ref = refs
        kall, vall = kx_ref[0], vx_ref[0]
    n_keys = kall.shape[0]
    q = q_ref[0]

    outs = []
    for hk in range(N_KV_HEADS):
        ksl = slice(hk * HEAD_DIM, (hk + 1) * HEAD_DIM)
        q4 = jnp.concatenate(
            [q[:, (hk * Q_PER_KV + g) * HEAD_DIM:(hk * Q_PER_KV + g + 1) * HEAD_DIM]
             for g in range(Q_PER_KV)], axis=0)
        s = lax.dot_general(q4, kall[:, ksl], (((1,), (1,)), ((), ())),
                            preferred_element_type=F32)
        if has_lat:
            s = (s.reshape(Q_PER_KV, tq, n_keys) + bias_ref[0][None]).reshape(Q_PER_KV * tq, n_keys)
        sink = jnp.concatenate(
            [jnp.full((tq, 1), sink_ref[hk * Q_PER_KV + g] * LOG2E, F32) for g in range(Q_PER_KV)],
            axis=0)
        m = jnp.maximum(jnp.max(s, axis=-1, keepdims=True), sink)
        p = jnp.exp2(s - m)
        denom = jnp.sum(p, axis=-1, keepdims=True) + jnp.exp2(sink - m)
        o = jnp.dot(p.astype(BF16), vall[:, ksl], preferred_element_type=F32) / denom
        outs += [o[g * tq:(g + 1) * tq, :] for g in range(Q_PER_KV)]
    o_ref[0] = jnp.concatenate(outs, axis=1).astype(BF16)


def _window_bias(tq, n_ctx):
    r = np.arange(tq)[:, None]
    c = np.arange(3 * tq + n_ctx)[None, :]
    band = (c >= r) & (c <= r + 2 * WINDOW)
    variants = []
    for first_col, end_col in ((tq, 3 * tq), (0, 3 * tq), (0, 2 * tq)):
        ok = (c >= 3 * tq) | (band & (c >= first_col) & (c < end_col))
        variants.append(np.where(ok, 0.0, NEG_INF))
    return jnp.asarray(np.stack(variants), F32)


def _attention_lat(sink, q, k, v, kx, vx):
    b, n, _ = q.shape
    n_ctx = kx.shape[1]
    tq = Q_BLOCK
    nb = n // tq
    assert tq == WINDOW and nb >= 2
    kv_spec = lambda f: pl.BlockSpec((1, tq, KV_WIDTH), f)
    prev = lambda i, j: (i, jnp.maximum(j - 1, 0), 0)
    cur = lambda i, j: (i, j, 0)
    nxt = lambda i, j: (i, jnp.minimum(j + 1, nb - 1), 0)
    ctx_spec = pl.BlockSpec((1, n_ctx, KV_WIDTH), lambda i, j: (i, 0, 0))
    bias_spec = pl.BlockSpec((1, tq, 3 * tq + n_ctx),
                             lambda i, j: (jnp.where(j == 0, 0, jnp.where(j == nb - 1, 2, 1)), 0, 0))
    return pl.pallas_call(
        functools.partial(_attn_kernel, tq=tq, has_lat=True),
        grid=(b, nb),
        in_specs=[pl.BlockSpec(memory_space=pltpu.SMEM),
                  pl.BlockSpec((1, tq, ATTN_WIDTH), cur),
                  kv_spec(prev), kv_spec(cur), kv_spec(nxt),
                  kv_spec(prev), kv_spec(cur), kv_spec(nxt),
                  ctx_spec, ctx_spec, bias_spec],
        out_specs=pl.BlockSpec((1, tq, ATTN_WIDTH), cur),
        out_shape=jax.ShapeDtypeStruct((b, n, ATTN_WIDTH), BF16),
        compiler_params=_params(2),
        name="attn_lat",
    )(sink, q, k, k, k, v, v, v, kx, vx, _window_bias(tq, n_ctx))


def _attention_ctx(sink, q, kx, vx):
    b, n, _ = q.shape
    tq = n
    ctx_spec = pl.BlockSpec((1, n, KV_WIDTH), lambda i: (i, 0, 0))
    return pl.pallas_call(
        functools.partial(_attn_kernel, tq=tq, has_lat=False),
        grid=(b,),
        in_specs=[pl.BlockSpec(memory_space=pltpu.SMEM),
                  pl.BlockSpec((1, tq, ATTN_WIDTH), lambda i: (i, 0, 0)),
                  ctx_spec, ctx_spec],
        out_specs=pl.BlockSpec((1, tq, ATTN_WIDTH), lambda i: (i, 0, 0)),
        out_shape=jax.ShapeDtypeStruct((b, n, ATTN_WIDTH), BF16),
        compiler_params=_params(1),
        name="attn_ctx",
    )(sink, q, kx, vx)


def _s5_exit_kernel(u_ref, we_ref, e_ref):
    half = e_ref.shape[1] // 2
    for gp in range(S5_PAIRS):
        r = jnp.dot(u_ref[:, gp * PAIR_IN:(gp + 1) * PAIR_IN], we_ref[gp],
                    preferred_element_type=F32)
        e_ref[:, gp * PAIR_STATE:(gp + 1) * PAIR_STATE] = r[:, :PAIR_STATE]
        e_ref[:, half + gp * PAIR_STATE:half + (gp + 1) * PAIR_STATE] = r[:, PAIR_STATE:]


def _s5_carry_kernel(e_ref, lr_ref, li_ref, p_ref, state, *, bsz):
    @pl.when(pl.program_id(1) == 0)
    def _():
        state[...] = jnp.zeros_like(state)

    def run(chunks):
        for gp in range(S5_PAIRS):
            re = slice(gp * PAIR_STATE, gp * PAIR_STATE + LANES)
            im = slice(gp * PAIR_STATE + LANES, (gp + 1) * PAIR_STATE)
            lr = lr_ref[0, :, gp * LANES:(gp + 1) * LANES]
            li = li_ref[0, :, gp * LANES:(gp + 1) * LANES]
            sr, si = state[:, re], state[:, im]
            for k in chunks:
                rows = slice(k * bsz, (k + 1) * bsz)
                p_ref[rows, re] = sr
                p_ref[rows, im] = si
                sr, si = (lr * sr - li * si + e_ref[rows, re],
                          lr * si + li * sr + e_ref[rows, im])
            state[:, re] = sr
            state[:, im] = si

    n_chunks = e_ref.shape[0] // bsz
    direction = pl.program_id(0)
    pl.when(direction == 0)(lambda: run(tuple(range(n_chunks))))
    pl.when(direction == 1)(lambda: run(tuple(reversed(range(n_chunks)))))


def _s5_out_kernel(u_ref, p_ref, toep_ref, csf_ref, csb_ref, y_ref):
    half = p_ref.shape[1] // 2
    grp = PAIR_IN // 2
    for gp in range(S5_PAIRS):
        pf = p_ref[:, gp * PAIR_STATE:(gp + 1) * PAIR_STATE].astype(BF16)
        pb = p_ref[:, half + gp * PAIR_STATE:half + (gp + 1) * PAIR_STATE].astype(BF16)
        st = (jnp.dot(pf, csf_ref[gp], preferred_element_type=F32)
              + jnp.dot(pb, csb_ref[gp], preferred_element_type=F32))
        for a in range(2):
            cols = slice(gp * PAIR_IN + a * grp, gp * PAIR_IN + (a + 1) * grp)
            y_ref[:, cols] = (jnp.dot(u_ref[:, cols], toep_ref[2 * gp + a],
                                      preferred_element_type=F32)
                              + st[:, a * grp:(a + 1) * grp])


def _s5(u2, bsz, n_ctx_chunks, weights):
    toep, we, csf, csb, lr, li = weights
    n_rows, width = u2.shape
    tr = S5_BLOCK_CHUNKS * bsz
    nb, nbc = n_rows // tr, n_ctx_chunks // S5_BLOCK_CHUNKS
    n_state = 2 * S5_PAIRS * PAIR_STATE
    row_blk = lambda w: pl.BlockSpec((tr, w), lambda j: (j, 0))
    exits = pl.pallas_call(
        _s5_exit_kernel,
        grid=(nb,),
        in_specs=[row_blk(width), _const_spec(we.shape)],
        out_specs=row_blk(n_state),
        out_shape=jax.ShapeDtypeStruct((n_rows, n_state), F32),
        compiler_params=_params(1),
        name="s5_exit",
    )(u2, we)

    def blk(d, k):
        back = jnp.where(k < nbc, nbc - 1 - k, nb + nbc - 1 - k)
        return jnp.where(d == 0, k, back)

    dir_blk = pl.BlockSpec((tr, n_state // 2), lambda d, k: (blk(d, k), d))
    lam_spec = pl.BlockSpec((1, 1, n_state // 4), lambda d, k: (d, 0, 0))
    entering = pl.pallas_call(
        functools.partial(_s5_carry_kernel, bsz=bsz),
        grid=(2, nb),
        in_specs=[dir_blk, lam_spec, lam_spec],
        out_specs=dir_blk,
        out_shape=jax.ShapeDtypeStruct((n_rows, n_state), F32),
        scratch_shapes=[pltpu.VMEM((bsz, n_state // 2), F32)],
        compiler_params=_params(2),
        name="s5_carry",
    )(exits, lr, li)

    return pl.pallas_call(
        _s5_out_kernel,
        grid=(nb,),
        in_specs=[row_blk(width), row_blk(n_state), _const_spec(toep.shape),
                  _const_spec(csf.shape), _const_spec(csb.shape)],
        out_specs=row_blk(width),
        out_shape=jax.ShapeDtypeStruct((n_rows, width), F32),
        compiler_params=_params(1),
        name="s5_out",
    )(u2, entering, toep, csf, csb)


def _pair_diag(m):
    g, r, c = m.shape
    eye = jnp.eye(2, dtype=m.dtype)
    return jnp.einsum('qarc,ab->qarbc', m.reshape(g // 2, 2, r, c), eye).reshape(g // 2, 2 * r, 2 * c)


def _s5_weights(lam_re, lam_im, log_dt, b_re, b_im, c_re, c_im):
    t = S5_T
    lam = lax.complex(lam_re.astype(F32), lam_im.astype(F32))
    lam_dt = lam * jnp.exp(log_dt.astype(F32))[..., None]
    lam_bar = jnp.exp(lam_dt)
    bb = ((lam_bar - 1) / lam)[..., None] * lax.complex(b_re.astype(F32), b_im.astype(F32))
    cm = lax.complex(c_re.astype(F32), c_im.astype(F32))
    steps = jnp.arange(t + 1, dtype=F32)
    pw = jnp.exp(lam_dt[None] * steps[:, None, None, None])

    kern = jnp.real(jnp.einsum('dgop,kdgp,dgpi->dkgoi', cm, pw[:t], bb))
    s_idx = jnp.arange(t)[:, None]
    t_idx = jnp.arange(t)[None, :]
    lag = t_idx - s_idx
    kf = jnp.where((lag >= 0)[..., None, None, None], kern[0][jnp.maximum(lag, 0)], 0.0)
    kb = jnp.where((lag <= 0)[..., None, None, None], kern[1][jnp.maximum(-lag, 0)], 0.0)
    toep = jnp.transpose(kf + kb, (2, 0, 4, 1, 3))
    toep = toep.reshape(SSM_GROUPS, t * SSM_GROUP, t * SSM_GROUP)

    ef = pw[t - 1 - jnp.arange(t), 0][..., None] * bb[0][None]
    eb = pw[jnp.arange(t), 1][..., None] * bb[1][None]
    to_cols = lambda m: jnp.transpose(m, (1, 0, 3, 2)).reshape(SSM_GROUPS, t * SSM_GROUP, SSM_STATE)
    we = jnp.concatenate([_pair_diag(to_cols(part(m))) for m in (ef, eb)
                          for part in (jnp.real, jnp.imag)], axis=-1)

    cf = cm[0][None] * pw[1 + jnp.arange(t), 0][:, :, None, :]
    cb = cm[1][None] * pw[t - jnp.arange(t), 1][:, :, None, :]
    to_rows = lambda m: jnp.transpose(m, (1, 3, 0, 2)).reshape(SSM_GROUPS, SSM_STATE, t * SSM_GROUP)
    cs = [jnp.concatenate([_pair_diag(to_rows(jnp.real(m))), _pair_diag(to_rows(-jnp.imag(m)))],
                          axis=1) for m in (cf, cb)]

    lam_t = pw[t].reshape(2, 1, N_STATE)
    return (toep.astype(BF16), we.astype(BF16), cs[0].astype(BF16), cs[1].astype(BF16),
            jnp.real(lam_t), jnp.imag(lam_t))


def _to_chunk_rows(u, t):
    b, n, _ = u.shape
    u = u.reshape(b, n // t, t, SSM_GROUPS, SSM_GROUP)
    return jnp.transpose(u, (1, 0, 3, 2, 4)).reshape(n // t * b, SSM_WIDTH * t)


def _from_chunk_rows(y, b, t):
    n_chunks = y.shape[0] // b
    y = y.reshape(n_chunks, b, SSM_GROUPS, t, SSM_GROUP)
    return jnp.transpose(y, (1, 0, 3, 2, 4)).reshape(b, n_chunks * t, SSM_WIDTH)


def _block_kernel(h_ref, attn_ref, cb_ref, z_ref, zp_ref, zn_ref, y_ref, u_ref, mod_ref, g_ref,
                  cw_ref, dsk_ref, wglu_ref, bglu_ref, wout_ref, w1_ref, w2_ref, o_ref, *, tm):
    j = pl.program_id(0)
    mod = mod_ref[0]
    z = z_ref[0]
    has_prev = jnp.where(j > 0, 1.0, 0.0)
    has_next = jnp.where(j < pl.num_programs(0) - 1, 1.0, 0.0)
    z_before = zp_ref[0][SUBLANES - 1:SUBLANES, :] * has_prev
    z_after = zn_ref[0][0:1, :] * has_next
    row = lax.broadcasted_iota(jnp.int32, (tm, 1), 0)
    z_dn = jnp.where(row == 0, z_before, pltpu.roll(z, 1, axis=0))
    z_up = jnp.where(row == tm - 1, z_after, pltpu.roll(z, tm - 1, axis=0))
    conv = cb_ref[0] * (z_dn * cw_ref[0:1, :] + z * cw_ref[1:2, :] + z_up * cw_ref[2:3, :])
    y = y_ref[0] + dsk_ref[...] * u_ref[0]
    gl = 0.5 * y * (1.0 + jnp.tanh(math.sqrt(2.0 / math.pi) * (y + 0.044715 * (y * y * y))))
    gate = jnp.dot(gl.astype(BF16), wglu_ref[...], preferred_element_type=F32) + bglu_ref[...]
    ssm = gl * _sigmoid(gate)
    a0, a1 = ATTN_WIDTH, ATTN_WIDTH + CONV_WIDTH
    m = (jnp.dot(attn_ref[0], wout_ref[0:a0, :], preferred_element_type=F32)
         + jnp.dot(conv.astype(BF16), wout_ref[a0:a1, :], preferred_element_type=F32)
         + jnp.dot(ssm.astype(BF16), wout_ref[a1:, :], preferred_element_type=F32))
    h1 = h_ref[0] + mod[2:3, :] * _rms(m, g_ref[1:2, :])
    a2 = (_rms(h1, g_ref[2:3, :]) * (1.0 + mod[4:5, :]) + mod[3:4, :]).astype(BF16)
    f = jnp.zeros((tm, D_MODEL), F32)
    for cidx in range(D_FF // FF_CHUNK):
        cs = slice(cidx * FF_CHUNK, (cidx + 1) * FF_CHUNK)
        t = jnp.maximum(jnp.dot(a2, w1_ref[:, cs], preferred_element_type=F32), 0.0)
        f = f + jnp.dot((t * t).astype(BF16), w2_ref[cs, :], preferred_element_type=F32)
    o_ref[0] = h1 + mod[5:6, :] * _rms(f, g_ref[3:4, :])


def _block(h, attn, cb, z, y, u, mods, g, conv_w, d_skip, w_glu, b_glu, w_out, w1, w2, tm):
    b, n, _ = h.shape
    nt = n // tm
    per = tm // SUBLANES
    n8 = n // SUBLANES
    row = lambda w: pl.BlockSpec((1, tm, w), lambda j, i: (i, j, 0))
    return pl.pallas_call(
        functools.partial(_block_kernel, tm=tm),
        grid=(nt, b),
        in_specs=[row(D_MODEL), row(ATTN_WIDTH), row(CONV_WIDTH), row(CONV_WIDTH),
                  pl.BlockSpec((1, SUBLANES, CONV_WIDTH),
                               lambda j, i: (i, jnp.maximum(j * per - 1, 0), 0)),
                  pl.BlockSpec((1, SUBLANES, CONV_WIDTH),
                               lambda j, i: (i, jnp.minimum((j + 1) * per, n8 - 1), 0)),
                  row(SSM_WIDTH), row(SSM_WIDTH),
                  pl.BlockSpec((1, N_MOD, D_MODEL), lambda j, i: (i, 0, 0)),
                  _const_spec((4, D_MODEL)),
                  _const_spec((3, CONV_WIDTH)),
                  _const_spec((1, SSM_WIDTH)),
                  _const_spec((SSM_WIDTH, SSM_WIDTH)),
                  _const_spec((1, SSM_WIDTH)),
                  _const_spec((MIX_WIDTH, D_MODEL)),
                  _const_spec((D_MODEL, D_FF)),
                  _const_spec((D_FF, D_MODEL))],
        out_specs=row(D_MODEL),
        out_shape=jax.ShapeDtypeStruct((b, n, D_MODEL), F32),
        compiler_params=_params(2),
        name="mix_mlp_block",
    )(h, attn, cb, z, z, z, y, u, mods, g, conv_w, d_skip, w_glu, b_glu, w_out, w1, w2)


def _rope_tables(n_lat, n_ctx):
    rows = n_lat // GRID_W
    row = jnp.broadcast_to(jnp.arange(rows)[:, None], (rows, GRID_W)).reshape(-1)
    col = jnp.broadcast_to(jnp.arange(GRID_W)[None, :], (rows, GRID_W)).reshape(-1)
    freqs = ROPE_BASE ** (-jnp.arange(ROPE_PAIRS, dtype=F32) / ROPE_PAIRS)
    ang_r = row[:, None].astype(F32) * freqs
    ang_c = col[:, None].astype(F32) * freqs
    cos_h = jnp.concatenate([jnp.cos(ang_r)] * 2 + [jnp.cos(ang_c)] * 2, axis=-1)
    sin_h = jnp.concatenate([-jnp.sin(ang_r), jnp.sin(ang_r),
                             -jnp.sin(ang_c), jnp.sin(ang_c)], axis=-1)
    scale = jnp.concatenate([jnp.full((ATTN_WIDTH,), LOG2E * HEAD_DIM ** -0.5, F32),
                             jnp.ones((KV_WIDTH,), F32)])
    n_heads = N_Q_HEADS + N_KV_HEADS
    cos = jnp.tile(cos_h, (1, n_heads)) * scale
    sin = jnp.tile(sin_h, (1, n_heads)) * scale
    cos_c = jnp.broadcast_to(scale, (n_ctx, ROPE_WIDTH))
    sin_c = jnp.zeros((n_ctx, ROPE_WIDTH), F32)
    return cos, sin, cos_c, sin_c


def kernel(x, c, ctx, c_ctx, w_ada, b_ada, norm_g, w_in, conv_w, attn_sink, ssm_lam_re, ssm_lam_im,
           ssm_log_dt, ssm_b_re, ssm_b_im, ssm_c_re, ssm_c_im, ssm_d, w_glu, b_glu, w_out,
           w_mlp_in, w_mlp_out):
    bsz, n_lat, _ = x.shape
    n_ctx = ctx.shape[1]
    tm_lat = min(ROW_TILE, n_lat)
    tm_ctx = min(ROW_TILE, n_ctx)
    cos, sin, cos_c, sin_c = _rope_tables(n_lat, n_ctx)

    pad = (-(bsz + 1)) % SUBLANES
    cvec = jnp.concatenate([c, c_ctx[None, :], jnp.zeros((pad, D_MODEL), F32)], axis=0)
    mods_all = _ada_mods(cvec, w_ada, b_ada)

    h, hc = x, ctx
    for l in range(DEPTH):
        with_ctx_out = l < DEPTH - 1
        mods = mods_all[l, :bsz].reshape(bsz, N_MOD, D_MODEL)
        mods_c = jnp.broadcast_to(mods_all[l, bsz].reshape(1, N_MOD, D_MODEL),
                                  (bsz, N_MOD, D_MODEL))
        g = norm_g[l]
        w_in_l = w_in[l].astype(BF16)
        q, k, v, cb, z, u = _in_proj(h, mods, g, w_in_l, cos, sin, tm_lat)
        qc, kc, vc, cbc, zc, uc = _in_proj(hc, mods_c, g, w_in_l, cos_c, sin_c, tm_ctx)

        attn = _attention_lat(attn_sink[l], q, k, v, kc, vc)

        s5w = _s5_weights(ssm_lam_re[l], ssm_lam_im[l], ssm_log_dt[l], ssm_b_re[l],
                          ssm_b_im[l], ssm_c_re[l], ssm_c_im[l])
        u2 = jnp.concatenate([_to_chunk_rows(uc, S5_T), _to_chunk_rows(u, S5_T)], axis=0).astype(BF16)
        y2 = _s5(u2, bsz, n_ctx // S5_T, s5w)
        n_ctx_rows = n_ctx // S5_T * bsz
        y = _from_chunk_rows(y2[n_ctx_rows:], bsz, S5_T)

        shared = (g, conv_w[l], ssm_d[l].reshape(1, SSM_WIDTH), w_glu[l].astype(BF16),
                  b_glu[l].reshape(1, SSM_WIDTH), w_out[l].astype(BF16),
                  w_mlp_in[l].astype(BF16), w_mlp_out[l].astype(BF16))
        h = _block(h, attn, cb, z, y, u, mods, *shared, tm_lat)
        if with_ctx_out:
            attn_c = _attention_ctx(attn_sink[l], qc, kc, vc)
            y_c = _from_chunk_rows(y2[:n_ctx_rows], bsz, S5_T)
            hc = _block(hc, attn_c, cbc, zc, y_c, uc, mods_c, *shared, tm_ctx)
    return h
```

```python
import functools
import math

import numpy as np
import jax
import jax.numpy as jnp
from jax import lax
from jax.experimental import pallas as pl
from jax.experimental.pallas import tpu as pltpu

F32 = jnp.float32
BF16 = jnp.bfloat16

D_MODEL = 1024
DEPTH = 4
GRID_W = 64
HEAD_DIM = 64
N_Q_HEADS = 8
N_KV_HEADS = 2
Q_PER_KV = N_Q_HEADS // N_KV_HEADS
ATTN_WIDTH = N_Q_HEADS * HEAD_DIM
KV_WIDTH = N_KV_HEADS * HEAD_DIM
WINDOW = 128
Q_BLOCK = 128
ROPE_BASE = 10000.0
ROPE_PAIRS = HEAD_DIM // 4
CONV_WIDTH = 256
SSM_WIDTH = 256
SSM_GROUP = 16
SSM_GROUPS = SSM_WIDTH // SSM_GROUP
SSM_STATE = 64
N_STATE = SSM_GROUPS * SSM_STATE
MIX_WIDTH = ATTN_WIDTH + CONV_WIDTH + SSM_WIDTH
IN_WIDTH = ATTN_WIDTH + 2 * KV_WIDTH + 3 * CONV_WIDTH + SSM_WIDTH
ROPE_WIDTH = ATTN_WIDTH + KV_WIDTH
D_FF = 4 * D_MODEL
N_MOD = 6
EPS = 1e-6
NEG_INF = -1e30
LOG2E = math.log2(math.e)

LANES = 128
SUBLANES = 8
VMEM_LIMIT = 56 * 1024 * 1024

ROW_TILE = 512
S5_T = 16
S5_BLOCK_CHUNKS = 8
S5_ROW_TILE = 256
S5_PAIRS = SSM_GROUPS // 2
PAIR_IN = 2 * SSM_GROUP * S5_T
PAIR_STATE = 4 * SSM_STATE
FF_CHUNK = 1024


def _const_spec(shape):
    nd = len(shape)
    return pl.BlockSpec(shape, lambda *_: (0,) * nd, pipeline_mode=pl.Buffered(1))


def _params(n_grid):
    return pltpu.CompilerParams(dimension_semantics=("arbitrary",) * n_grid,
                                vmem_limit_bytes=VMEM_LIMIT)


def _rms(x, g):
    ms = jnp.mean(x * x, axis=-1, keepdims=True)
    return x * lax.rsqrt(ms + EPS) * g


def _sigmoid(x):
    return 1.0 / (1.0 + jnp.exp(-x))


def _ada_kernel(c_ref, w_ref, b_ref, o_ref):
    c = c_ref[...]
    act = (c * _sigmoid(c)).astype(BF16)
    o_ref[0] = jnp.dot(act, w_ref[0].astype(BF16), preferred_element_type=F32) + b_ref[0]


def _ada_mods(cvec, w_ada, b_ada):
    rows = cvec.shape[0]
    tn = 1024
    n_out = N_MOD * D_MODEL
    return pl.pallas_call(
        _ada_kernel,
        grid=(DEPTH, n_out // tn),
        in_specs=[pl.BlockSpec((rows, D_MODEL), lambda l, j: (0, 0)),
                  pl.BlockSpec((1, D_MODEL, tn), lambda l, j: (l, 0, j)),
                  pl.BlockSpec((1, 1, tn), lambda l, j: (l, 0, j))],
        out_specs=pl.BlockSpec((1, rows, tn), lambda l, j: (l, 0, j)),
        out_shape=jax.ShapeDtypeStruct((DEPTH, rows, n_out), F32),
        compiler_params=_params(2),
        name="ada_mods",
    )(cvec, w_ada, b_ada.reshape(DEPTH, 1, n_out))


def _piece_transpose(src):
    per = LANES // SSM_GROUP
    n = src[0][0].shape[0]
    slot = lax.broadcasted_iota(jnp.int32, (n, LANES), 1) // SSM_GROUP

    def moved(a, bh, k):
        return src[a][bh] if k == 0 else pltpu.roll(src[a][bh], k * SSM_GROUP, axis=1)

    dst = []
    for b in range(len(src)):
        bh, bl = divmod(b, per)
        halves = []
        for ah in range(len(src) // per):
            out = moved(ah * per, bh, -bl % per)
            for j in range(1, per):
                out = jnp.where(slot == j, moved(ah * per + j, bh, (j - bl) % per), out)
            halves.append(out)
        dst.append(halves)
    return dst


def _inproj_kernel(h_ref, mod_ref, g_ref, w_ref, cos_ref, sin_ref,
                   q_ref, k_ref, v_ref, cb_ref, z_ref, u_ref, u2_ref, u_scr):
    mod = mod_ref[0]
    a = _rms(h_ref[0], g_ref[0:1, :]) * (1.0 + mod[1:2, :]) + mod[0:1, :]
    p = jnp.dot(a.astype(BF16), w_ref[...], preferred_element_type=F32)
    lane = lax.broadcasted_iota(jnp.int32, (1, LANES), 1)
    first_of_pair = (lane & ROPE_PAIRS) == 0
    for s in range(ROPE_WIDTH // LANES):
        sl = slice(s * LANES, (s + 1) * LANES)
        t = p[:, sl]
        partner = jnp.where(first_of_pair,
                            pltpu.roll(t, LANES - ROPE_PAIRS, axis=1),
                            pltpu.roll(t, ROPE_PAIRS, axis=1))
        r = (t * cos_ref[:, sl] + partner * sin_ref[:, sl]).astype(BF16)
        if s < ATTN_WIDTH // LANES:
            q_ref[0, :, sl] = r
        else:
            k_ref[0] = r
    o = ROPE_WIDTH
    v_ref[0] = p[:, o:o + KV_WIDTH].astype(BF16)
    o += KV_WIDTH
    cb_ref[0] = p[:, o:o + CONV_WIDTH]
    o += CONV_WIDTH
    z_ref[0] = p[:, o:o + CONV_WIDTH] * p[:, o + CONV_WIDTH:o + 2 * CONV_WIDTH]
    o += 2 * CONV_WIDTH
    u_ref[0] = p[:, o:o + SSM_WIDTH]
    n_chunks = u2_ref.shape[0]
    halves = SSM_WIDTH // LANES
    for hf in range(halves):
        u_scr[hf] = p[:, o + hf * LANES:o + (hf + 1) * LANES]
    steps = [[u_scr[hf, pl.ds(s, n_chunks, stride=S5_T), :]
              for hf in range(halves)] for s in range(S5_T)]
    groups = _piece_transpose(steps)
    for gi in range(SSM_GROUPS):
        for hf in range(halves):
            c0 = gi * SSM_GROUP * S5_T + hf * LANES
            u2_ref[:, c0:c0 + LANES] = groups[gi][hf].astype(BF16)


def _in_proj(h, mods, g, w_in, cos, sin, tm):
    b, n, _ = h.shape
    nt = n // tm
    widths = [(ATTN_WIDTH, BF16), (KV_WIDTH, BF16), (KV_WIDTH, BF16),
              (CONV_WIDTH, F32), (CONV_WIDTH, F32), (SSM_WIDTH, F32)]
    chunk_rows = pl.BlockSpec((tm // S5_T, S5_T * SSM_WIDTH), lambda j, i: (i * nt + j, 0))
    return pl.pallas_call(
        _inproj_kernel,
        grid=(nt, b),
        in_specs=[pl.BlockSpec((1, tm, D_MODEL), lambda j, i: (i, j, 0)),
                  pl.BlockSpec((1, N_MOD, D_MODEL), lambda j, i: (i, 0, 0)),
                  _const_spec((4, D_MODEL)),
                  _const_spec((D_MODEL, IN_WIDTH)),
                  pl.BlockSpec((tm, ROPE_WIDTH), lambda j, i: (j, 0)),
                  pl.BlockSpec((tm, ROPE_WIDTH), lambda j, i: (j, 0))],
        out_specs=[pl.BlockSpec((1, tm, w), lambda j, i: (i, j, 0)) for w, _ in widths]
        + [chunk_rows],
        out_shape=[jax.ShapeDtypeStruct((b, n, w), dt) for w, dt in widths]
        + [jax.ShapeDtypeStruct((b * n // S5_T, S5_T * SSM_WIDTH), BF16)],
        scratch_shapes=[pltpu.VMEM((SSM_WIDTH // LANES, tm, LANES), F32)],
        compiler_params=_params(2),
        name="in_proj",
    )(h, mods, g, w_in, cos, sin)


def _attn_kernel(*refs, tq, has_lat):
    if has_lat:
        (sink_ref, q_ref, kp_ref, kc_ref, kn_ref, vp_ref, vc_ref, vn_ref,
         kx_ref, vx_ref, bias_ref, o_ref) = refs
        kall = jnp.concatenate([kp_ref[0], kc_ref[0], kn_ref[0], kx_ref[0]], axis=0)
        vall = jnp.concatenate([vp_ref[0], vc_ref[0], vn_ref[0], vx_ref[0]], axis=0)
    else:
        sink_ref, q_ref, kx_ref, vx_ref, o_ref = refs
        kall, vall = kx_ref[0], vx_ref[0]
    n_keys = kall.shape[0]
    q = q_ref[0]

    outs = []
    for hk in range(N_KV_HEADS):
        ksl = slice(hk * HEAD_DIM, (hk + 1) * HEAD_DIM)
        q4 = jnp.concatenate(
            [q[:, (hk * Q_PER_KV + g) * HEAD_DIM:(hk * Q_PER_KV + g + 1) * HEAD_DIM]
             for g in range(Q_PER_KV)], axis=0)
        s = lax.dot_general(q4, kall[:, ksl], (((1,), (1,)), ((), ())),
                            preferred_element_type=F32)
        if has_lat:
            s = (s.reshape(Q_PER_KV, tq, n_keys) + bias_ref[0][None]).reshape(Q_PER_KV * tq, n_keys)
        sink = jnp.concatenate(
            [jnp.full((tq, 1), sink_ref[hk * Q_PER_KV + g] * LOG2E, F32) for g in range(Q_PER_KV)],
            axis=0)
        m = jnp.maximum(jnp.max(s, axis=-1, keepdims=True), sink)
        p = jnp.exp2(s - m)
        denom = jnp.sum(p, axis=-1, keepdims=True) + jnp.exp2(sink - m)
        o = jnp.dot(p.astype(BF16), vall[:, ksl], preferred_element_type=F32) / denom
        outs += [o[g * tq:(g + 1) * tq, :] for g in range(Q_PER_KV)]
    o_ref[0] = jnp.concatenate(outs, axis=1).astype(BF16)


def _window_bias(tq, n_ctx):
    r = np.arange(tq)[:, None]
    c = np.arange(3 * tq + n_ctx)[None, :]
    band = (c >= r) & (c <= r + 2 * WINDOW)
    variants = []
    for first_col, end_col in ((tq, 3 * tq), (0, 3 * tq), (0, 2 * tq)):
        ok = (c >= 3 * tq) | (band & (c >= first_col) & (c < end_col))
        variants.append(np.where(ok, 0.0, NEG_INF))
    return jnp.asarray(np.stack(variants), F32)


def _attention_lat(sink, q, k, v, kx, vx):
    b, n, _ = q.shape
    n_ctx = kx.shape[1]
    tq = Q_BLOCK
    nb = n // tq
    assert tq == WINDOW and nb >= 2
    kv_spec = lambda f: pl.BlockSpec((1, tq, KV_WIDTH), f)
    prev = lambda i, j: (i, jnp.maximum(j - 1, 0), 0)
    cur = lambda i, j: (i, j, 0)
    nxt = lambda i, j: (i, jnp.minimum(j + 1, nb - 1), 0)
    ctx_spec = pl.BlockSpec((1, n_ctx, KV_WIDTH), lambda i, j: (i, 0, 0))
    bias_spec = pl.BlockSpec((1, tq, 3 * tq + n_ctx),
                             lambda i, j: (jnp.where(j == 0, 0, jnp.where(j == nb - 1, 2, 1)), 0, 0))
    return pl.pallas_call(
        functools.partial(_attn_kernel, tq=tq, has_lat=True),
        grid=(b, nb),
        in_specs=[pl.BlockSpec(memory_space=pltpu.SMEM),
                  pl.BlockSpec((1, tq, ATTN_WIDTH), cur),
                  kv_spec(prev), kv_spec(cur), kv_spec(nxt),
                  kv_spec(prev), kv_spec(cur), kv_spec(nxt),
                  ctx_spec, ctx_spec, bias_spec],
        out_specs=pl.BlockSpec((1, tq, ATTN_WIDTH), cur),
        out_shape=jax.ShapeDtypeStruct((b, n, ATTN_WIDTH), BF16),
        compiler_params=_params(2),
        name="attn_lat",
    )(sink, q, k, k, k, v, v, v, kx, vx, _window_bias(tq, n_ctx))


def _attention_ctx(sink, q, kx, vx):
    b, n, _ = q.shape
    tq = n
    ctx_spec = pl.BlockSpec((1, n, KV_WIDTH), lambda i: (i, 0, 0))
    return pl.pallas_call(
        functools.partial(_attn_kernel, tq=tq, has_lat=False),
        grid=(b,),
        in_specs=[pl.BlockSpec(memory_space=pltpu.SMEM),
                  pl.BlockSpec((1, tq, ATTN_WIDTH), lambda i: (i, 0, 0)),
                  ctx_spec, ctx_spec],
        out_specs=pl.BlockSpec((1, tq, ATTN_WIDTH), lambda i: (i, 0, 0)),
        out_shape=jax.ShapeDtypeStruct((b, n, ATTN_WIDTH), BF16),
        compiler_params=_params(1),
        name="attn_ctx",
    )(sink, q, kx, vx)


def _s5_exit_kernel(u_ref, we_ref, e_ref):
    half = e_ref.shape[1] // 2
    for gp in range(S5_PAIRS):
        r = jnp.dot(u_ref[:, gp * PAIR_IN:(gp + 1) * PAIR_IN], we_ref[gp],
                    preferred_element_type=F32)
        e_ref[:, gp * PAIR_STATE:(gp + 1) * PAIR_STATE] = r[:, :PAIR_STATE]
        e_ref[:, half + gp * PAIR_STATE:half + (gp + 1) * PAIR_STATE] = r[:, PAIR_STATE:]


def _s5_carry_kernel(e_ref, init_ref, lr_ref, li_ref, p_ref, fin_ref, state):
    @pl.when(pl.program_id(1) == 0)
    def _():
        state[...] = init_ref[0]

    def run(chunks):
        for gp in range(S5_PAIRS):
            re = slice(gp * PAIR_STATE, gp * PAIR_STATE + LANES)
            im = slice(gp * PAIR_STATE + LANES, (gp + 1) * PAIR_STATE)
            lr = lr_ref[0, :, gp * LANES:(gp + 1) * LANES]
            li = li_ref[0, :, gp * LANES:(gp + 1) * LANES]
            sr, si = state[:, re], state[:, im]
            for k in chunks:
                p_ref[:, k, re] = sr
                p_ref[:, k, im] = si
                sr, si = (lr * sr - li * si + e_ref[:, k, re],
                          lr * si + li * sr + e_ref[:, k, im])
            state[:, re] = sr
            state[:, im] = si

    n_chunks = e_ref.shape[1]
    direction = pl.program_id(0)
    pl.when(direction == 0)(lambda: run(tuple(range(n_chunks))))
    pl.when(direction == 1)(lambda: run(tuple(reversed(range(n_chunks)))))
    fin_ref[0] = state[...]


def _s5_out_kernel(u_ref, p_ref, toep_ref, csf_ref, csb_ref, y_ref):
    half = p_ref.shape[1] // 2
    grp = PAIR_IN // 2
    for gp in range(S5_PAIRS):
        pf = p_ref[:, gp * PAIR_STATE:(gp + 1) * PAIR_STATE].astype(BF16)
        pb = p_ref[:, half + gp * PAIR_STATE:half + (gp + 1) * PAIR_STATE].astype(BF16)
        st = (jnp.dot(pf, csf_ref[gp], preferred_element_type=F32)
              + jnp.dot(pb, csb_ref[gp], preferred_element_type=F32))
        for a in range(2):
            cols = slice(gp * PAIR_IN + a * grp, gp * PAIR_IN + (a + 1) * grp)
            y_ref[:, cols] = (jnp.dot(u_ref[:, cols], toep_ref[2 * gp + a],
                                      preferred_element_type=F32)
                              + st[:, a * grp:(a + 1) * grp])


def _s5_segment(u2, bsz, weights, init):
    toep, we, csf, csb, lr, li = weights
    n_rows, width = u2.shape
    nc = n_rows // bsz
    tr = min(S5_ROW_TILE, n_rows)
    n_state = 2 * S5_PAIRS * PAIR_STATE
    row_blk = lambda w: pl.BlockSpec((tr, w), lambda j: (j, 0))
    exits = pl.pallas_call(
        _s5_exit_kernel,
        grid=(n_rows // tr,),
        in_specs=[row_blk(width), _const_spec(we.shape)],
        out_specs=row_blk(n_state),
        out_shape=jax.ShapeDtypeStruct((n_rows, n_state), F32),
        compiler_params=_params(1),
        name="s5_exit",
    )(u2, we)

    nb = nc // S5_BLOCK_CHUNKS
    dir_blk = pl.BlockSpec((bsz, S5_BLOCK_CHUNKS, n_state // 2),
                           lambda d, k: (0, jnp.where(d == 0, k, nb - 1 - k), d))
    state_spec = pl.BlockSpec((1, bsz, n_state // 2), lambda d, k: (d, 0, 0))
    lam_spec = pl.BlockSpec((1, 1, n_state // 4), lambda d, k: (d, 0, 0))
    entering, leaving = pl.pallas_call(
        _s5_carry_kernel,
        grid=(2, nb),
        in_specs=[dir_blk, state_spec, lam_spec, lam_spec],
        out_specs=[dir_blk, state_spec],
        out_shape=[jax.ShapeDtypeStruct((bsz, nc, n_state), F32),
                   jax.ShapeDtypeStruct((2, bsz, n_state // 2), F32)],
        scratch_shapes=[pltpu.VMEM((bsz, n_state // 2), F32)],
        compiler_params=_params(2),
        name="s5_carry",
    )(exits.reshape(bsz, nc, n_state), init, lr, li)

    y2 = pl.pallas_call(
        _s5_out_kernel,
        grid=(n_rows // tr,),
        in_specs=[row_blk(width), row_blk(n_state), _const_spec(toep.shape),
                  _const_spec(csf.shape), _const_spec(csb.shape)],
        out_specs=row_blk(width),
        out_shape=jax.ShapeDtypeStruct((n_rows, width), F32),
        compiler_params=_params(1),
        name="s5_out",
    )(u2, entering.reshape(n_rows, n_state), toep, csf, csb)
    return y2, leaving


def _pair_diag(m):
    g, r, c = m.shape
    eye = jnp.eye(2, dtype=m.dtype)
    return jnp.einsum('qarc,ab->qarbc', m.reshape(g // 2, 2, r, c), eye).reshape(g // 2, 2 * r, 2 * c)


def _s5_weights(lam_re, lam_im, log_dt, b_re, b_im, c_re, c_im):
    t = S5_T
    lam = lax.complex(lam_re.astype(F32), lam_im.astype(F32))
    lam_dt = lam * jnp.exp(log_dt.astype(F32))[..., None]
    lam_bar = jnp.exp(lam_dt)
    bb = ((lam_bar - 1) / lam)[..., None] * lax.complex(b_re.astype(F32), b_im.astype(F32))
    cm = lax.complex(c_re.astype(F32), c_im.astype(F32))
    steps = jnp.arange(t + 1, dtype=F32)
    pw = jnp.exp(lam_dt[None] * steps[:, None, None, None])

    kern = jnp.real(jnp.einsum('dgop,kdgp,dgpi->dkgoi', cm, pw[:t], bb))
    s_idx = jnp.arange(t)[:, None]
    t_idx = jnp.arange(t)[None, :]
    lag = t_idx - s_idx
    kf = jnp.where((lag >= 0)[..., None, None, None], kern[0][jnp.maximum(lag, 0)], 0.0)
    kb = jnp.where((lag <= 0)[..., None, None, None], kern[1][jnp.maximum(-lag, 0)], 0.0)
    toep = jnp.transpose(kf + kb, (2, 0, 4, 1, 3))
    toep = toep.reshape(SSM_GROUPS, t * SSM_GROUP, t * SSM_GROUP)

    ef = pw[t - 1 - jnp.arange(t), 0][..., None] * bb[0][None]
    eb = pw[jnp.arange(t), 1][..., None] * bb[1][None]
    to_cols = lambda m: jnp.transpose(m, (1, 0, 3, 2)).reshape(SSM_GROUPS, t * SSM_GROUP, SSM_STATE)
    we = jnp.concatenate([_pair_diag(to_cols(part(m))) for m in (ef, eb)
                          for part in (jnp.real, jnp.imag)], axis=-1)

    cf = cm[0][None] * pw[1 + jnp.arange(t), 0][:, :, None, :]
    cb = cm[1][None] * pw[t - jnp.arange(t), 1][:, :, None, :]
    to_rows = lambda m: jnp.transpose(m, (1, 3, 0, 2)).reshape(SSM_GROUPS, SSM_STATE, t * SSM_GROUP)
    cs = [jnp.concatenate([_pair_diag(to_rows(jnp.real(m))), _pair_diag(to_rows(-jnp.imag(m)))],
                          axis=1) for m in (cf, cb)]

    lam_t = pw[t].reshape(2, 1, N_STATE)
    return (toep.astype(BF16), we.astype(BF16), cs[0].astype(BF16), cs[1].astype(BF16),
            jnp.real(lam_t), jnp.imag(lam_t))


def _block_kernel(h_ref, attn_ref, cb_ref, z_ref, zp_ref, zn_ref, y2_ref, u_ref, mod_ref, g_ref,
                  cw_ref, dsk_ref, wglu_ref, bglu_ref, wout_ref, w1_ref, w2_ref, o_ref, y_scr,
                  *, tm):
    j = pl.program_id(0)
    mod = mod_ref[0]
    z = z_ref[0]
    has_prev = jnp.where(j > 0, 1.0, 0.0)
    has_next = jnp.where(j < pl.num_programs(0) - 1, 1.0, 0.0)
    z_before = zp_ref[0][SUBLANES - 1:SUBLANES, :] * has_prev
    z_after = zn_ref[0][0:1, :] * has_next
    row = lax.broadcasted_iota(jnp.int32, (tm, 1), 0)
    z_dn = jnp.where(row == 0, z_before, pltpu.roll(z, 1, axis=0))
    z_up = jnp.where(row == tm - 1, z_after, pltpu.roll(z, tm - 1, axis=0))
    conv = cb_ref[0] * (z_dn * cw_ref[0:1, :] + z * cw_ref[1:2, :] + z_up * cw_ref[2:3, :])
    n_chunks = y2_ref.shape[0]
    halves = S5_T * SSM_GROUP // LANES
    groups = [[y2_ref[:, (gi * halves + hf) * LANES:(gi * halves + hf + 1) * LANES]
               for hf in range(halves)] for gi in range(SSM_GROUPS)]
    steps = _piece_transpose(groups)
    for s in range(S5_T):
        for hf in range(SSM_WIDTH // LANES):
            y_scr[hf, pl.ds(s, n_chunks, stride=S5_T), :] = steps[s][hf]
    y = jnp.concatenate([y_scr[hf] for hf in range(SSM_WIDTH // LANES)], axis=1)
    y = y + dsk_ref[...] * u_ref[0]
    gl = 0.5 * y * (1.0 + jnp.tanh(math.sqrt(2.0 / math.pi) * (y + 0.044715 * (y * y * y))))
    gate = jnp.dot(gl.astype(BF16), wglu_ref[...], preferred_element_type=F32) + bglu_ref[...]
    ssm = gl * _sigmoid(gate)
    a0, a1 = ATTN_WIDTH, ATTN_WIDTH + CONV_WIDTH
    m = (jnp.dot(attn_ref[0], wout_ref[0:a0, :], preferred_element_type=F32)
         + jnp.dot(conv.astype(BF16), wout_ref[a0:a1, :], preferred_element_type=F32)
         + jnp.dot(ssm.astype(BF16), wout_ref[a1:, :], preferred_element_type=F32))
    h1 = h_ref[0] + mod[2:3, :] * _rms(m, g_ref[1:2, :])
    a2 = (_rms(h1, g_ref[2:3, :]) * (1.0 + mod[4:5, :]) + mod[3:4, :]).astype(BF16)
    f = jnp.zeros((tm, D_MODEL), F32)
    for cidx in range(D_FF // FF_CHUNK):
        cs = slice(cidx * FF_CHUNK, (cidx + 1) * FF_CHUNK)
        t = jnp.maximum(jnp.dot(a2, w1_ref[:, cs], preferred_element_type=F32), 0.0)
        f = f + jnp.dot((t * t).astype(BF16), w2_ref[cs, :], preferred_element_type=F32)
    o_ref[0] = h1 + mod[5:6, :] * _rms(f, g_ref[3:4, :])


def _block(h, attn, cb, z, y2, u, mods, g, conv_w, d_skip, w_glu, b_glu, w_out, w1, w2, tm):
    b, n, _ = h.shape
    nt = n // tm
    per = tm // SUBLANES
    n8 = n // SUBLANES
    row = lambda w: pl.BlockSpec((1, tm, w), lambda j, i: (i, j, 0))
    chunk_rows = pl.BlockSpec((tm // S5_T, S5_T * SSM_WIDTH), lambda j, i: (i * nt + j, 0))
    return pl.pallas_call(
        functools.partial(_block_kernel, tm=tm),
        grid=(nt, b),
        in_specs=[row(D_MODEL), row(ATTN_WIDTH), row(CONV_WIDTH), row(CONV_WIDTH),
                  pl.BlockSpec((1, SUBLANES, CONV_WIDTH),
                               lambda j, i: (i, jnp.maximum(j * per - 1, 0), 0)),
                  pl.BlockSpec((1, SUBLANES, CONV_WIDTH),
                               lambda j, i: (i, jnp.minimum((j + 1) * per, n8 - 1), 0)),
                  chunk_rows, row(SSM_WIDTH),
                  pl.BlockSpec((1, N_MOD, D_MODEL), lambda j, i: (i, 0, 0)),
                  _const_spec((4, D_MODEL)),
                  _const_spec((3, CONV_WIDTH)),
                  _const_spec((1, SSM_WIDTH)),
                  _const_spec((SSM_WIDTH, SSM_WIDTH)),
                  _const_spec((1, SSM_WIDTH)),
                  _const_spec((MIX_WIDTH, D_MODEL)),
                  _const_spec((D_MODEL, D_FF)),
                  _const_spec((D_FF, D_MODEL))],
        out_specs=row(D_MODEL),
        out_shape=jax.ShapeDtypeStruct((b, n, D_MODEL), F32),
        scratch_shapes=[pltpu.VMEM((SSM_WIDTH // LANES, tm, LANES), F32)],
        compiler_params=_params(2),
        name="mix_mlp_block",
    )(h, attn, cb, z, z, z, y2, u, mods, g, conv_w, d_skip, w_glu, b_glu, w_out, w1, w2)


def _rope_tables(n_lat, n_ctx):
    rows = n_lat // GRID_W
    row = jnp.broadcast_to(jnp.arange(rows)[:, None], (rows, GRID_W)).reshape(-1)
    col = jnp.broadcast_to(jnp.arange(GRID_W)[None, :], (rows, GRID_W)).reshape(-1)
    freqs = ROPE_BASE ** (-jnp.arange(ROPE_PAIRS, dtype=F32) / ROPE_PAIRS)
    ang_r = row[:, None].astype(F32) * freqs
    ang_c = col[:, None].astype(F32) * freqs
    cos_h = jnp.concatenate([jnp.cos(ang_r)] * 2 + [jnp.cos(ang_c)] * 2, axis=-1)
    sin_h = jnp.concatenate([-jnp.sin(ang_r), jnp.sin(ang_r),
                             -jnp.sin(ang_c), jnp.sin(ang_c)], axis=-1)
    scale = jnp.concatenate([jnp.full((ATTN_WIDTH,), LOG2E * HEAD_DIM ** -0.5, F32),
                             jnp.ones((KV_WIDTH,), F32)])
    n_heads = N_Q_HEADS + N_KV_HEADS
    cos = jnp.tile(cos_h, (1, n_heads)) * scale
    sin = jnp.tile(sin_h, (1, n_heads)) * scale
    cos_c = jnp.broadcast_to(scale, (n_ctx, ROPE_WIDTH))
    sin_c = jnp.zeros((n_ctx, ROPE_WIDTH), F32)
    return cos, sin, cos_c, sin_c


def kernel(x, c, ctx, c_ctx, w_ada, b_ada, norm_g, w_in, conv_w, attn_sink, ssm_lam_re, ssm_lam_im,
           ssm_log_dt, ssm_b_re, ssm_b_im, ssm_c_re, ssm_c_im, ssm_d, w_glu, b_glu, w_out,
           w_mlp_in, w_mlp_out):
    bsz, n_lat, _ = x.shape
    n_ctx = ctx.shape[1]
    tm_lat = min(ROW_TILE, n_lat)
    tm_ctx = min(ROW_TILE, n_ctx)
    cos, sin, cos_c, sin_c = _rope_tables(n_lat, n_ctx)

    pad = (-(bsz + 1)) % SUBLANES
    cvec = jnp.concatenate([c, c_ctx[None, :], jnp.zeros((pad, D_MODEL), F32)], axis=0)
    mods_all = _ada_mods(cvec, w_ada, b_ada)

    h, hc = x, ctx
    for l in range(DEPTH):
        with_ctx_out = l < DEPTH - 1
        mods = mods_all[l, :bsz].reshape(bsz, N_MOD, D_MODEL)
        mods_c = jnp.broadcast_to(mods_all[l, bsz].reshape(1, N_MOD, D_MODEL),
                                  (bsz, N_MOD, D_MODEL))
        g = norm_g[l]
        w_in_l = w_in[l].astype(BF16)
        q, k, v, cb, z, u, u2 = _in_proj(h, mods, g, w_in_l, cos, sin, tm_lat)
        qc, kc, vc, cbc, zc, uc, u2c = _in_proj(hc, mods_c, g, w_in_l, cos_c, sin_c, tm_ctx)

        attn = _attention_lat(attn_sink[l], q, k, v, kc, vc)

        s5w = _s5_weights(ssm_lam_re[l], ssm_lam_im[l], ssm_log_dt[l], ssm_b_re[l],
                          ssm_b_im[l], ssm_c_re[l], ssm_c_im[l])
        zero_state = jnp.zeros((2, bsz, 2 * N_STATE), F32)
        y2c, ctx_state = _s5_segment(u2c, bsz, s5w, zero_state)
        y2, _ = _s5_segment(u2, bsz, s5w, ctx_state)

        shared = (g, conv_w[l], ssm_d[l].reshape(1, SSM_WIDTH), w_glu[l].astype(BF16),
                  b_glu[l].reshape(1, SSM_WIDTH), w_out[l].astype(BF16),
                  w_mlp_in[l].astype(BF16), w_mlp_out[l].astype(BF16))
        h = _block(h, attn, cb, z, y2, u, mods, *shared, tm_lat)
        if with_ctx_out:
            attn_c = _attention_ctx(attn_sink[l], qc, kc, vc)
            hc = _block(hc, attn_c, cbc, zc, y2c, uc, mods_c, *shared, tm_ctx)
    return h
```

```python
import functools
import math

import numpy as np
import jax
import jax.numpy as jnp
from jax import lax
from jax.experimental import pallas as pl
from jax.experimental.pallas import tpu as pltpu

F32 = jnp.float32
BF16 = jnp.bfloat16

D_MODEL = 1024
DEPTH = 4
GRID_W = 64
HEAD_DIM = 64
N_Q_HEADS = 8
N_KV_HEADS = 2
Q_PER_KV = N_Q_HEADS // N_KV_HEADS
ATTN_WIDTH = N_Q_HEADS * HEAD_DIM
KV_WIDTH = N_KV_HEADS * HEAD_DIM
WINDOW = 128
Q_BLOCK = 128
ROPE_BASE = 10000.0
ROPE_PAIRS = HEAD_DIM // 4
CONV_WIDTH = 256
SSM_WIDTH = 256
SSM_GROUP = 16
SSM_GROUPS = SSM_WIDTH // SSM_GROUP
SSM_STATE = 64
N_STATE = SSM_GROUPS * SSM_STATE
MIX_WIDTH = ATTN_WIDTH + CONV_WIDTH + SSM_WIDTH
IN_WIDTH = ATTN_WIDTH + 2 * KV_WIDTH + 3 * CONV_WIDTH + SSM_WIDTH
ROPE_WIDTH = ATTN_WIDTH + KV_WIDTH
D_FF = 4 * D_MODEL
N_MOD = 6
EPS = 1e-6
NEG_INF = -1e30
LOG2E = math.log2(math.e)

LANES = 128
SUBLANES = 8
VMEM_LIMIT = 56 * 1024 * 1024

ROW_TILE = 512
S5_T = 16
S5_BLOCK_CHUNKS = 8
S5_ROW_TILE = 256
S5_PAIRS = SSM_GROUPS // 2
PAIR_IN = 2 * SSM_GROUP * S5_T
PAIR_STATE = 4 * SSM_STATE
FF_CHUNK = 1024


def _const_spec(shape):
    nd = len(shape)
    return pl.BlockSpec(shape, lambda *_: (0,) * nd, pipeline_mode=pl.Buffered(1))


def _params(n_grid):
    return pltpu.CompilerParams(dimension_semantics=("arbitrary",) * n_grid,
                                vmem_limit_bytes=VMEM_LIMIT)


def _rms(x, g):
    ms = jnp.mean(x * x, axis=-1, keepdims=True)
    return x * lax.rsqrt(ms + EPS) * g


def _sigmoid(x):
    return 1.0 / (1.0 + jnp.exp(-x))


def _ada_kernel(c_ref, w_ref, b_ref, o_ref):
    c = c_ref[...]
    act = (c * _sigmoid(c)).astype(BF16)
    o_ref[0] = jnp.dot(act, w_ref[0].astype(BF16), preferred_element_type=F32) + b_ref[0]


def _ada_mods(cvec, w_ada, b_ada):
    rows = cvec.shape[0]
    tn = 1024
    n_out = N_MOD * D_MODEL
    return pl.pallas_call(
        _ada_kernel,
        grid=(DEPTH, n_out // tn),
        in_specs=[pl.BlockSpec((rows, D_MODEL), lambda l, j: (0, 0)),
                  pl.BlockSpec((1, D_MODEL, tn), lambda l, j: (l, 0, j)),
                  pl.BlockSpec((1, 1, tn), lambda l, j: (l, 0, j))],
        out_specs=pl.BlockSpec((1, rows, tn), lambda l, j: (l, 0, j)),
        out_shape=jax.ShapeDtypeStruct((DEPTH, rows, n_out), F32),
        compiler_params=_params(2),
        name="ada_mods",
    )(cvec, w_ada, b_ada.reshape(DEPTH, 1, n_out))


def _piece_transpose(src):
    per = LANES // SSM_GROUP
    n = src[0][0].shape[0]
    slot = lax.broadcasted_iota(jnp.int32, (n, LANES), 1) // SSM_GROUP

    def moved(a, bh, k):
        return src[a][bh] if k == 0 else pltpu.roll(src[a][bh], k * SSM_GROUP, axis=1)

    dst = []
    for b in range(len(src)):
        bh, bl = divmod(b, per)
        halves = []
        for ah in range(len(src) // per):
            out = moved(ah * per, bh, -bl % per)
            for j in range(1, per):
                out = jnp.where(slot == j, moved(ah * per + j, bh, (j - bl) % per), out)
            halves.append(out)
        dst.append(halves)
    return dst


def _inproj_kernel(*refs, lat):
    h_ref, mod_ref, g_ref, w_ref, cos_ref, sin_ref = refs[:6]
    if lat:
        qe_ref, qo_ref, k2_ref, v4_ref, cb_ref, z_ref, u_ref, u2_ref, u_scr = refs[6:]
    else:
        q_ref, k_ref, v_ref, k2_ref, v4_ref, cb_ref, z_ref, u_ref, u2_ref, u_scr = refs[6:]
    mod = mod_ref[0]
    a = _rms(h_ref[0], g_ref[0:1, :]) * (1.0 + mod[1:2, :]) + mod[0:1, :]
    p = jnp.dot(a.astype(BF16), w_ref[...], preferred_element_type=F32)
    lane = lax.broadcasted_iota(jnp.int32, (1, LANES), 1)
    first_of_pair = (lane & ROPE_PAIRS) == 0
    low_head = lane < HEAD_DIM
    for s in range(ROPE_WIDTH // LANES):
        sl = slice(s * LANES, (s + 1) * LANES)
        t = p[:, sl]
        partner = jnp.where(first_of_pair,
                            pltpu.roll(t, LANES - ROPE_PAIRS, axis=1),
                            pltpu.roll(t, ROPE_PAIRS, axis=1))
        r = t * cos_ref[:, sl] + partner * sin_ref[:, sl]
        if s < ATTN_WIDTH // LANES:
            if lat:
                qe_ref[0, :, sl] = jnp.where(low_head, r, 0.0).astype(BF16)
                qo_ref[0, :, sl] = jnp.where(low_head, 0.0, r).astype(BF16)
            else:
                q_ref[0, :, sl] = r.astype(BF16)
        else:
            k2_ref[0, :, 0:LANES] = r.astype(BF16)
            k2_ref[0, :, LANES:2 * LANES] = pltpu.roll(r, HEAD_DIM, axis=1).astype(BF16)
            if not lat:
                k_ref[0] = r.astype(BF16)
    o = ROPE_WIDTH
    v = p[:, o:o + KV_WIDTH]
    v_swapped = pltpu.roll(v, HEAD_DIM, axis=1)
    for idx, part in enumerate((jnp.where(low_head, v, 0.0), jnp.where(low_head, 0.0, v_swapped),
                                jnp.where(low_head, v_swapped, 0.0), jnp.where(low_head, 0.0, v))):
        v4_ref[0, :, idx * LANES:(idx + 1) * LANES] = part.astype(BF16)
    if not lat:
        v_ref[0] = v.astype(BF16)
    o += KV_WIDTH
    cb_ref[0] = p[:, o:o + CONV_WIDTH]
    o += CONV_WIDTH
    z_ref[0] = p[:, o:o + CONV_WIDTH] * p[:, o + CONV_WIDTH:o + 2 * CONV_WIDTH]
    o += 2 * CONV_WIDTH
    u_ref[0] = p[:, o:o + SSM_WIDTH]
    n_chunks = u2_ref.shape[0]
    halves = SSM_WIDTH // LANES
    for hf in range(halves):
        u_scr[hf] = p[:, o + hf * LANES:o + (hf + 1) * LANES]
    steps = [[u_scr[hf, pl.ds(s, n_chunks, stride=S5_T), :]
              for hf in range(halves)] for s in range(S5_T)]
    groups = _piece_transpose(steps)
    for gi in range(SSM_GROUPS):
        for hf in range(halves):
            c0 = gi * SSM_GROUP * S5_T + hf * LANES
            u2_ref[:, c0:c0 + LANES] = groups[gi][hf].astype(BF16)


def _in_proj(h, mods, g, w_in, cos, sin, tm, lat):
    b, n, _ = h.shape
    nt = n // tm
    if lat:
        widths = [(ATTN_WIDTH, BF16), (ATTN_WIDTH, BF16)]
    else:
        widths = [(ATTN_WIDTH, BF16), (KV_WIDTH, BF16), (KV_WIDTH, BF16)]
    widths += [(2 * KV_WIDTH, BF16), (4 * KV_WIDTH, BF16),
               (CONV_WIDTH, F32), (CONV_WIDTH, F32), (SSM_WIDTH, F32)]
    chunk_rows = pl.BlockSpec((tm // S5_T, S5_T * SSM_WIDTH), lambda j, i: (i * nt + j, 0))
    return pl.pallas_call(
        functools.partial(_inproj_kernel, lat=lat),
        grid=(nt, b),
        in_specs=[pl.BlockSpec((1, tm, D_MODEL), lambda j, i: (i, j, 0)),
                  pl.BlockSpec((1, N_MOD, D_MODEL), lambda j, i: (i, 0, 0)),
                  _const_spec((4, D_MODEL)),
                  _const_spec((D_MODEL, IN_WIDTH)),
                  pl.BlockSpec((tm, ROPE_WIDTH), lambda j, i: (j, 0)),
                  pl.BlockSpec((tm, ROPE_WIDTH), lambda j, i: (j, 0))],
        out_specs=[pl.BlockSpec((1, tm, w), lambda j, i: (i, j, 0)) for w, _ in widths]
        + [chunk_rows],
        out_shape=[jax.ShapeDtypeStruct((b, n, w), dt) for w, dt in widths]
        + [jax.ShapeDtypeStruct((b * n // S5_T, S5_T * SSM_WIDTH), BF16)],
        scratch_shapes=[pltpu.VMEM((SSM_WIDTH // LANES, tm, LANES), F32)],
        compiler_params=_params(2),
        name="in_proj",
    )(h, mods, g, w_in, cos, sin)


def _softmax_parts(s, sink):
    m = jnp.maximum(jnp.max(s, axis=-1, keepdims=True), sink)
    p = jnp.exp2(s - m)
    denom = jnp.sum(p, axis=-1, keepdims=True) + jnp.exp2(sink - m)
    return p, 1.0 / denom


def _attn_ctx_kernel(sink_ref, q_ref, k_ref, v_ref, o_ref):
    q, k, v = q_ref[0], k_ref[0], v_ref[0]
    outs = []
    for h in range(N_Q_HEADS):
        hk = h // Q_PER_KV
        ksl = slice(hk * HEAD_DIM, (hk + 1) * HEAD_DIM)
        s = lax.dot_general(q[:, h * HEAD_DIM:(h + 1) * HEAD_DIM], k[:, ksl],
                            (((1,), (1,)), ((), ())), preferred_element_type=F32)
        p, inv = _softmax_parts(s, sink_ref[h] * LOG2E)
        outs.append(jnp.dot(p.astype(BF16), v[:, ksl], preferred_element_type=F32) * inv)
    o_ref[0] = jnp.concatenate(outs, axis=1).astype(BF16)


def _attn_lat_kernel(sink_ref, qe_ref, qo_ref, kp_ref, kc_ref, kn_ref, kx_ref,
                     vp_ref, vc_ref, vn_ref, vx_ref, bias_ref, o_ref, s_scr, p_scr, inv_scr, *, tq):
    n = pl.program_id(0)

    @pl.when(n == 0)
    def _():
        s_scr[...] = jnp.zeros_like(s_scr)
        p_scr[...] = jnp.zeros_like(p_scr)
        inv_scr[...] = jnp.zeros_like(inv_scr)

    lane = lax.broadcasted_iota(jnp.int32, (1, LANES), 1)
    low_head = lane < HEAD_DIM
    slabs = ATTN_WIDTH // LANES

    def step(new, old):
        vall = jnp.concatenate([vp_ref[0], vc_ref[0], vn_ref[0], vx_ref[0]], axis=0)
        for t in range(slabs):
            c0 = (t // (slabs // N_KV_HEADS)) * 2 * LANES
            o = (jnp.dot(p_scr[new, 2 * t], vall[:, c0:c0 + LANES], preferred_element_type=F32)
                 + jnp.dot(p_scr[new, 2 * t + 1], vall[:, c0 + LANES:c0 + 2 * LANES],
                           preferred_element_type=F32))
            o_ref[0, :, t * LANES:(t + 1) * LANES] = (o * inv_scr[new, t]).astype(BF16)

        kall = jnp.concatenate([kp_ref[0], kc_ref[0], kn_ref[0], kx_ref[0]], axis=0)
        for hk in range(N_KV_HEADS):
            t0 = hk * (slabs // N_KV_HEADS)
            for par, q_ref in enumerate((qe_ref, qo_ref)):
                lhs = jnp.concatenate([q_ref[0, :, t0 * LANES:(t0 + 1) * LANES],
                                       q_ref[0, :, (t0 + 1) * LANES:(t0 + 2) * LANES]], axis=0)
                kv = (hk + par) % 2
                s = lax.dot_general(lhs, kall[:, kv * LANES:(kv + 1) * LANES],
                                    (((1,), (1,)), ((), ())), preferred_element_type=F32)
                s_scr[new, 2 * t0 + par] = s[:tq]
                s_scr[new, 2 * t0 + 2 + par] = s[tq:]

        invs = []
        for h in range(N_Q_HEADS):
            p, inv = _softmax_parts(s_scr[old, h] + bias_ref[0], sink_ref[h] * LOG2E)
            p_scr[old, h] = p.astype(BF16)
            invs.append(inv)
        for t in range(slabs):
            inv_scr[old, t] = jnp.where(low_head, invs[2 * t], invs[2 * t + 1])

    for k in range(2):
        pl.when(n % 2 == k)(functools.partial(step, k, 1 - k))


def _window_bias(tq, n_ctx):
    r = np.arange(tq)[:, None]
    c = np.arange(3 * tq + n_ctx)[None, :]
    band = (c >= r) & (c <= r + 2 * WINDOW)
    variants = []
    for first_col, end_col in ((tq, 3 * tq), (0, 3 * tq), (0, 2 * tq)):
        ok = (c >= 3 * tq) | (band & (c >= first_col) & (c < end_col))
        variants.append(np.where(ok, 0.0, NEG_INF))
    return jnp.asarray(np.stack(variants), F32)


def _attention_lat(sink, qe, qo, k2, v4, k2x, v4x):
    b, n, _ = qe.shape
    n_ctx = k2x.shape[1]
    tq = Q_BLOCK
    nb = n // tq
    total = b * nb
    n_keys = 3 * tq + n_ctx
    assert tq == WINDOW and nb >= 2

    def block_of(step, lag):
        m = jnp.clip(step - lag, 0, total - 1)
        return m // nb, m % nb

    def rows(width, lag, shift=0):
        def index(step):
            i, j = block_of(step, lag)
            return i, jnp.clip(j + shift, 0, nb - 1), 0
        return pl.BlockSpec((1, tq, width), index)

    def ctx_rows(width, lag):
        return pl.BlockSpec((1, n_ctx, width), lambda step: (block_of(step, lag)[0], 0, 0))

    def bias_index(step):
        j = block_of(step, 1)[1]
        return jnp.where(j == 0, 0, jnp.where(j == nb - 1, 2, 1)), 0, 0

    kw, vw = 2 * KV_WIDTH, 4 * KV_WIDTH
    return pl.pallas_call(
        functools.partial(_attn_lat_kernel, tq=tq),
        grid=(total + 2,),
        in_specs=[pl.BlockSpec(memory_space=pltpu.SMEM),
                  rows(ATTN_WIDTH, 0), rows(ATTN_WIDTH, 0),
                  rows(kw, 0, -1), rows(kw, 0), rows(kw, 0, 1), ctx_rows(kw, 0),
                  rows(vw, 2, -1), rows(vw, 2), rows(vw, 2, 1), ctx_rows(vw, 2),
                  pl.BlockSpec((1, tq, n_keys), bias_index)],
        out_specs=rows(ATTN_WIDTH, 2),
        out_shape=jax.ShapeDtypeStruct((b, n, ATTN_WIDTH), BF16),
        scratch_shapes=[pltpu.VMEM((2, N_Q_HEADS, tq, n_keys), F32),
                        pltpu.VMEM((2, N_Q_HEADS, tq, n_keys), BF16),
                        pltpu.VMEM((2, ATTN_WIDTH // LANES, tq, LANES), F32)],
        compiler_params=_params(1),
        name="attn_lat",
    )(sink, qe, qo, k2, k2, k2, k2x, v4, v4, v4, v4x, _window_bias(tq, n_ctx))


def _attention_ctx(sink, q, k, v):
    b, n, _ = q.shape
    spec = lambda w: pl.BlockSpec((1, n, w), lambda i: (i, 0, 0))
    return pl.pallas_call(
        _attn_ctx_kernel,
        grid=(b,),
        in_specs=[pl.BlockSpec(memory_space=pltpu.SMEM), spec(ATTN_WIDTH), spec(KV_WIDTH),
                  spec(KV_WIDTH)],
        out_specs=spec(ATTN_WIDTH),
        out_shape=jax.ShapeDtypeStruct((b, n, ATTN_WIDTH), BF16),
        compiler_params=_params(1),
        name="attn_ctx",
    )(sink, q, k, v)


def _s5_exit_kernel(u_ref, we_ref, e_ref):
    half = e_ref.shape[1] // 2
    for gp in range(S5_PAIRS):
        r = jnp.dot(u_ref[:, gp * PAIR_IN:(gp + 1) * PAIR_IN], we_ref[gp],
                    preferred_element_type=F32)
        e_ref[:, gp * PAIR_STATE:(gp + 1) * PAIR_STATE] = r[:, :PAIR_STATE]
        e_ref[:, half + gp * PAIR_STATE:half + (gp + 1) * PAIR_STATE] = r[:, PAIR_STATE:]


def _s5_carry_kernel(e_ref, init_ref, lr_ref, li_ref, p_ref, fin_ref, state):
    @pl.when(pl.program_id(1) == 0)
    def _():
        state[...] = init_ref[0]

    def run(chunks):
        for gp in range(S5_PAIRS):
            re = slice(gp * PAIR_STATE, gp * PAIR_STATE + LANES)
            im = slice(gp * PAIR_STATE + LANES, (gp + 1) * PAIR_STATE)
            lr = lr_ref[0, :, gp * LANES:(gp + 1) * LANES]
            li = li_ref[0, :, gp * LANES:(gp + 1) * LANES]
            sr, si = state[:, re], state[:, im]
            for k in chunks:
                p_ref[:, k, re] = sr
                p_ref[:, k, im] = si
                sr, si = (lr * sr - li * si + e_ref[:, k, re],
                          lr * si + li * sr + e_ref[:, k, im])
            state[:, re] = sr
            state[:, im] = si

    n_chunks = e_ref.shape[1]
    direction = pl.program_id(0)
    pl.when(direction == 0)(lambda: run(tuple(range(n_chunks))))
    pl.when(direction == 1)(lambda: run(tuple(reversed(range(n_chunks)))))
    fin_ref[0] = state[...]


def _s5_out_kernel(u_ref, p_ref, toep_ref, csf_ref, csb_ref, y_ref):
    half = p_ref.shape[1] // 2
    grp = PAIR_IN // 2
    for gp in range(S5_PAIRS):
        pf = p_ref[:, gp * PAIR_STATE:(gp + 1) * PAIR_STATE].astype(BF16)
        pb = p_ref[:, half + gp * PAIR_STATE:half + (gp + 1) * PAIR_STATE].astype(BF16)
        st = (jnp.dot(pf, csf_ref[gp], preferred_element_type=F32)
              + jnp.dot(pb, csb_ref[gp], preferred_element_type=F32))
        for a in range(2):
            cols = slice(gp * PAIR_IN + a * grp, gp * PAIR_IN + (a + 1) * grp)
            y_ref[:, cols] = (jnp.dot(u_ref[:, cols], toep_ref[2 * gp + a],
                                      preferred_element_type=F32)
                              + st[:, a * grp:(a + 1) * grp])


def _s5_segment(u2, bsz, weights, init):
    toep, we, csf, csb, lr, li = weights
    n_rows, width = u2.shape
    nc = n_rows // bsz
    tr = min(S5_ROW_TILE, n_rows)
    n_state = 2 * S5_PAIRS * PAIR_STATE
    row_blk = lambda w: pl.BlockSpec((tr, w), lambda j: (j, 0))
    exits = pl.pallas_call(
        _s5_exit_kernel,
        grid=(n_rows // tr,),
        in_specs=[row_blk(width), _const_spec(we.shape)],
        out_specs=row_blk(n_state),
        out_shape=jax.ShapeDtypeStruct((n_rows, n_state), F32),
        compiler_params=_params(1),
        name="s5_exit",
    )(u2, we)

    nb = nc // S5_BLOCK_CHUNKS
    dir_blk = pl.BlockSpec((bsz, S5_BLOCK_CHUNKS, n_state // 2),
                           lambda d, k: (0, jnp.where(d == 0, k, nb - 1 - k), d))
    state_spec = pl.BlockSpec((1, bsz, n_state // 2), lambda d, k: (d, 0, 0))
    lam_spec = pl.BlockSpec((1, 1, n_state // 4), lambda d, k: (d, 0, 0))
    entering, leaving = pl.pallas_call(
        _s5_carry_kernel,
        grid=(2, nb),
        in_specs=[dir_blk, state_spec, lam_spec, lam_spec],
        out_specs=[dir_blk, state_spec],
        out_shape=[jax.ShapeDtypeStruct((bsz, nc, n_state), F32),
                   jax.ShapeDtypeStruct((2, bsz, n_state // 2), F32)],
        scratch_shapes=[pltpu.VMEM((bsz, n_state // 2), F32)],
        compiler_params=_params(2),
        name="s5_carry",
    )(exits.reshape(bsz, nc, n_state), init, lr, li)

    y2 = pl.pallas_call(
        _s5_out_kernel,
        grid=(n_rows // tr,),
        in_specs=[row_blk(width), row_blk(n_state), _const_spec(toep.shape),
                  _const_spec(csf.shape), _const_spec(csb.shape)],
        out_specs=row_blk(width),
        out_shape=jax.ShapeDtypeStruct((n_rows, width), F32),
        compiler_params=_params(1),
        name="s5_out",
    )(u2, entering.reshape(n_rows, n_state), toep, csf, csb)
    return y2, leaving


def _pair_diag(m):
    g, r, c = m.shape
    eye = jnp.eye(2, dtype=m.dtype)
    return jnp.einsum('qarc,ab->qarbc', m.reshape(g // 2, 2, r, c), eye).reshape(g // 2, 2 * r, 2 * c)


def _s5_weights(lam_re, lam_im, log_dt, b_re, b_im, c_re, c_im):
    t = S5_T
    lam = lax.complex(lam_re.astype(F32), lam_im.astype(F32))
    lam_dt = lam * jnp.exp(log_dt.astype(F32))[..., None]
    lam_bar = jnp.exp(lam_dt)
    bb = ((lam_bar - 1) / lam)[..., None] * lax.complex(b_re.astype(F32), b_im.astype(F32))
    cm = lax.complex(c_re.astype(F32), c_im.astype(F32))
    steps = jnp.arange(t + 1, dtype=F32)
    pw = jnp.exp(lam_dt[None] * steps[:, None, None, None])

    kern = jnp.real(jnp.einsum('dgop,kdgp,dgpi->dkgoi', cm, pw[:t], bb))
    s_idx = jnp.arange(t)[:, None]
    t_idx = jnp.arange(t)[None, :]
    lag = t_idx - s_idx
    kf = jnp.where((lag >= 0)[..., None, None, None], kern[0][jnp.maximum(lag, 0)], 0.0)
    kb = jnp.where((lag <= 0)[..., None, None, None], kern[1][jnp.maximum(-lag, 0)], 0.0)
    toep = jnp.transpose(kf + kb, (2, 0, 4, 1, 3))
    toep = toep.reshape(SSM_GROUPS, t * SSM_GROUP, t * SSM_GROUP)

    ef = pw[t - 1 - jnp.arange(t), 0][..., None] * bb[0][None]
    eb = pw[jnp.arange(t), 1][..., None] * bb[1][None]
    to_cols = lambda m: jnp.transpose(m, (1, 0, 3, 2)).reshape(SSM_GROUPS, t * SSM_GROUP, SSM_STATE)
    we = jnp.concatenate([_pair_diag(to_cols(part(m))) for m in (ef, eb)
                          for part in (jnp.real, jnp.imag)], axis=-1)

    cf = cm[0][None] * pw[1 + jnp.arange(t), 0][:, :, None, :]
    cb = cm[1][None] * pw[t - jnp.arange(t), 1][:, :, None, :]
    to_rows = lambda m: jnp.transpose(m, (1, 3, 0, 2)).reshape(SSM_GROUPS, SSM_STATE, t * SSM_GROUP)
    cs = [jnp.concatenate([_pair_diag(to_rows(jnp.real(m))), _pair_diag(to_rows(-jnp.imag(m)))],
                          axis=1) for m in (cf, cb)]

    lam_t = pw[t].reshape(2, 1, N_STATE)
    return (toep.astype(BF16), we.astype(BF16), cs[0].astype(BF16), cs[1].astype(BF16),
            jnp.real(lam_t), jnp.imag(lam_t))


def _block_kernel(h_ref, attn_ref, cb_ref, z_ref, zp_ref, zn_ref, y2_ref, u_ref, mod_ref, g_ref,
                  cw_ref, dsk_ref, wglu_ref, bglu_ref, wout_ref, w1_ref, w2_ref, o_ref, y_scr,
                  *, tm):
    j = pl.program_id(0)
    mod = mod_ref[0]
    z = z_ref[0]
    has_prev = jnp.where(j > 0, 1.0, 0.0)
    has_next = jnp.where(j < pl.num_programs(0) - 1, 1.0, 0.0)
    z_before = zp_ref[0][SUBLANES - 1:SUBLANES, :] * has_prev
    z_after = zn_ref[0][0:1, :] * has_next
    row = lax.broadcasted_iota(jnp.int32, (tm, 1), 0)
    z_dn = jnp.where(row == 0, z_before, pltpu.roll(z, 1, axis=0))
    z_up = jnp.where(row == tm - 1, z_after, pltpu.roll(z, tm - 1, axis=0))
    conv = cb_ref[0] * (z_dn * cw_ref[0:1, :] + z * cw_ref[1:2, :] + z_up * cw_ref[2:3, :])
    n_chunks = y2_ref.shape[0]
    halves = S5_T * SSM_GROUP // LANES
    groups = [[y2_ref[:, (gi * halves + hf) * LANES:(gi * halves + hf + 1) * LANES]
               for hf in range(halves)] for gi in range(SSM_GROUPS)]
    steps = _piece_transpose(groups)
    for s in range(S5_T):
        for hf in range(SSM_WIDTH // LANES):
            y_scr[hf, pl.ds(s, n_chunks, stride=S5_T), :] = steps[s][hf]
    y = jnp.concatenate([y_scr[hf] for hf in range(SSM_WIDTH // LANES)], axis=1)
    y = y + dsk_ref[...] * u_ref[0]
    gl = 0.5 * y * (1.0 + jnp.tanh(math.sqrt(2.0 / math.pi) * (y + 0.044715 * (y * y * y))))
    gate = jnp.dot(gl.astype(BF16), wglu_ref[...], preferred_element_type=F32) + bglu_ref[...]
    ssm = gl * _sigmoid(gate)
    a0, a1 = ATTN_WIDTH, ATTN_WIDTH + CONV_WIDTH
    m = (jnp.dot(attn_ref[0], wout_ref[0:a0, :], preferred_element_type=F32)
         + jnp.dot(conv.astype(BF16), wout_ref[a0:a1, :], preferred_element_type=F32)
         + jnp.dot(ssm.astype(BF16), wout_ref[a1:, :], preferred_element_type=F32))
    h1 = h_ref[0] + mod[2:3, :] * _rms(m, g_ref[1:2, :])
    a2 = (_rms(h1, g_ref[2:3, :]) * (1.0 + mod[4:5, :]) + mod[3:4, :]).astype(BF16)
    f = jnp.zeros((tm, D_MODEL), F32)
    for cidx in range(D_FF // FF_CHUNK):
        cs = slice(cidx * FF_CHUNK, (cidx + 1) * FF_CHUNK)
        t = jnp.maximum(jnp.dot(a2, w1_ref[:, cs], preferred_element_type=F32), 0.0)
        f = f + jnp.dot((t * t).astype(BF16), w2_ref[cs, :], preferred_element_type=F32)
    o_ref[0] = h1 + mod[5:6, :] * _rms(f, g_ref[3:4, :])


def _block(h, attn, cb, z, y2, u, mods, g, conv_w, d_skip, w_glu, b_glu, w_out, w1, w2, tm):
    b, n, _ = h.shape
    nt = n // tm
    per = tm // SUBLANES
    n8 = n // SUBLANES
    row = lambda w: pl.BlockSpec((1, tm, w), lambda j, i: (i, j, 0))
    chunk_rows = pl.BlockSpec((tm // S5_T, S5_T * SSM_WIDTH), lambda j, i: (i * nt + j, 0))
    return pl.pallas_call(
        functools.partial(_block_kernel, tm=tm),
        grid=(nt, b),
        in_specs=[row(D_MODEL), row(ATTN_WIDTH), row(CONV_WIDTH), row(CONV_WIDTH),
                  pl.BlockSpec((1, SUBLANES, CONV_WIDTH),
                               lambda j, i: (i, jnp.maximum(j * per - 1, 0), 0)),
                  pl.BlockSpec((1, SUBLANES, CONV_WIDTH),
                               lambda j, i: (i, jnp.minimum((j + 1) * per, n8 - 1), 0)),
                  chunk_rows, row(SSM_WIDTH),
                  pl.BlockSpec((1, N_MOD, D_MODEL), lambda j, i: (i, 0, 0)),
                  _const_spec((4, D_MODEL)),
                  _const_spec((3, CONV_WIDTH)),
                  _const_spec((1, SSM_WIDTH)),
                  _const_spec((SSM_WIDTH, SSM_WIDTH)),
                  _const_spec((1, SSM_WIDTH)),
                  _const_spec((MIX_WIDTH, D_MODEL)),
                  _const_spec((D_MODEL, D_FF)),
                  _const_spec((D_FF, D_MODEL))],
        out_specs=row(D_MODEL),
        out_shape=jax.ShapeDtypeStruct((b, n, D_MODEL), F32),
        scratch_shapes=[pltpu.VMEM((SSM_WIDTH // LANES, tm, LANES), F32)],
        compiler_params=_params(2),
        name="mix_mlp_block",
    )(h, attn, cb, z, z, z, y2, u, mods, g, conv_w, d_skip, w_glu, b_glu, w_out, w1, w2)


def _rope_tables(n_lat, n_ctx):
    rows = n_lat // GRID_W
    row = jnp.broadcast_to(jnp.arange(rows)[:, None], (rows, GRID_W)).reshape(-1)
    col = jnp.broadcast_to(jnp.arange(GRID_W)[None, :], (rows, GRID_W)).reshape(-1)
    freqs = ROPE_BASE ** (-jnp.arange(ROPE_PAIRS, dtype=F32) / ROPE_PAIRS)
    ang_r = row[:, None].astype(F32) * freqs
    ang_c = col[:, None].astype(F32) * freqs
    cos_h = jnp.concatenate([jnp.cos(ang_r)] * 2 + [jnp.cos(ang_c)] * 2, axis=-1)
    sin_h = jnp.concatenate([-jnp.sin(ang_r), jnp.sin(ang_r),
                             -jnp.sin(ang_c), jnp.sin(ang_c)], axis=-1)
    scale = jnp.concatenate([jnp.full((ATTN_WIDTH,), LOG2E * HEAD_DIM ** -0.5, F32),
                             jnp.ones((KV_WIDTH,), F32)])
    n_heads = N_Q_HEADS + N_KV_HEADS
    cos = jnp.tile(cos_h, (1, n_heads)) * scale
    sin = jnp.tile(sin_h, (1, n_heads)) * scale
    cos_c = jnp.broadcast_to(scale, (n_ctx, ROPE_WIDTH))
    sin_c = jnp.zeros((n_ctx, ROPE_WIDTH), F32)
    return cos, sin, cos_c, sin_c


def kernel(x, c, ctx, c_ctx, w_ada, b_ada, norm_g, w_in, conv_w, attn_sink, ssm_lam_re, ssm_lam_im,
           ssm_log_dt, ssm_b_re, ssm_b_im, ssm_c_re, ssm_c_im, ssm_d, w_glu, b_glu, w_out,
           w_mlp_in, w_mlp_out):
    bsz, n_lat, _ = x.shape
    n_ctx = ctx.shape[1]
    tm_lat = min(ROW_TILE, n_lat)
    tm_ctx = min(ROW_TILE, n_ctx)
    cos, sin, cos_c, sin_c = _rope_tables(n_lat, n_ctx)

    pad = (-(bsz + 1)) % SUBLANES
    cvec = jnp.concatenate([c, c_ctx[None, :], jnp.zeros((pad, D_MODEL), F32)], axis=0)
    mods_all = _ada_mods(cvec, w_ada, b_ada)

    h, hc = x, ctx
    for l in range(DEPTH):
        with_ctx_out = l < DEPTH - 1
        mods = mods_all[l, :bsz].reshape(bsz, N_MOD, D_MODEL)
        mods_c = jnp.broadcast_to(mods_all[l, bsz].reshape(1, N_MOD, D_MODEL),
                                  (bsz, N_MOD, D_MODEL))
        g = norm_g[l]
        w_in_l = w_in[l].astype(BF16)
        qe, qo, k2, v4, cb, z, u, u2 = _in_proj(h, mods, g, w_in_l, cos, sin, tm_lat, lat=True)
        qc, kc, vc, k2c, v4c, cbc, zc, uc, u2c = _in_proj(hc, mods_c, g, w_in_l, cos_c, sin_c,
                                                          tm_ctx, lat=False)

        attn = _attention_lat(attn_sink[l], qe, qo, k2, v4, k2c, v4c)

        s5w = _s5_weights(ssm_lam_re[l], ssm_lam_im[l], ssm_log_dt[l], ssm_b_re[l],
                          ssm_b_im[l], ssm_c_re[l], ssm_c_im[l])
        zero_state = jnp.zeros((2, bsz, 2 * N_STATE), F32)
        y2c, ctx_state = _s5_segment(u2c, bsz, s5w, zero_state)
        y2, _ = _s5_segment(u2, bsz, s5w, ctx_state)

        shared = (g, conv_w[l], ssm_d[l].reshape(1, SSM_WIDTH), w_glu[l].astype(BF16),
                  b_glu[l].reshape(1, SSM_WIDTH), w_out[l].astype(BF16),
                  w_mlp_in[l].astype(BF16), w_mlp_out[l].astype(BF16))
        h = _block(h, attn, cb, z, y2, u, mods, *shared, tm_lat)
        if with_ctx_out:
            attn_c = _attention_ctx(attn_sink[l], qc, kc, vc)
            hc = _block(hc, attn_c, cbc, zc, y2c, uc, mods_c, *shared, tm_ctx)
    return h
```

```python
import functools
import math

import numpy as np
import jax
import jax.numpy as jnp
from jax import lax
from jax.experimental import pallas as pl
from jax.experimental.pallas import tpu as pltpu

F32 = jnp.float32
BF16 = jnp.bfloat16

D_MODEL = 1024
DEPTH = 4
GRID_W = 64
HEAD_DIM = 64
N_Q_HEADS = 8
N_KV_HEADS = 2
Q_PER_KV = N_Q_HEADS // N_KV_HEADS
ATTN_WIDTH = N_Q_HEADS * HEAD_DIM
KV_WIDTH = N_KV_HEADS * HEAD_DIM
WINDOW = 128
Q_BLOCK = 128
ROPE_BASE = 10000.0
ROPE_PAIRS = HEAD_DIM // 4
CONV_WIDTH = 256
SSM_WIDTH = 256
SSM_GROUP = 16
SSM_GROUPS = SSM_WIDTH // SSM_GROUP
SSM_STATE = 64
N_STATE = SSM_GROUPS * SSM_STATE
MIX_WIDTH = ATTN_WIDTH + CONV_WIDTH + SSM_WIDTH
IN_WIDTH = ATTN_WIDTH + 2 * KV_WIDTH + 3 * CONV_WIDTH + SSM_WIDTH
ROPE_WIDTH = ATTN_WIDTH + KV_WIDTH
D_FF = 4 * D_MODEL
N_MOD = 6
EPS = 1e-6
NEG_INF = -1e30
LOG2E = math.log2(math.e)

LANES = 128
SUBLANES = 8
VMEM_LIMIT = 56 * 1024 * 1024

ROW_TILE = 512
S5_T = 16
S5_BLOCK_CHUNKS = 8
S5_ROW_TILE = 256
S5_PAIRS = SSM_GROUPS // 2
PAIR_IN = 2 * SSM_GROUP * S5_T
PAIR_STATE = 4 * SSM_STATE
FF_CHUNK = 1024


def _layer_spec(arr, layer):
    nd = arr.ndim
    return pl.BlockSpec((None,) + arr.shape[1:], lambda *_: (layer,) + (0,) * (nd - 1),
                        pipeline_mode=pl.Buffered(1))


def _params(n_grid):
    return pltpu.CompilerParams(dimension_semantics=("arbitrary",) * n_grid,
                                vmem_limit_bytes=VMEM_LIMIT)


def _rms(x, g):
    ms = jnp.mean(x * x, axis=-1, keepdims=True)
    return x * lax.rsqrt(ms + EPS) * g


def _sigmoid(x):
    return 1.0 / (1.0 + jnp.exp(-x))


def _ada_kernel(c_ref, w_ref, b_ref, o_ref):
    c = c_ref[...]
    act = (c * _sigmoid(c)).astype(BF16)
    o_ref[0] = jnp.dot(act, w_ref[0].astype(BF16), preferred_element_type=F32) + b_ref[0]


def _ada_mods(cvec, w_ada, b_ada):
    rows = cvec.shape[0]
    tn = 1024
    n_out = N_MOD * D_MODEL
    return pl.pallas_call(
        _ada_kernel,
        grid=(DEPTH, n_out // tn),
        in_specs=[pl.BlockSpec((rows, D_MODEL), lambda l, j: (0, 0)),
                  pl.BlockSpec((1, D_MODEL, tn), lambda l, j: (l, 0, j)),
                  pl.BlockSpec((1, 1, tn), lambda l, j: (l, 0, j))],
        out_specs=pl.BlockSpec((1, rows, tn), lambda l, j: (l, 0, j)),
        out_shape=jax.ShapeDtypeStruct((DEPTH, rows, n_out), F32),
        compiler_params=_params(2),
        name="ada_mods",
    )(cvec, w_ada, b_ada.reshape(DEPTH, 1, n_out))


def _piece_transpose(src):
    per = LANES // SSM_GROUP
    n = src[0][0].shape[0]
    slot = lax.broadcasted_iota(jnp.int32, (n, LANES), 1) // SSM_GROUP

    def moved(a, bh, k):
        return src[a][bh] if k == 0 else pltpu.roll(src[a][bh], k * SSM_GROUP, axis=1)

    dst = []
    for b in range(len(src)):
        bh, bl = divmod(b, per)
        halves = []
        for ah in range(len(src) // per):
            out = moved(ah * per, bh, -bl % per)
            for j in range(1, per):
                out = jnp.where(slot == j, moved(ah * per + j, bh, (j - bl) % per), out)
            halves.append(out)
        dst.append(halves)
    return dst


def _inproj_kernel(*refs, lat):
    h_ref, mod_ref, g_ref, w_ref, cos_ref, sin_ref = refs[:6]
    if lat:
        qe_ref, qo_ref, k2_ref, v4_ref, cb_ref, z_ref, u_ref, u2_ref, u_scr = refs[6:]
    else:
        q_ref, k_ref, v_ref, k2_ref, v4_ref, cb_ref, z_ref, u_ref, u2_ref, u_scr = refs[6:]
    mod = mod_ref[0]
    a = _rms(h_ref[0], g_ref[0:1, :]) * (1.0 + mod[1:2, :]) + mod[0:1, :]
    p = jnp.dot(a.astype(BF16), w_ref[...], preferred_element_type=F32)
    lane = lax.broadcasted_iota(jnp.int32, (1, LANES), 1)
    first_of_pair = (lane & ROPE_PAIRS) == 0
    low_head = lane < HEAD_DIM
    for s in range(ROPE_WIDTH // LANES):
        sl = slice(s * LANES, (s + 1) * LANES)
        t = p[:, sl]
        partner = jnp.where(first_of_pair,
                            pltpu.roll(t, LANES - ROPE_PAIRS, axis=1),
                            pltpu.roll(t, ROPE_PAIRS, axis=1))
        r = t * cos_ref[:, sl] + partner * sin_ref[:, sl]
        if s < ATTN_WIDTH // LANES:
            if lat:
                qe_ref[0, :, sl] = jnp.where(low_head, r, 0.0).astype(BF16)
                qo_ref[0, :, sl] = jnp.where(low_head, 0.0, r).astype(BF16)
            else:
                q_ref[0, :, sl] = r.astype(BF16)
        else:
            k2_ref[0, :, 0:LANES] = r.astype(BF16)
            k2_ref[0, :, LANES:2 * LANES] = pltpu.roll(r, HEAD_DIM, axis=1).astype(BF16)
            if not lat:
                k_ref[0] = r.astype(BF16)
    o = ROPE_WIDTH
    v = p[:, o:o + KV_WIDTH]
    v_swapped = pltpu.roll(v, HEAD_DIM, axis=1)
    for idx, part in enumerate((jnp.where(low_head, v, 0.0), jnp.where(low_head, 0.0, v_swapped),
                                jnp.where(low_head, v_swapped, 0.0), jnp.where(low_head, 0.0, v))):
        v4_ref[0, :, idx * LANES:(idx + 1) * LANES] = part.astype(BF16)
    if not lat:
        v_ref[0] = v.astype(BF16)
    o += KV_WIDTH
    cb_ref[0] = p[:, o:o + CONV_WIDTH]
    o += CONV_WIDTH
    z_ref[0] = p[:, o:o + CONV_WIDTH] * p[:, o + CONV_WIDTH:o + 2 * CONV_WIDTH]
    o += 2 * CONV_WIDTH
    u_ref[0] = p[:, o:o + SSM_WIDTH]
    n_chunks = u2_ref.shape[0]
    halves = SSM_WIDTH // LANES
    for hf in range(halves):
        u_scr[hf] = p[:, o + hf * LANES:o + (hf + 1) * LANES]
    steps = [[u_scr[hf, pl.ds(s, n_chunks, stride=S5_T), :]
              for hf in range(halves)] for s in range(S5_T)]
    groups = _piece_transpose(steps)
    for gi in range(SSM_GROUPS):
        for hf in range(halves):
            c0 = gi * SSM_GROUP * S5_T + hf * LANES
            u2_ref[:, c0:c0 + LANES] = groups[gi][hf].astype(BF16)


def _in_proj(h, mods, mod_row, norm_g, w_in, cos, sin, tm, lat, layer):
    b, n, _ = h.shape
    nt = n // tm
    if lat:
        widths = [(ATTN_WIDTH, BF16), (ATTN_WIDTH, BF16)]
    else:
        widths = [(ATTN_WIDTH, BF16), (KV_WIDTH, BF16), (KV_WIDTH, BF16)]
    widths += [(2 * KV_WIDTH, BF16), (4 * KV_WIDTH, BF16),
               (CONV_WIDTH, F32), (CONV_WIDTH, F32), (SSM_WIDTH, F32)]
    chunk_rows = pl.BlockSpec((tm // S5_T, S5_T * SSM_WIDTH), lambda j, i: (i * nt + j, 0))
    return pl.pallas_call(
        functools.partial(_inproj_kernel, lat=lat),
        grid=(nt, b),
        in_specs=[pl.BlockSpec((1, tm, D_MODEL), lambda j, i: (i, j, 0)),
                  pl.BlockSpec((1, N_MOD, D_MODEL), lambda j, i: (mod_row(i), 0, 0)),
                  _layer_spec(norm_g, layer),
                  _layer_spec(w_in, layer),
                  pl.BlockSpec((tm, ROPE_WIDTH), lambda j, i: (j, 0)),
                  pl.BlockSpec((tm, ROPE_WIDTH), lambda j, i: (j, 0))],
        out_specs=[pl.BlockSpec((1, tm, w), lambda j, i: (i, j, 0)) for w, _ in widths]
        + [chunk_rows],
        out_shape=[jax.ShapeDtypeStruct((b, n, w), dt) for w, dt in widths]
        + [jax.ShapeDtypeStruct((b * n // S5_T, S5_T * SSM_WIDTH), BF16)],
        scratch_shapes=[pltpu.VMEM((SSM_WIDTH // LANES, tm, LANES), F32)],
        compiler_params=_params(2),
        name="in_proj",
    )(h, mods, norm_g, w_in, cos, sin)


def _softmax_parts(s, sink):
    m = jnp.maximum(jnp.max(s, axis=-1, keepdims=True), sink)
    p = jnp.exp2(s - m)
    denom = jnp.sum(p, axis=-1, keepdims=True) + jnp.exp2(sink - m)
    return p, 1.0 / denom


def _attn_ctx_kernel(sink_ref, q_ref, k_ref, v_ref, o_ref, *, layer):
    q, k, v = q_ref[0], k_ref[0], v_ref[0]
    outs = []
    for h in range(N_Q_HEADS):
        hk = h // Q_PER_KV
        ksl = slice(hk * HEAD_DIM, (hk + 1) * HEAD_DIM)
        s = lax.dot_general(q[:, h * HEAD_DIM:(h + 1) * HEAD_DIM], k[:, ksl],
                            (((1,), (1,)), ((), ())), preferred_element_type=F32)
        p, inv = _softmax_parts(s, sink_ref[layer, h] * LOG2E)
        outs.append(jnp.dot(p.astype(BF16), v[:, ksl], preferred_element_type=F32) * inv)
    o_ref[0] = jnp.concatenate(outs, axis=1).astype(BF16)


def _attn_lat_kernel(sink_ref, qe_ref, qo_ref, kp_ref, kc_ref, kn_ref, kx_ref,
                     vp_ref, vc_ref, vn_ref, vx_ref, bias_ref, o_ref, s_scr, p_scr, inv_scr, *, tq,
                     layer):
    n = pl.program_id(0)

    @pl.when(n == 0)
    def _():
        s_scr[...] = jnp.zeros_like(s_scr)
        p_scr[...] = jnp.zeros_like(p_scr)
        inv_scr[...] = jnp.zeros_like(inv_scr)

    lane = lax.broadcasted_iota(jnp.int32, (1, LANES), 1)
    low_head = lane < HEAD_DIM
    slabs = ATTN_WIDTH // LANES

    def step(new, old):
        vall = jnp.concatenate([vp_ref[0], vc_ref[0], vn_ref[0], vx_ref[0]], axis=0)
        for t in range(slabs):
            c0 = (t // (slabs // N_KV_HEADS)) * 2 * LANES
            o = (jnp.dot(p_scr[new, 2 * t], vall[:, c0:c0 + LANES], preferred_element_type=F32)
                 + jnp.dot(p_scr[new, 2 * t + 1], vall[:, c0 + LANES:c0 + 2 * LANES],
                           preferred_element_type=F32))
            o_ref[0, :, t * LANES:(t + 1) * LANES] = (o * inv_scr[new, t]).astype(BF16)

        kall = jnp.concatenate([kp_ref[0], kc_ref[0], kn_ref[0], kx_ref[0]], axis=0)
        for hk in range(N_KV_HEADS):
            t0 = hk * (slabs // N_KV_HEADS)
            for par, q_ref in enumerate((qe_ref, qo_ref)):
                lhs = jnp.concatenate([q_ref[0, :, t0 * LANES:(t0 + 1) * LANES],
                                       q_ref[0, :, (t0 + 1) * LANES:(t0 + 2) * LANES]], axis=0)
                kv = (hk + par) % 2
                s = lax.dot_general(lhs, kall[:, kv * LANES:(kv + 1) * LANES],
                                    (((1,), (1,)), ((), ())), preferred_element_type=F32)
                s_scr[new, 2 * t0 + par] = s[:tq]
                s_scr[new, 2 * t0 + 2 + par] = s[tq:]

        invs = []
        for h in range(N_Q_HEADS):
            p, inv = _softmax_parts(s_scr[old, h] + bias_ref[0], sink_ref[layer, h] * LOG2E)
            p_scr[old, h] = p.astype(BF16)
            invs.append(inv)
        for t in range(slabs):
            inv_scr[old, t] = jnp.where(low_head, invs[2 * t], invs[2 * t + 1])

    for k in range(2):
        pl.when(n % 2 == k)(functools.partial(step, k, 1 - k))


def _window_bias(tq, n_ctx):
    r = np.arange(tq)[:, None]
    c = np.arange(3 * tq + n_ctx)[None, :]
    band = (c >= r) & (c <= r + 2 * WINDOW)
    variants = []
    for first_col, end_col in ((tq, 3 * tq), (0, 3 * tq), (0, 2 * tq)):
        ok = (c >= 3 * tq) | (band & (c >= first_col) & (c < end_col))
        variants.append(np.where(ok, 0.0, NEG_INF))
    return jnp.asarray(np.stack(variants), F32)


def _attention_lat(sink, layer, qe, qo, k2, v4, k2x, v4x):
    b, n, _ = qe.shape
    n_ctx = k2x.shape[1]
    tq = Q_BLOCK
    nb = n // tq
    total = b * nb
    n_keys = 3 * tq + n_ctx
    assert tq == WINDOW and nb >= 2

    def block_of(step, lag):
        m = jnp.clip(step - lag, 0, total - 1)
        return m // nb, m % nb

    def rows(width, lag, shift=0):
        def index(step):
            i, j = block_of(step, lag)
            return i, jnp.clip(j + shift, 0, nb - 1), 0
        return pl.BlockSpec((1, tq, width), index)

    def ctx_rows(width, lag):
        return pl.BlockSpec((1, n_ctx, width), lambda step: (block_of(step, lag)[0], 0, 0))

    def bias_index(step):
        j = block_of(step, 1)[1]
        return jnp.where(j == 0, 0, jnp.where(j == nb - 1, 2, 1)), 0, 0

    kw, vw = 2 * KV_WIDTH, 4 * KV_WIDTH
    return pl.pallas_call(
        functools.partial(_attn_lat_kernel, tq=tq, layer=layer),
        grid=(total + 2,),
        in_specs=[pl.BlockSpec(memory_space=pltpu.SMEM),
                  rows(ATTN_WIDTH, 0), rows(ATTN_WIDTH, 0),
                  rows(kw, 0, -1), rows(kw, 0), rows(kw, 0, 1), ctx_rows(kw, 0),
                  rows(vw, 2, -1), rows(vw, 2), rows(vw, 2, 1), ctx_rows(vw, 2),
                  pl.BlockSpec((1, tq, n_keys), bias_index)],
        out_specs=rows(ATTN_WIDTH, 2),
        out_shape=jax.ShapeDtypeStruct((b, n, ATTN_WIDTH), BF16),
        scratch_shapes=[pltpu.VMEM((2, N_Q_HEADS, tq, n_keys), F32),
                        pltpu.VMEM((2, N_Q_HEADS, tq, n_keys), BF16),
                        pltpu.VMEM((2, ATTN_WIDTH // LANES, tq, LANES), F32)],
        compiler_params=_params(1),
        name="attn_lat",
    )(sink, qe, qo, k2, k2, k2, k2x, v4, v4, v4, v4x, _window_bias(tq, n_ctx))


def _attention_ctx(sink, layer, q, k, v):
    b, n, _ = q.shape
    spec = lambda w: pl.BlockSpec((1, n, w), lambda i: (i, 0, 0))
    return pl.pallas_call(
        functools.partial(_attn_ctx_kernel, layer=layer),
        grid=(b,),
        in_specs=[pl.BlockSpec(memory_space=pltpu.SMEM), spec(ATTN_WIDTH), spec(KV_WIDTH),
                  spec(KV_WIDTH)],
        out_specs=spec(ATTN_WIDTH),
        out_shape=jax.ShapeDtypeStruct((b, n, ATTN_WIDTH), BF16),
        compiler_params=_params(1),
        name="attn_ctx",
    )(sink, q, k, v)


def _s5_exit_kernel(u_ref, we_ref, e_ref):
    half = e_ref.shape[1] // 2
    for gp in range(S5_PAIRS):
        r = jnp.dot(u_ref[:, gp * PAIR_IN:(gp + 1) * PAIR_IN], we_ref[gp],
                    preferred_element_type=F32)
        e_ref[:, gp * PAIR_STATE:(gp + 1) * PAIR_STATE] = r[:, :PAIR_STATE]
        e_ref[:, half + gp * PAIR_STATE:half + (gp + 1) * PAIR_STATE] = r[:, PAIR_STATE:]


def _s5_carry_kernel(e_ref, init_ref, lr_ref, li_ref, p_ref, fin_ref, state):
    @pl.when(pl.program_id(1) == 0)
    def _():
        state[...] = init_ref[0]

    def run(chunks):
        for gp in range(S5_PAIRS):
            re = slice(gp * PAIR_STATE, gp * PAIR_STATE + LANES)
            im = slice(gp * PAIR_STATE + LANES, (gp + 1) * PAIR_STATE)
            lr = lr_ref[0, :, gp * LANES:(gp + 1) * LANES]
            li = li_ref[0, :, gp * LANES:(gp + 1) * LANES]
            sr, si = state[:, re], state[:, im]
            for k in chunks:
                p_ref[:, k, re] = sr
                p_ref[:, k, im] = si
                sr, si = (lr * sr - li * si + e_ref[:, k, re],
                          lr * si + li * sr + e_ref[:, k, im])
            state[:, re] = sr
            state[:, im] = si

    n_chunks = e_ref.shape[1]
    direction = pl.program_id(0)
    pl.when(direction == 0)(lambda: run(tuple(range(n_chunks))))
    pl.when(direction == 1)(lambda: run(tuple(reversed(range(n_chunks)))))
    fin_ref[0] = state[...]


def _s5_out_kernel(u_ref, p_ref, toep_ref, csf_ref, csb_ref, y_ref):
    half = p_ref.shape[1] // 2
    grp = PAIR_IN // 2
    for gp in range(S5_PAIRS):
        pf = p_ref[:, gp * PAIR_STATE:(gp + 1) * PAIR_STATE].astype(BF16)
        pb = p_ref[:, half + gp * PAIR_STATE:half + (gp + 1) * PAIR_STATE].astype(BF16)
        st = (jnp.dot(pf, csf_ref[gp], preferred_element_type=F32)
              + jnp.dot(pb, csb_ref[gp], preferred_element_type=F32))
        for a in range(2):
            cols = slice(gp * PAIR_IN + a * grp, gp * PAIR_IN + (a + 1) * grp)
            y_ref[:, cols] = (jnp.dot(u_ref[:, cols], toep_ref[2 * gp + a],
                                      preferred_element_type=F32)
                              + st[:, a * grp:(a + 1) * grp])


def _s5_segment(u2, bsz, weights, init, layer):
    toep, we, csf, csb, lr, li = weights
    n_rows, width = u2.shape
    nc = n_rows // bsz
    tr = min(S5_ROW_TILE, n_rows)
    n_state = 2 * S5_PAIRS * PAIR_STATE
    row_blk = lambda w: pl.BlockSpec((tr, w), lambda j: (j, 0))
    exits = pl.pallas_call(
        _s5_exit_kernel,
        grid=(n_rows // tr,),
        in_specs=[row_blk(width), _layer_spec(we, layer)],
        out_specs=row_blk(n_state),
        out_shape=jax.ShapeDtypeStruct((n_rows, n_state), F32),
        compiler_params=_params(1),
        name="s5_exit",
    )(u2, we)

    nb = nc // S5_BLOCK_CHUNKS
    dir_blk = pl.BlockSpec((bsz, S5_BLOCK_CHUNKS, n_state // 2),
                           lambda d, k: (0, jnp.where(d == 0, k, nb - 1 - k), d))
    state_spec = pl.BlockSpec((1, bsz, n_state // 2), lambda d, k: (d, 0, 0))
    lam_spec = pl.BlockSpec((None, 1, 1, n_state // 4), lambda d, k: (layer, d, 0, 0))
    entering, leaving = pl.pallas_call(
        _s5_carry_kernel,
        grid=(2, nb),
        in_specs=[dir_blk, state_spec, lam_spec, lam_spec],
        out_specs=[dir_blk, state_spec],
        out_shape=[jax.ShapeDtypeStruct((bsz, nc, n_state), F32),
                   jax.ShapeDtypeStruct((2, bsz, n_state // 2), F32)],
        scratch_shapes=[pltpu.VMEM((bsz, n_state // 2), F32)],
        compiler_params=_params(2),
        name="s5_carry",
    )(exits.reshape(bsz, nc, n_state), init, lr, li)

    y2 = pl.pallas_call(
        _s5_out_kernel,
        grid=(n_rows // tr,),
        in_specs=[row_blk(width), row_blk(n_state), _layer_spec(toep, layer),
                  _layer_spec(csf, layer), _layer_spec(csb, layer)],
        out_specs=row_blk(width),
        out_shape=jax.ShapeDtypeStruct((n_rows, width), F32),
        compiler_params=_params(1),
        name="s5_out",
    )(u2, entering.reshape(n_rows, n_state), toep, csf, csb)
    return y2, leaving


def _pair_diag(m):
    g, r, c = m.shape
    eye = jnp.eye(2, dtype=m.dtype)
    return jnp.einsum('qarc,ab->qarbc', m.reshape(g // 2, 2, r, c), eye).reshape(g // 2, 2 * r, 2 * c)


def _s5_weights(lam_re, lam_im, log_dt, b_re, b_im, c_re, c_im):
    t = S5_T
    lam = lax.complex(lam_re.astype(F32), lam_im.astype(F32))
    lam_dt = lam * jnp.exp(log_dt.astype(F32))[..., None]
    lam_bar = jnp.exp(lam_dt)
    bb = ((lam_bar - 1) / lam)[..., None] * lax.complex(b_re.astype(F32), b_im.astype(F32))
    cm = lax.complex(c_re.astype(F32), c_im.astype(F32))
    steps = jnp.arange(t + 1, dtype=F32)
    pw = jnp.exp(lam_dt[None] * steps[:, None, None, None])

    kern = jnp.real(jnp.einsum('dgop,kdgp,dgpi->dkgoi', cm, pw[:t], bb))
    s_idx = jnp.arange(t)[:, None]
    t_idx = jnp.arange(t)[None, :]
    lag = t_idx - s_idx
    kf = jnp.where((lag >= 0)[..., None, None, None], kern[0][jnp.maximum(lag, 0)], 0.0)
    kb = jnp.where((lag <= 0)[..., None, None, None], kern[1][jnp.maximum(-lag, 0)], 0.0)
    toep = jnp.transpose(kf + kb, (2, 0, 4, 1, 3))
    toep = toep.reshape(SSM_GROUPS, t * SSM_GROUP, t * SSM_GROUP)

    ef = pw[t - 1 - jnp.arange(t), 0][..., None] * bb[0][None]
    eb = pw[jnp.arange(t), 1][..., None] * bb[1][None]
    to_cols = lambda m: jnp.transpose(m, (1, 0, 3, 2)).reshape(SSM_GROUPS, t * SSM_GROUP, SSM_STATE)
    we = jnp.concatenate([_pair_diag(to_cols(part(m))) for m in (ef, eb)
                          for part in (jnp.real, jnp.imag)], axis=-1)

    cf = cm[0][None] * pw[1 + jnp.arange(t), 0][:, :, None, :]
    cb = cm[1][None] * pw[t - jnp.arange(t), 1][:, :, None, :]
    to_rows = lambda m: jnp.transpose(m, (1, 3, 0, 2)).reshape(SSM_GROUPS, SSM_STATE, t * SSM_GROUP)
    cs = [jnp.concatenate([_pair_diag(to_rows(jnp.real(m))), _pair_diag(to_rows(-jnp.imag(m)))],
                          axis=1) for m in (cf, cb)]

    lam_t = pw[t].reshape(2, 1, N_STATE)
    return (toep.astype(BF16), we.astype(BF16), cs[0].astype(BF16), cs[1].astype(BF16),
            jnp.real(lam_t), jnp.imag(lam_t))


def _block_kernel(h_ref, attn_ref, cb_ref, z_ref, zp_ref, zn_ref, y2_ref, u_ref, mod_ref, g_ref,
                  cw_ref, dsk_ref, wglu_ref, bglu_ref, wout_ref, w1_ref, w2_ref, o_ref, y_scr,
                  *, tm):
    j = pl.program_id(0)
    mod = mod_ref[0]
    z = z_ref[0]
    has_prev = jnp.where(j > 0, 1.0, 0.0)
    has_next = jnp.where(j < pl.num_programs(0) - 1, 1.0, 0.0)
    z_before = zp_ref[0][SUBLANES - 1:SUBLANES, :] * has_prev
    z_after = zn_ref[0][0:1, :] * has_next
    row = lax.broadcasted_iota(jnp.int32, (tm, 1), 0)
    z_dn = jnp.where(row == 0, z_before, pltpu.roll(z, 1, axis=0))
    z_up = jnp.where(row == tm - 1, z_after, pltpu.roll(z, tm - 1, axis=0))
    conv = cb_ref[0] * (z_dn * cw_ref[0:1, :] + z * cw_ref[1:2, :] + z_up * cw_ref[2:3, :])
    n_chunks = y2_ref.shape[0]
    halves = S5_T * SSM_GROUP // LANES
    groups = [[y2_ref[:, (gi * halves + hf) * LANES:(gi * halves + hf + 1) * LANES]
               for hf in range(halves)] for gi in range(SSM_GROUPS)]
    steps = _piece_transpose(groups)
    for s in range(S5_T):
        for hf in range(SSM_WIDTH // LANES):
            y_scr[hf, pl.ds(s, n_chunks, stride=S5_T), :] = steps[s][hf]
    y = jnp.concatenate([y_scr[hf] for hf in range(SSM_WIDTH // LANES)], axis=1)
    y = y + dsk_ref[...] * u_ref[0]
    gl = 0.5 * y * (1.0 + jnp.tanh(math.sqrt(2.0 / math.pi) * (y + 0.044715 * (y * y * y))))
    gate = jnp.dot(gl.astype(BF16), wglu_ref[...], preferred_element_type=F32) + bglu_ref[...]
    ssm = gl * _sigmoid(gate)
    a0, a1 = ATTN_WIDTH, ATTN_WIDTH + CONV_WIDTH
    m = (jnp.dot(attn_ref[0], wout_ref[0:a0, :], preferred_element_type=F32)
         + jnp.dot(conv.astype(BF16), wout_ref[a0:a1, :], preferred_element_type=F32)
         + jnp.dot(ssm.astype(BF16), wout_ref[a1:, :], preferred_element_type=F32))
    h1 = h_ref[0] + mod[2:3, :] * _rms(m, g_ref[1:2, :])
    a2 = (_rms(h1, g_ref[2:3, :]) * (1.0 + mod[4:5, :]) + mod[3:4, :]).astype(BF16)
    f = jnp.zeros((tm, D_MODEL), F32)
    for cidx in range(D_FF // FF_CHUNK):
        cs = slice(cidx * FF_CHUNK, (cidx + 1) * FF_CHUNK)
        t = jnp.maximum(jnp.dot(a2, w1_ref[:, cs], preferred_element_type=F32), 0.0)
        f = f + jnp.dot((t * t).astype(BF16), w2_ref[cs, :], preferred_element_type=F32)
    o_ref[0] = h1 + mod[5:6, :] * _rms(f, g_ref[3:4, :])


def _block(h, attn, cb, z, y2, u, mods, mod_row, params, tm, layer):
    b, n, _ = h.shape
    nt = n // tm
    per = tm // SUBLANES
    n8 = n // SUBLANES
    row = lambda w: pl.BlockSpec((1, tm, w), lambda j, i: (i, j, 0))
    chunk_rows = pl.BlockSpec((tm // S5_T, S5_T * SSM_WIDTH), lambda j, i: (i * nt + j, 0))
    return pl.pallas_call(
        functools.partial(_block_kernel, tm=tm),
        grid=(nt, b),
        in_specs=[row(D_MODEL), row(ATTN_WIDTH), row(CONV_WIDTH), row(CONV_WIDTH),
                  pl.BlockSpec((1, SUBLANES, CONV_WIDTH),
                               lambda j, i: (i, jnp.maximum(j * per - 1, 0), 0)),
                  pl.BlockSpec((1, SUBLANES, CONV_WIDTH),
                               lambda j, i: (i, jnp.minimum((j + 1) * per, n8 - 1), 0)),
                  chunk_rows, row(SSM_WIDTH),
                  pl.BlockSpec((1, N_MOD, D_MODEL), lambda j, i: (mod_row(i), 0, 0))]
        + [_layer_spec(p, layer) for p in params],
        out_specs=row(D_MODEL),
        out_shape=jax.ShapeDtypeStruct((b, n, D_MODEL), F32),
        scratch_shapes=[pltpu.VMEM((SSM_WIDTH // LANES, tm, LANES), F32)],
        compiler_params=_params(2),
        name="mix_mlp_block",
    )(h, attn, cb, z, z, z, y2, u, mods, *params)


def _rope_tables(n_lat, n_ctx):
    f32 = np.float32
    rows = n_lat // GRID_W
    row = np.repeat(np.arange(rows), GRID_W).astype(f32)[:, None]
    col = np.tile(np.arange(GRID_W), rows).astype(f32)[:, None]
    freqs = (f32(ROPE_BASE) ** (-np.arange(ROPE_PAIRS, dtype=f32) / f32(ROPE_PAIRS))).astype(f32)
    ang_r, ang_c = row * freqs, col * freqs
    cos_h = np.concatenate([np.cos(ang_r)] * 2 + [np.cos(ang_c)] * 2, axis=-1)
    sin_h = np.concatenate([-np.sin(ang_r), np.sin(ang_r), -np.sin(ang_c), np.sin(ang_c)], axis=-1)
    scale = np.concatenate([np.full((ATTN_WIDTH,), LOG2E * HEAD_DIM ** -0.5, f32),
                            np.ones((KV_WIDTH,), f32)])
    n_heads = N_Q_HEADS + N_KV_HEADS
    cos = (np.tile(cos_h, (1, n_heads)) * scale).astype(f32)
    sin = (np.tile(sin_h, (1, n_heads)) * scale).astype(f32)
    cos_c = np.broadcast_to(scale, (n_ctx, ROPE_WIDTH)).astype(f32)
    sin_c = np.zeros((n_ctx, ROPE_WIDTH), f32)
    return tuple(jnp.asarray(t) for t in (cos, sin, cos_c, sin_c))


def kernel(x, c, ctx, c_ctx, w_ada, b_ada, norm_g, w_in, conv_w, attn_sink, ssm_lam_re, ssm_lam_im,
           ssm_log_dt, ssm_b_re, ssm_b_im, ssm_c_re, ssm_c_im, ssm_d, w_glu, b_glu, w_out,
           w_mlp_in, w_mlp_out):
    bsz, n_lat, _ = x.shape
    n_ctx = ctx.shape[1]
    tm_lat = min(ROW_TILE, n_lat)
    tm_ctx = min(ROW_TILE, n_ctx)
    cos, sin, cos_c, sin_c = _rope_tables(n_lat, n_ctx)

    pad = (-(bsz + 1)) % SUBLANES
    cvec = jnp.concatenate([c, c_ctx[None, :], jnp.zeros((pad, D_MODEL), F32)], axis=0)
    rows = cvec.shape[0]
    mods = _ada_mods(cvec, w_ada, b_ada).reshape(DEPTH * rows, N_MOD, D_MODEL)

    s5w = jax.vmap(_s5_weights)(ssm_lam_re, ssm_lam_im, ssm_log_dt, ssm_b_re, ssm_b_im,
                                ssm_c_re, ssm_c_im)
    w_in_b = w_in.astype(BF16)
    block_params = (norm_g, conv_w, ssm_d.reshape(DEPTH, 1, SSM_WIDTH), w_glu.astype(BF16),
                    b_glu.reshape(DEPTH, 1, SSM_WIDTH), w_out.astype(BF16),
                    w_mlp_in.astype(BF16), w_mlp_out.astype(BF16))
    zero_state = jnp.zeros((2, bsz, 2 * N_STATE), F32)

    h, hc = x, ctx
    for l in range(DEPTH):
        with_ctx_out = l < DEPTH - 1
        lat_row = lambda i, base=l * rows: base + i
        ctx_row = lambda i, base=l * rows + bsz: base
        qe, qo, k2, v4, cb, z, u, u2 = _in_proj(h, mods, lat_row, norm_g, w_in_b, cos, sin,
                                                tm_lat, True, l)
        qc, kc, vc, k2c, v4c, cbc, zc, uc, u2c = _in_proj(hc, mods, ctx_row, norm_g, w_in_b,
                                                          cos_c, sin_c, tm_ctx, False, l)

        attn = _attention_lat(attn_sink, l, qe, qo, k2, v4, k2c, v4c)

        y2c, ctx_state = _s5_segment(u2c, bsz, s5w, zero_state, l)
        y2, _ = _s5_segment(u2, bsz, s5w, ctx_state, l)

        h = _block(h, attn, cb, z, y2, u, mods, lat_row, block_params, tm_lat, l)
        if with_ctx_out:
            attn_c = _attention_ctx(attn_sink, l, qc, kc, vc)
            hc = _block(hc, attn_c, cbc, zc, y2c, uc, mods, ctx_row, block_params, tm_ctx, l)
    return h
```

```python
import functools
import math

import numpy as np
import jax
import jax.numpy as jnp
from jax import lax
from jax.experimental import pallas as pl
from jax.experimental.pallas import tpu as pltpu

F32 = jnp.float32
BF16 = jnp.bfloat16

D_MODEL = 1024
DEPTH = 4
GRID_W = 64
HEAD_DIM = 64
N_Q_HEADS = 8
N_KV_HEADS = 2
Q_PER_KV = N_Q_HEADS // N_KV_HEADS
ATTN_WIDTH = N_Q_HEADS * HEAD_DIM
KV_WIDTH = N_KV_HEADS * HEAD_DIM
WINDOW = 128
Q_BLOCK = 128
ROPE_BASE = 10000.0
ROPE_PAIRS = HEAD_DIM // 4
CONV_WIDTH = 256
SSM_WIDTH = 256
SSM_GROUP = 16
SSM_GROUPS = SSM_WIDTH // SSM_GROUP
SSM_STATE = 64
N_STATE = SSM_GROUPS * SSM_STATE
MIX_WIDTH = ATTN_WIDTH + CONV_WIDTH + SSM_WIDTH
IN_WIDTH = ATTN_WIDTH + 2 * KV_WIDTH + 3 * CONV_WIDTH + SSM_WIDTH
ROPE_WIDTH = ATTN_WIDTH + KV_WIDTH
D_FF = 4 * D_MODEL
N_MOD = 6
EPS = 1e-6
NEG_INF = -1e30
LOG2E = math.log2(math.e)

LANES = 128
SUBLANES = 8
VMEM_LIMIT = 56 * 1024 * 1024

ROW_TILE = 512
S5_T = 16
S5_BLOCK_CHUNKS = 8
S5_ROW_TILE = 256
S5_PAIRS = SSM_GROUPS // 2
PAIR_IN = 2 * SSM_GROUP * S5_T
PAIR_STATE = 4 * SSM_STATE
FF_CHUNK = 1024


def _layer_spec(arr, layer):
    nd = arr.ndim
    return pl.BlockSpec((None,) + arr.shape[1:], lambda *_: (layer,) + (0,) * (nd - 1),
                        pipeline_mode=pl.Buffered(1))


def _params(n_grid):
    return pltpu.CompilerParams(dimension_semantics=("arbitrary",) * n_grid,
                                vmem_limit_bytes=VMEM_LIMIT)


def _rms(x, g):
    ms = jnp.mean(x * x, axis=-1, keepdims=True)
    return x * lax.rsqrt(ms + EPS) * g


def _sigmoid(x):
    return 1.0 / (1.0 + jnp.exp(-x))


def _ada_kernel(c_ref, w_ref, b_ref, o_ref):
    c = c_ref[...]
    act = (c * _sigmoid(c)).astype(BF16)
    o_ref[0] = jnp.dot(act, w_ref[0].astype(BF16), preferred_element_type=F32) + b_ref[0]


def _ada_mods(cvec, w_ada, b_ada):
    rows = cvec.shape[0]
    tn = 1024
    n_out = N_MOD * D_MODEL
    return pl.pallas_call(
        _ada_kernel,
        grid=(DEPTH, n_out // tn),
        in_specs=[pl.BlockSpec((rows, D_MODEL), lambda l, j: (0, 0)),
                  pl.BlockSpec((1, D_MODEL, tn), lambda l, j: (l, 0, j)),
                  pl.BlockSpec((1, 1, tn), lambda l, j: (l, 0, j))],
        out_specs=pl.BlockSpec((1, rows, tn), lambda l, j: (l, 0, j)),
        out_shape=jax.ShapeDtypeStruct((DEPTH, rows, n_out), F32),
        compiler_params=_params(2),
        name="ada_mods",
    )(cvec, w_ada, b_ada.reshape(DEPTH, 1, n_out))


def _piece_transpose(src):
    per = LANES // SSM_GROUP
    n = src[0][0].shape[0]
    slot = lax.broadcasted_iota(jnp.int32, (n, LANES), 1) // SSM_GROUP

    def moved(a, bh, k):
        return src[a][bh] if k == 0 else pltpu.roll(src[a][bh], k * SSM_GROUP, axis=1)

    dst = []
    for b in range(len(src)):
        bh, bl = divmod(b, per)
        halves = []
        for ah in range(len(src) // per):
            out = moved(ah * per, bh, -bl % per)
            for j in range(1, per):
                out = jnp.where(slot == j, moved(ah * per + j, bh, (j - bl) % per), out)
            halves.append(out)
        dst.append(halves)
    return dst


def _inproj_kernel(*refs, lat):
    h_ref, mod_ref, g_ref, w_ref, cos_ref, sin_ref = refs[:6]
    if lat:
        qe_ref, qo_ref, k2_ref, v4_ref, cb_ref, z_ref, u_ref, u2_ref, u_scr = refs[6:]
    else:
        q_ref, k_ref, v_ref, k2_ref, v4_ref, cb_ref, z_ref, u_ref, u2_ref, u_scr = refs[6:]
    mod = mod_ref[0]
    a = _rms(h_ref[0], g_ref[0:1, :]) * (1.0 + mod[1:2, :]) + mod[0:1, :]
    p = jnp.dot(a.astype(BF16), w_ref[...], preferred_element_type=F32)
    lane = lax.broadcasted_iota(jnp.int32, (1, LANES), 1)
    first_of_pair = (lane & ROPE_PAIRS) == 0
    low_head = lane < HEAD_DIM
    for s in range(ROPE_WIDTH // LANES):
        sl = slice(s * LANES, (s + 1) * LANES)
        t = p[:, sl]
        partner = jnp.where(first_of_pair,
                            pltpu.roll(t, LANES - ROPE_PAIRS, axis=1),
                            pltpu.roll(t, ROPE_PAIRS, axis=1))
        r = t * cos_ref[:, sl] + partner * sin_ref[:, sl]
        if s < ATTN_WIDTH // LANES:
            if lat:
                qe_ref[0, :, sl] = jnp.where(low_head, r, 0.0).astype(BF16)
                qo_ref[0, :, sl] = jnp.where(low_head, 0.0, r).astype(BF16)
            else:
                q_ref[0, :, sl] = r.astype(BF16)
        else:
            k2_ref[0, :, 0:LANES] = r.astype(BF16)
            k2_ref[0, :, LANES:2 * LANES] = pltpu.roll(r, HEAD_DIM, axis=1).astype(BF16)
            if not lat:
                k_ref[0] = r.astype(BF16)
    o = ROPE_WIDTH
    v = p[:, o:o + KV_WIDTH]
    v_swapped = pltpu.roll(v, HEAD_DIM, axis=1)
    for idx, part in enumerate((jnp.where(low_head, v, 0.0), jnp.where(low_head, 0.0, v_swapped),
                                jnp.where(low_head, v_swapped, 0.0), jnp.where(low_head, 0.0, v))):
        v4_ref[0, :, idx * LANES:(idx + 1) * LANES] = part.astype(BF16)
    if not lat:
        v_ref[0] = v.astype(BF16)
    o += KV_WIDTH
    cb_ref[0] = p[:, o:o + CONV_WIDTH]
    o += CONV_WIDTH
    z_ref[0] = p[:, o:o + CONV_WIDTH] * p[:, o + CONV_WIDTH:o + 2 * CONV_WIDTH]
    o += 2 * CONV_WIDTH
    u_ref[0] = p[:, o:o + SSM_WIDTH]
    n_chunks = u2_ref.shape[0]
    halves = SSM_WIDTH // LANES
    for hf in range(halves):
        u_scr[hf] = p[:, o + hf * LANES:o + (hf + 1) * LANES]
    steps = [[u_scr[hf, pl.ds(s, n_chunks, stride=S5_T), :]
              for hf in range(halves)] for s in range(S5_T)]
    groups = _piece_transpose(steps)
    for gi in range(SSM_GROUPS):
        for hf in range(halves):
            c0 = gi * SSM_GROUP * S5_T + hf * LANES
            u2_ref[:, c0:c0 + LANES] = groups[gi][hf].astype(BF16)


def _in_proj(h, mods, mod_row, norm_g, w_in, cos, sin, tm, lat, layer):
    b, n, _ = h.shape
    nt = n // tm
    if lat:
        widths = [(ATTN_WIDTH, BF16), (ATTN_WIDTH, BF16)]
    else:
        widths = [(ATTN_WIDTH, BF16), (KV_WIDTH, BF16), (KV_WIDTH, BF16)]
    widths += [(2 * KV_WIDTH, BF16), (4 * KV_WIDTH, BF16),
               (CONV_WIDTH, F32), (CONV_WIDTH, F32), (SSM_WIDTH, F32)]
    chunk_rows = pl.BlockSpec((tm // S5_T, S5_T * SSM_WIDTH), lambda j, i: (i * nt + j, 0))
    return pl.pallas_call(
        functools.partial(_inproj_kernel, lat=lat),
        grid=(nt, b),
        in_specs=[pl.BlockSpec((1, tm, D_MODEL), lambda j, i: (i, j, 0)),
                  pl.BlockSpec((1, N_MOD, D_MODEL), lambda j, i: (mod_row(i), 0, 0)),
                  _layer_spec(norm_g, layer),
                  _layer_spec(w_in, layer),
                  pl.BlockSpec((tm, ROPE_WIDTH), lambda j, i: (j, 0)),
                  pl.BlockSpec((tm, ROPE_WIDTH), lambda j, i: (j, 0))],
        out_specs=[pl.BlockSpec((1, tm, w), lambda j, i: (i, j, 0)) for w, _ in widths]
        + [chunk_rows],
        out_shape=[jax.ShapeDtypeStruct((b, n, w), dt) for w, dt in widths]
        + [jax.ShapeDtypeStruct((b * n // S5_T, S5_T * SSM_WIDTH), BF16)],
        scratch_shapes=[pltpu.VMEM((SSM_WIDTH // LANES, tm, LANES), F32)],
        compiler_params=_params(2),
        name="in_proj",
    )(h, mods, norm_g, w_in, cos, sin)


def _softmax_parts(s, sink):
    m = jnp.maximum(jnp.max(s, axis=-1, keepdims=True), sink)
    p = jnp.exp2(s - m)
    denom = jnp.sum(p, axis=-1, keepdims=True) + jnp.exp2(sink - m)
    return p, 1.0 / denom


def _attn_ctx_kernel(sink_ref, q_ref, k_ref, v_ref, o_ref, *, layer):
    q, k, v = q_ref[0], k_ref[0], v_ref[0]
    outs = []
    for h in range(N_Q_HEADS):
        hk = h // Q_PER_KV
        ksl = slice(hk * HEAD_DIM, (hk + 1) * HEAD_DIM)
        s = lax.dot_general(q[:, h * HEAD_DIM:(h + 1) * HEAD_DIM], k[:, ksl],
                            (((1,), (1,)), ((), ())), preferred_element_type=F32)
        p, inv = _softmax_parts(s, sink_ref[layer, h] * LOG2E)
        outs.append(jnp.dot(p.astype(BF16), v[:, ksl], preferred_element_type=F32) * inv)
    o_ref[0] = jnp.concatenate(outs, axis=1).astype(BF16)


def _attn_lat_kernel(sink_ref, qe_ref, qo_ref, kp_ref, kc_ref, kn_ref, kx_ref,
                     vp_ref, vc_ref, vn_ref, vx_ref, bias_ref, o_ref, s_scr, p_scr, inv_scr, *, tq,
                     layer):
    n = pl.program_id(0)

    @pl.when(n == 0)
    def _():
        s_scr[...] = jnp.zeros_like(s_scr)
        p_scr[...] = jnp.zeros_like(p_scr)
        inv_scr[...] = jnp.zeros_like(inv_scr)

    lane = lax.broadcasted_iota(jnp.int32, (1, LANES), 1)
    low_head = lane < HEAD_DIM
    slabs = ATTN_WIDTH // LANES

    def step(new, old):
        vall = jnp.concatenate([vp_ref[0], vc_ref[0], vn_ref[0], vx_ref[0]], axis=0)
        for t in range(slabs):
            c0 = (t // (slabs // N_KV_HEADS)) * 2 * LANES
            o = (jnp.dot(p_scr[new, 2 * t], vall[:, c0:c0 + LANES], preferred_element_type=F32)
                 + jnp.dot(p_scr[new, 2 * t + 1], vall[:, c0 + LANES:c0 + 2 * LANES],
                           preferred_element_type=F32))
            o_ref[0, :, t * LANES:(t + 1) * LANES] = (o * inv_scr[new, t]).astype(BF16)

        kall = jnp.concatenate([kp_ref[0], kc_ref[0], kn_ref[0], kx_ref[0]], axis=0)
        for hk in range(N_KV_HEADS):
            t0 = hk * (slabs // N_KV_HEADS)
            for par, q_ref in enumerate((qe_ref, qo_ref)):
                lhs = jnp.concatenate([q_ref[0, :, t0 * LANES:(t0 + 1) * LANES],
                                       q_ref[0, :, (t0 + 1) * LANES:(t0 + 2) * LANES]], axis=0)
                kv = (hk + par) % 2
                s = lax.dot_general(lhs, kall[:, kv * LANES:(kv + 1) * LANES],
                                    (((1,), (1,)), ((), ())), preferred_element_type=F32)
                s_scr[new, 2 * t0 + par] = s[:tq]
                s_scr[new, 2 * t0 + 2 + par] = s[tq:]

        invs = []
        for h in range(N_Q_HEADS):
            p, inv = _softmax_parts(s_scr[old, h] + bias_ref[0], sink_ref[layer, h] * LOG2E)
            p_scr[old, h] = p.astype(BF16)
            invs.append(inv)
        for t in range(slabs):
            inv_scr[old, t] = jnp.where(low_head, invs[2 * t], invs[2 * t + 1])

    for k in range(2):
        pl.when(n % 2 == k)(functools.partial(step, k, 1 - k))


def _window_bias(tq, n_ctx):
    r = np.arange(tq)[:, None]
    c = np.arange(3 * tq + n_ctx)[None, :]
    band = (c >= r) & (c <= r + 2 * WINDOW)
    variants = []
    for first_col, end_col in ((tq, 3 * tq), (0, 3 * tq), (0, 2 * tq)):
        ok = (c >= 3 * tq) | (band & (c >= first_col) & (c < end_col))
        variants.append(np.where(ok, 0.0, NEG_INF))
    return jnp.asarray(np.stack(variants), F32)


def _attention_lat(sink, layer, qe, qo, k2, v4, k2x, v4x):
    b, n, _ = qe.shape
    n_ctx = k2x.shape[1]
    tq = Q_BLOCK
    nb = n // tq
    total = b * nb
    n_keys = 3 * tq + n_ctx
    assert tq == WINDOW and nb >= 2

    def block_of(step, lag):
        m = jnp.clip(step - lag, 0, total - 1)
        return m // nb, m % nb

    def rows(width, lag, shift=0):
        def index(step):
            i, j = block_of(step, lag)
            return i, jnp.clip(j + shift, 0, nb - 1), 0
        return pl.BlockSpec((1, tq, width), index)

    def ctx_rows(width, lag):
        return pl.BlockSpec((1, n_ctx, width), lambda step: (block_of(step, lag)[0], 0, 0))

    def bias_index(step):
        j = block_of(step, 1)[1]
        return jnp.where(j == 0, 0, jnp.where(j == nb - 1, 2, 1)), 0, 0

    kw, vw = 2 * KV_WIDTH, 4 * KV_WIDTH
    return pl.pallas_call(
        functools.partial(_attn_lat_kernel, tq=tq, layer=layer),
        grid=(total + 2,),
        in_specs=[pl.BlockSpec(memory_space=pltpu.SMEM),
                  rows(ATTN_WIDTH, 0), rows(ATTN_WIDTH, 0),
                  rows(kw, 0, -1), rows(kw, 0), rows(kw, 0, 1), ctx_rows(kw, 0),
                  rows(vw, 2, -1), rows(vw, 2), rows(vw, 2, 1), ctx_rows(vw, 2),
                  pl.BlockSpec((1, tq, n_keys), bias_index)],
        out_specs=rows(ATTN_WIDTH, 2),
        out_shape=jax.ShapeDtypeStruct((b, n, ATTN_WIDTH), BF16),
        scratch_shapes=[pltpu.VMEM((2, N_Q_HEADS, tq, n_keys), F32),
                        pltpu.VMEM((2, N_Q_HEADS, tq, n_keys), BF16),
                        pltpu.VMEM((2, ATTN_WIDTH // LANES, tq, LANES), F32)],
        compiler_params=_params(1),
        name="attn_lat",
    )(sink, qe, qo, k2, k2, k2, k2x, v4, v4, v4, v4x, _window_bias(tq, n_ctx))


def _attention_ctx(sink, layer, q, k, v):
    b, n, _ = q.shape
    spec = lambda w: pl.BlockSpec((1, n, w), lambda i: (i, 0, 0))
    return pl.pallas_call(
        functools.partial(_attn_ctx_kernel, layer=layer),
        grid=(b,),
        in_specs=[pl.BlockSpec(memory_space=pltpu.SMEM), spec(ATTN_WIDTH), spec(KV_WIDTH),
                  spec(KV_WIDTH)],
        out_specs=spec(ATTN_WIDTH),
        out_shape=jax.ShapeDtypeStruct((b, n, ATTN_WIDTH), BF16),
        compiler_params=_params(1),
        name="attn_ctx",
    )(sink, q, k, v)


def _s5_exit_kernel(u_ref, we_ref, e_ref):
    half = e_ref.shape[1] // 2
    for gp in range(S5_PAIRS):
        r = jnp.dot(u_ref[:, gp * PAIR_IN:(gp + 1) * PAIR_IN], we_ref[gp],
                    preferred_element_type=F32)
        e_ref[:, gp * PAIR_STATE:(gp + 1) * PAIR_STATE] = r[:, :PAIR_STATE]
        e_ref[:, half + gp * PAIR_STATE:half + (gp + 1) * PAIR_STATE] = r[:, PAIR_STATE:]


def _s5_carry_kernel(e_ref, init_ref, lr_ref, li_ref, p_ref, fin_ref, state):
    @pl.when(pl.program_id(1) == 0)
    def _():
        state[...] = init_ref[0]

    def run(chunks):
        for gp in range(S5_PAIRS):
            re = slice(gp * PAIR_STATE, gp * PAIR_STATE + LANES)
            im = slice(gp * PAIR_STATE + LANES, (gp + 1) * PAIR_STATE)
            lr = lr_ref[0, :, gp * LANES:(gp + 1) * LANES]
            li = li_ref[0, :, gp * LANES:(gp + 1) * LANES]
            sr, si = state[:, re], state[:, im]
            for k in chunks:
                p_ref[:, k, re] = sr
                p_ref[:, k, im] = si
                sr, si = (lr * sr - li * si + e_ref[:, k, re],
                          lr * si + li * sr + e_ref[:, k, im])
            state[:, re] = sr
            state[:, im] = si

    n_chunks = e_ref.shape[1]
    direction = pl.program_id(0)
    pl.when(direction == 0)(lambda: run(tuple(range(n_chunks))))
    pl.when(direction == 1)(lambda: run(tuple(reversed(range(n_chunks)))))
    fin_ref[0] = state[...]


def _s5_out_kernel(u_ref, p_ref, toep_ref, csf_ref, csb_ref, y_ref):
    half = p_ref.shape[1] // 2
    grp = PAIR_IN // 2
    for gp in range(S5_PAIRS):
        pf = p_ref[:, gp * PAIR_STATE:(gp + 1) * PAIR_STATE].astype(BF16)
        pb = p_ref[:, half + gp * PAIR_STATE:half + (gp + 1) * PAIR_STATE].astype(BF16)
        st = (jnp.dot(pf, csf_ref[gp], preferred_element_type=F32)
              + jnp.dot(pb, csb_ref[gp], preferred_element_type=F32))
        for a in range(2):
            cols = slice(gp * PAIR_IN + a * grp, gp * PAIR_IN + (a + 1) * grp)
            y_ref[:, cols] = (jnp.dot(u_ref[:, cols], toep_ref[2 * gp + a],
                                      preferred_element_type=F32)
                              + st[:, a * grp:(a + 1) * grp])


def _s5_segment(u2, bsz, weights, init, layer):
    toep, we, csf, csb, lr, li = weights
    n_rows, width = u2.shape
    nc = n_rows // bsz
    tr = min(S5_ROW_TILE, n_rows)
    n_state = 2 * S5_PAIRS * PAIR_STATE
    row_blk = lambda w: pl.BlockSpec((tr, w), lambda j: (j, 0))
    exits = pl.pallas_call(
        _s5_exit_kernel,
        grid=(n_rows // tr,),
        in_specs=[row_blk(width), _layer_spec(we, layer)],
        out_specs=row_blk(n_state),
        out_shape=jax.ShapeDtypeStruct((n_rows, n_state), F32),
        compiler_params=_params(1),
        name="s5_exit",
    )(u2, we)

    nb = nc // S5_BLOCK_CHUNKS
    dir_blk = pl.BlockSpec((bsz, S5_BLOCK_CHUNKS, n_state // 2),
                           lambda d, k: (0, jnp.where(d == 0, k, nb - 1 - k), d))
    state_spec = pl.BlockSpec((1, bsz, n_state // 2), lambda d, k: (d, 0, 0))
    lam_spec = pl.BlockSpec((None, 1, 1, n_state // 4), lambda d, k: (layer, d, 0, 0))
    entering, leaving = pl.pallas_call(
        _s5_carry_kernel,
        grid=(2, nb),
        in_specs=[dir_blk, state_spec, lam_spec, lam_spec],
        out_specs=[dir_blk, state_spec],
        out_shape=[jax.ShapeDtypeStruct((bsz, nc, n_state), F32),
                   jax.ShapeDtypeStruct((2, bsz, n_state // 2), F32)],
        scratch_shapes=[pltpu.VMEM((bsz, n_state // 2), F32)],
        compiler_params=_params(2),
        name="s5_carry",
    )(exits.reshape(bsz, nc, n_state), init, lr, li)

    y2 = pl.pallas_call(
        _s5_out_kernel,
        grid=(n_rows // tr,),
        in_specs=[row_blk(width), row_blk(n_state), _layer_spec(toep, layer),
                  _layer_spec(csf, layer), _layer_spec(csb, layer)],
        out_specs=row_blk(width),
        out_shape=jax.ShapeDtypeStruct((n_rows, width), F32),
        compiler_params=_params(1),
        name="s5_out",
    )(u2, entering.reshape(n_rows, n_state), toep, csf, csb)
    return y2, leaving


def _pair_diag(m):
    g, r, c = m.shape
    eye = jnp.eye(2, dtype=m.dtype)
    return jnp.einsum('qarc,ab->qarbc', m.reshape(g // 2, 2, r, c), eye).reshape(g // 2, 2 * r, 2 * c)


def _s5_weights(lam_re, lam_im, log_dt, b_re, b_im, c_re, c_im):
    t = S5_T
    lam = lax.complex(lam_re.astype(F32), lam_im.astype(F32))
    lam_dt = lam * jnp.exp(log_dt.astype(F32))[..., None]
    lam_bar = jnp.exp(lam_dt)
    bb = ((lam_bar - 1) / lam)[..., None] * lax.complex(b_re.astype(F32), b_im.astype(F32))
    cm = lax.complex(c_re.astype(F32), c_im.astype(F32))
    steps = jnp.arange(t + 1, dtype=F32)
    pw = jnp.exp(lam_dt[None] * steps[:, None, None, None])

    kern = jnp.real(jnp.einsum('dgop,kdgp,dgpi->dkgoi', cm, pw[:t], bb))
    s_idx = jnp.arange(t)[:, None]
    t_idx = jnp.arange(t)[None, :]
    lag = t_idx - s_idx
    kf = jnp.where((lag >= 0)[..., None, None, None], kern[0][jnp.maximum(lag, 0)], 0.0)
    kb = jnp.where((lag <= 0)[..., None, None, None], kern[1][jnp.maximum(-lag, 0)], 0.0)
    toep = jnp.transpose(kf + kb, (2, 0, 4, 1, 3))
    toep = toep.reshape(SSM_GROUPS, t * SSM_GROUP, t * SSM_GROUP)

    ef = pw[t - 1 - jnp.arange(t), 0][..., None] * bb[0][None]
    eb = pw[jnp.arange(t), 1][..., None] * bb[1][None]
    to_cols = lambda m: jnp.transpose(m, (1, 0, 3, 2)).reshape(SSM_GROUPS, t * SSM_GROUP, SSM_STATE)
    we = jnp.concatenate([_pair_diag(to_cols(part(m))) for m in (ef, eb)
                          for part in (jnp.real, jnp.imag)], axis=-1)

    cf = cm[0][None] * pw[1 + jnp.arange(t), 0][:, :, None, :]
    cb = cm[1][None] * pw[t - jnp.arange(t), 1][:, :, None, :]
    to_rows = lambda m: jnp.transpose(m, (1, 3, 0, 2)).reshape(SSM_GROUPS, SSM_STATE, t * SSM_GROUP)
    cs = [jnp.concatenate([_pair_diag(to_rows(jnp.real(m))), _pair_diag(to_rows(-jnp.imag(m)))],
                          axis=1) for m in (cf, cb)]

    lam_t = pw[t].reshape(2, 1, N_STATE)
    return (toep.astype(BF16), we.astype(BF16), cs[0].astype(BF16), cs[1].astype(BF16),
            jnp.real(lam_t), jnp.imag(lam_t))


def _block_kernel(h_ref, attn_ref, cb_ref, z_ref, zp_ref, zn_ref, y2_ref, u_ref, mod_ref, g_ref,
                  cw_ref, dsk_ref, wglu_ref, bglu_ref, wout_ref, w1_ref, w2_ref, o_ref, y_scr,
                  mix_scr, *, tm, n_batch, n_tiles):
    step_id = pl.program_id(0)

    @pl.when(step_id == 0)
    def _():
        mix_scr[...] = jnp.zeros_like(mix_scr)

    j = jnp.minimum(step_id, n_tiles * n_batch - 1) // n_batch

    def step(new, old):
        mod = mod_ref[0]
        m = jnp.dot(mix_scr[old], wout_ref[...], preferred_element_type=F32)
        h1 = h_ref[0] + mod[2:3, :] * _rms(m, g_ref[1:2, :])
        a2 = (_rms(h1, g_ref[2:3, :]) * (1.0 + mod[4:5, :]) + mod[3:4, :]).astype(BF16)
        f = jnp.zeros((tm, D_MODEL), F32)
        for cidx in range(D_FF // FF_CHUNK):
            cs = slice(cidx * FF_CHUNK, (cidx + 1) * FF_CHUNK)
            t = jnp.maximum(jnp.dot(a2, w1_ref[:, cs], preferred_element_type=F32), 0.0)
            f = f + jnp.dot((t * t).astype(BF16), w2_ref[cs, :], preferred_element_type=F32)
        o_ref[0] = h1 + mod[5:6, :] * _rms(f, g_ref[3:4, :])

        z = z_ref[0]
        has_prev = jnp.where(j > 0, 1.0, 0.0)
        has_next = jnp.where(j < n_tiles - 1, 1.0, 0.0)
        z_before = zp_ref[0][SUBLANES - 1:SUBLANES, :] * has_prev
        z_after = zn_ref[0][0:1, :] * has_next
        row = lax.broadcasted_iota(jnp.int32, (tm, 1), 0)
        z_dn = jnp.where(row == 0, z_before, pltpu.roll(z, 1, axis=0))
        z_up = jnp.where(row == tm - 1, z_after, pltpu.roll(z, tm - 1, axis=0))
        conv = cb_ref[0] * (z_dn * cw_ref[0:1, :] + z * cw_ref[1:2, :] + z_up * cw_ref[2:3, :])
        n_chunks = y2_ref.shape[0]
        halves = S5_T * SSM_GROUP // LANES
        groups = [[y2_ref[:, (gi * halves + hf) * LANES:(gi * halves + hf + 1) * LANES]
                   for hf in range(halves)] for gi in range(SSM_GROUPS)]
        steps = _piece_transpose(groups)
        for s in range(S5_T):
            for hf in range(SSM_WIDTH // LANES):
                y_scr[hf, pl.ds(s, n_chunks, stride=S5_T), :] = steps[s][hf]
        y = jnp.concatenate([y_scr[hf] for hf in range(SSM_WIDTH // LANES)], axis=1)
        y = y + dsk_ref[...] * u_ref[0]
        gl = 0.5 * y * (1.0 + jnp.tanh(math.sqrt(2.0 / math.pi) * (y + 0.044715 * (y * y * y))))
        gate = jnp.dot(gl.astype(BF16), wglu_ref[...], preferred_element_type=F32) + bglu_ref[...]
        ssm = gl * _sigmoid(gate)
        a0, a1 = ATTN_WIDTH, ATTN_WIDTH + CONV_WIDTH
        mix_scr[new, :, 0:a0] = attn_ref[0]
        mix_scr[new, :, a0:a1] = conv.astype(BF16)
        mix_scr[new, :, a1:] = ssm.astype(BF16)

    for k in range(2):
        pl.when(step_id % 2 == k)(functools.partial(step, k, 1 - k))


def _block(h, attn, cb, z, y2, u, mods, mod_row, params, tm, layer):
    b, n, _ = h.shape
    nt = n // tm
    total = nt * b
    per = tm // SUBLANES
    n8 = n // SUBLANES

    def tile(step, lag):
        m = jnp.clip(step - lag, 0, total - 1)
        return m // b, m % b

    def row(w, lag):
        def index(step):
            j, i = tile(step, lag)
            return i, j, 0
        return pl.BlockSpec((1, tm, w), index)

    def halo(offset):
        def index(step):
            j, i = tile(step, 0)
            return i, jnp.clip(j * per + offset, 0, n8 - 1), 0
        return pl.BlockSpec((1, SUBLANES, CONV_WIDTH), index)

    def chunk_index(step):
        j, i = tile(step, 0)
        return i * nt + j, 0

    return pl.pallas_call(
        functools.partial(_block_kernel, tm=tm, n_batch=b, n_tiles=nt),
        grid=(total + 1,),
        in_specs=[row(D_MODEL, 1), row(ATTN_WIDTH, 0), row(CONV_WIDTH, 0), row(CONV_WIDTH, 0),
                  halo(-1), halo(per),
                  pl.BlockSpec((tm // S5_T, S5_T * SSM_WIDTH), chunk_index), row(SSM_WIDTH, 0),
                  pl.BlockSpec((1, N_MOD, D_MODEL), lambda step: (mod_row(tile(step, 1)[1]), 0, 0))]
        + [_layer_spec(p, layer) for p in params],
        out_specs=row(D_MODEL, 1),
        out_shape=jax.ShapeDtypeStruct((b, n, D_MODEL), F32),
        scratch_shapes=[pltpu.VMEM((SSM_WIDTH // LANES, tm, LANES), F32),
                        pltpu.VMEM((2, tm, MIX_WIDTH), BF16)],
        compiler_params=_params(1),
        name="mix_mlp_block",
    )(h, attn, cb, z, z, z, y2, u, mods, *params)


def _rope_tables(n_lat, n_ctx):
    f32 = np.float32
    rows = n_lat // GRID_W
    row = np.repeat(np.arange(rows), GRID_W).astype(f32)[:, None]
    col = np.tile(np.arange(GRID_W), rows).astype(f32)[:, None]
    freqs = (f32(ROPE_BASE) ** (-np.arange(ROPE_PAIRS, dtype=f32) / f32(ROPE_PAIRS))).astype(f32)
    ang_r, ang_c = row * freqs, col * freqs
    cos_h = np.concatenate([np.cos(ang_r)] * 2 + [np.cos(ang_c)] * 2, axis=-1)
    sin_h = np.concatenate([-np.sin(ang_r), np.sin(ang_r), -np.sin(ang_c), np.sin(ang_c)], axis=-1)
    scale = np.concatenate([np.full((ATTN_WIDTH,), LOG2E * HEAD_DIM ** -0.5, f32),
                            np.ones((KV_WIDTH,), f32)])
    n_heads = N_Q_HEADS + N_KV_HEADS
    cos = (np.tile(cos_h, (1, n_heads)) * scale).astype(f32)
    sin = (np.tile(sin_h, (1, n_heads)) * scale).astype(f32)
    cos_c = np.broadcast_to(scale, (n_ctx, ROPE_WIDTH)).astype(f32)
    sin_c = np.zeros((n_ctx, ROPE_WIDTH), f32)
    return tuple(jnp.asarray(t) for t in (cos, sin, cos_c, sin_c))


def kernel(x, c, ctx, c_ctx, w_ada, b_ada, norm_g, w_in, conv_w, attn_sink, ssm_lam_re, ssm_lam_im,
           ssm_log_dt, ssm_b_re, ssm_b_im, ssm_c_re, ssm_c_im, ssm_d, w_glu, b_glu, w_out,
           w_mlp_in, w_mlp_out):
    bsz, n_lat, _ = x.shape
    n_ctx = ctx.shape[1]
    tm_lat = min(ROW_TILE, n_lat)
    tm_ctx = min(ROW_TILE, n_ctx)
    cos, sin, cos_c, sin_c = _rope_tables(n_lat, n_ctx)

    pad = (-(bsz + 1)) % SUBLANES
    cvec = jnp.concatenate([c, c_ctx[None, :], jnp.zeros((pad, D_MODEL), F32)], axis=0)
    rows = cvec.shape[0]
    mods = _ada_mods(cvec, w_ada, b_ada).reshape(DEPTH * rows, N_MOD, D_MODEL)

    s5w = jax.vmap(_s5_weights)(ssm_lam_re, ssm_lam_im, ssm_log_dt, ssm_b_re, ssm_b_im,
                                ssm_c_re, ssm_c_im)
    w_in_b = w_in.astype(BF16)
    block_params = (norm_g, conv_w, ssm_d.reshape(DEPTH, 1, SSM_WIDTH), w_glu.astype(BF16),
                    b_glu.reshape(DEPTH, 1, SSM_WIDTH), w_out.astype(BF16),
                    w_mlp_in.astype(BF16), w_mlp_out.astype(BF16))
    zero_state = jnp.zeros((2, bsz, 2 * N_STATE), F32)

    h, hc = x, ctx
    for l in range(DEPTH):
        with_ctx_out = l < DEPTH - 1
        lat_row = lambda i, base=l * rows: base + i
        ctx_row = lambda i, base=l * rows + bsz: base
        qe, qo, k2, v4, cb, z, u, u2 = _in_proj(h, mods, lat_row, norm_g, w_in_b, cos, sin,
                                                tm_lat, True, l)
        qc, kc, vc, k2c, v4c, cbc, zc, uc, u2c = _in_proj(hc, mods, ctx_row, norm_g, w_in_b,
                                                          cos_c, sin_c, tm_ctx, False, l)

        attn = _attention_lat(attn_sink, l, qe, qo, k2, v4, k2c, v4c)

        y2c, ctx_state = _s5_segment(u2c, bsz, s5w, zero_state, l)
        y2, _ = _s5_segment(u2, bsz, s5w, ctx_state, l)

        h = _block(h, attn, cb, z, y2, u, mods, lat_row, block_params, tm_lat, l)
        if with_ctx_out:
            attn_c = _attention_ctx(attn_sink, l, qc, kc, vc)
            hc = _block(hc, attn_c, cbc, zc, y2c, uc, mods, ctx_row, block_params, tm_ctx, l)
    return h
```

```python
import functools
import math

import numpy as np
import jax
import jax.numpy as jnp
from jax import lax
from jax.experimental import pallas as pl
from jax.experimental.pallas import tpu as pltpu

F32 = jnp.float32
BF16 = jnp.bfloat16

D_MODEL = 1024
DEPTH = 4
GRID_W = 64
HEAD_DIM = 64
N_Q_HEADS = 8
N_KV_HEADS = 2
Q_PER_KV = N_Q_HEADS // N_KV_HEADS
ATTN_WIDTH = N_Q_HEADS * HEAD_DIM
KV_WIDTH = N_KV_HEADS * HEAD_DIM
WINDOW = 128
Q_BLOCK = 128
ROPE_BASE = 10000.0
ROPE_PAIRS = HEAD_DIM // 4
CONV_WIDTH = 256
SSM_WIDTH = 256
SSM_GROUP = 16
SSM_GROUPS = SSM_WIDTH // SSM_GROUP
SSM_STATE = 64
N_STATE = SSM_GROUPS * SSM_STATE
MIX_WIDTH = ATTN_WIDTH + CONV_WIDTH + SSM_WIDTH
IN_WIDTH = ATTN_WIDTH + 2 * KV_WIDTH + 3 * CONV_WIDTH + SSM_WIDTH
ROPE_WIDTH = ATTN_WIDTH + KV_WIDTH
D_FF = 4 * D_MODEL
N_MOD = 6
EPS = 1e-6
NEG_INF = -1e30
LOG2E = math.log2(math.e)

LANES = 128
SUBLANES = 8
VMEM_LIMIT = 56 * 1024 * 1024

ROW_TILE = 512
S5_T = 16
S5_BLOCK_CHUNKS = 8
S5_ROW_TILE = 256
S5_PAIRS = SSM_GROUPS // 2
PAIR_IN = 2 * SSM_GROUP * S5_T
PAIR_STATE = 4 * SSM_STATE
FF_CHUNK = 1024


def _layer_spec(arr, layer):
    nd = arr.ndim
    return pl.BlockSpec((None,) + arr.shape[1:], lambda *_: (layer,) + (0,) * (nd - 1),
                        pipeline_mode=pl.Buffered(1))


def _params(n_grid):
    return pltpu.CompilerParams(dimension_semantics=("arbitrary",) * n_grid,
                                vmem_limit_bytes=VMEM_LIMIT)


def _rms(x, g):
    ms = jnp.mean(x * x, axis=-1, keepdims=True)
    return x * lax.rsqrt(ms + EPS) * g


def _sigmoid(x):
    return 1.0 / (1.0 + jnp.exp(-x))


def _ada_kernel(c_ref, w_ref, b_ref, o_ref):
    c = c_ref[...]
    act = (c * _sigmoid(c)).astype(BF16)
    o_ref[0] = jnp.dot(act, w_ref[0].astype(BF16), preferred_element_type=F32) + b_ref[0]


def _ada_mods(cvec, w_ada, b_ada):
    rows = cvec.shape[0]
    tn = 1024
    n_out = N_MOD * D_MODEL
    return pl.pallas_call(
        _ada_kernel,
        grid=(DEPTH, n_out // tn),
        in_specs=[pl.BlockSpec((rows, D_MODEL), lambda l, j: (0, 0)),
                  pl.BlockSpec((1, D_MODEL, tn), lambda l, j: (l, 0, j)),
                  pl.BlockSpec((1, 1, tn), lambda l, j: (l, 0, j))],
        out_specs=pl.BlockSpec((1, rows, tn), lambda l, j: (l, 0, j)),
        out_shape=jax.ShapeDtypeStruct((DEPTH, rows, n_out), F32),
        compiler_params=_params(2),
        name="ada_mods",
    )(cvec, w_ada, b_ada.reshape(DEPTH, 1, n_out))


def _piece_transpose(src):
    per = LANES // SSM_GROUP
    n = src[0][0].shape[0]
    slot = lax.broadcasted_iota(jnp.int32, (n, LANES), 1) // SSM_GROUP

    def moved(a, bh, k):
        return src[a][bh] if k == 0 else pltpu.roll(src[a][bh], k * SSM_GROUP, axis=1)

    dst = []
    for b in range(len(src)):
        bh, bl = divmod(b, per)
        halves = []
        for ah in range(len(src) // per):
            out = moved(ah * per, bh, -bl % per)
            for j in range(1, per):
                out = jnp.where(slot == j, moved(ah * per + j, bh, (j - bl) % per), out)
            halves.append(out)
        dst.append(halves)
    return dst


def _inproj_kernel(*refs, lat):
    h_ref, mod_ref, g_ref, w_ref, cos_ref, sin_ref = refs[:6]
    if lat:
        qe_ref, qo_ref, k2_ref, v4_ref, cb_ref, z_ref, u_ref, u2_ref = refs[6:14]
    else:
        q_ref, k_ref, v_ref, k2_ref, v4_ref, cb_ref, z_ref, u_ref, u2_ref = refs[6:15]
    u_scr, a_scr, p_scr = refs[-3:]
    step_id = pl.program_id(0)

    @pl.when(step_id == 0)
    def _():
        a_scr[...] = jnp.zeros_like(a_scr)
        p_scr[...] = jnp.zeros_like(p_scr)

    lane = lax.broadcasted_iota(jnp.int32, (1, LANES), 1)
    first_of_pair = (lane & ROPE_PAIRS) == 0
    low_head = lane < HEAD_DIM

    def step(new, old):
        p_scr[old] = jnp.dot(a_scr[old], w_ref[...], preferred_element_type=F32)

        mod = mod_ref[0]
        a = _rms(h_ref[0], g_ref[0:1, :]) * (1.0 + mod[1:2, :]) + mod[0:1, :]
        a_scr[new] = a.astype(BF16)

        for s in range(ROPE_WIDTH // LANES):
            sl = slice(s * LANES, (s + 1) * LANES)
            t = p_scr[new, :, sl]
            partner = jnp.where(first_of_pair,
                                pltpu.roll(t, LANES - ROPE_PAIRS, axis=1),
                                pltpu.roll(t, ROPE_PAIRS, axis=1))
            r = t * cos_ref[:, sl] + partner * sin_ref[:, sl]
            if s < ATTN_WIDTH // LANES:
                if lat:
                    qe_ref[0, :, sl] = jnp.where(low_head, r, 0.0).astype(BF16)
                    qo_ref[0, :, sl] = jnp.where(low_head, 0.0, r).astype(BF16)
                else:
                    q_ref[0, :, sl] = r.astype(BF16)
            else:
                k2_ref[0, :, 0:LANES] = r.astype(BF16)
                k2_ref[0, :, LANES:2 * LANES] = pltpu.roll(r, HEAD_DIM, axis=1).astype(BF16)
                if not lat:
                    k_ref[0] = r.astype(BF16)
        o = ROPE_WIDTH
        v = p_scr[new, :, o:o + KV_WIDTH]
        v_swapped = pltpu.roll(v, HEAD_DIM, axis=1)
        for idx, part in enumerate((jnp.where(low_head, v, 0.0),
                                    jnp.where(low_head, 0.0, v_swapped),
                                    jnp.where(low_head, v_swapped, 0.0),
                                    jnp.where(low_head, 0.0, v))):
            v4_ref[0, :, idx * LANES:(idx + 1) * LANES] = part.astype(BF16)
        if not lat:
            v_ref[0] = v.astype(BF16)
        o += KV_WIDTH
        cb_ref[0] = p_scr[new, :, o:o + CONV_WIDTH]
        o += CONV_WIDTH
        z_ref[0] = (p_scr[new, :, o:o + CONV_WIDTH]
                    * p_scr[new, :, o + CONV_WIDTH:o + 2 * CONV_WIDTH])
        o += 2 * CONV_WIDTH
        u_ref[0] = p_scr[new, :, o:o + SSM_WIDTH]
        n_chunks = u2_ref.shape[0]
        halves = SSM_WIDTH // LANES
        for hf in range(halves):
            u_scr[hf] = p_scr[new, :, o + hf * LANES:o + (hf + 1) * LANES]
        steps = [[u_scr[hf, pl.ds(s, n_chunks, stride=S5_T), :]
                  for hf in range(halves)] for s in range(S5_T)]
        groups = _piece_transpose(steps)
        for gi in range(SSM_GROUPS):
            for hf in range(halves):
                c0 = gi * SSM_GROUP * S5_T + hf * LANES
                u2_ref[:, c0:c0 + LANES] = groups[gi][hf].astype(BF16)

    for k in range(2):
        pl.when(step_id % 2 == k)(functools.partial(step, k, 1 - k))


def _in_proj(h, mods, mod_row, norm_g, w_in, cos, sin, tm, lat, layer):
    b, n, _ = h.shape
    nt = n // tm
    total = nt * b
    if lat:
        widths = [(ATTN_WIDTH, BF16), (ATTN_WIDTH, BF16)]
    else:
        widths = [(ATTN_WIDTH, BF16), (KV_WIDTH, BF16), (KV_WIDTH, BF16)]
    widths += [(2 * KV_WIDTH, BF16), (4 * KV_WIDTH, BF16),
               (CONV_WIDTH, F32), (CONV_WIDTH, F32), (SSM_WIDTH, F32)]

    def tile(step, lag):
        m = jnp.clip(step - lag, 0, total - 1)
        return m // b, m % b

    def row(w, lag):
        def index(step):
            j, i = tile(step, lag)
            return i, j, 0
        return pl.BlockSpec((1, tm, w), index)

    def chunk_index(step):
        j, i = tile(step, 2)
        return i * nt + j, 0

    table = pl.BlockSpec((tm, ROPE_WIDTH), lambda step: (tile(step, 2)[0], 0))
    return pl.pallas_call(
        functools.partial(_inproj_kernel, lat=lat),
        grid=(total + 2,),
        in_specs=[row(D_MODEL, 0),
                  pl.BlockSpec((1, N_MOD, D_MODEL), lambda step: (mod_row(tile(step, 0)[1]), 0, 0)),
                  _layer_spec(norm_g, layer),
                  _layer_spec(w_in, layer),
                  table, table],
        out_specs=[row(w, 2) for w, _ in widths]
        + [pl.BlockSpec((tm // S5_T, S5_T * SSM_WIDTH), chunk_index)],
        out_shape=[jax.ShapeDtypeStruct((b, n, w), dt) for w, dt in widths]
        + [jax.ShapeDtypeStruct((b * n // S5_T, S5_T * SSM_WIDTH), BF16)],
        scratch_shapes=[pltpu.VMEM((SSM_WIDTH // LANES, tm, LANES), F32),
                        pltpu.VMEM((2, tm, D_MODEL), BF16),
                        pltpu.VMEM((2, tm, IN_WIDTH), F32)],
        compiler_params=_params(1),
        name="in_proj",
    )(h, mods, norm_g, w_in, cos, sin)


def _softmax_parts(s, sink):
    m = jnp.maximum(jnp.max(s, axis=-1, keepdims=True), sink)
    p = jnp.exp2(s - m)
    denom = jnp.sum(p, axis=-1, keepdims=True) + jnp.exp2(sink - m)
    return p, 1.0 / denom


def _attn_ctx_kernel(sink_ref, q_ref, k_ref, v_ref, o_ref, *, layer):
    q, k, v = q_ref[0], k_ref[0], v_ref[0]
    outs = []
    for h in range(N_Q_HEADS):
        hk = h // Q_PER_KV
        ksl = slice(hk * HEAD_DIM, (hk + 1) * HEAD_DIM)
        s = lax.dot_general(q[:, h * HEAD_DIM:(h + 1) * HEAD_DIM], k[:, ksl],
                            (((1,), (1,)), ((), ())), preferred_element_type=F32)
        p, inv = _softmax_parts(s, sink_ref[layer, h] * LOG2E)
        outs.append(jnp.dot(p.astype(BF16), v[:, ksl], preferred_element_type=F32) * inv)
    o_ref[0] = jnp.concatenate(outs, axis=1).astype(BF16)


def _attn_lat_kernel(sink_ref, qe_ref, qo_ref, k_ref, kx_ref, v_ref, vx_ref, bias_ref, o_ref,
                     s_scr, p_scr, inv_scr, *, tq, layer, n_blocks, total):
    n = pl.program_id(0)

    def window(lag):
        j = jnp.clip(n - lag, 0, total - 1) % n_blocks
        return pl.multiple_of(jnp.clip(j - 1, 0, n_blocks - 3) * tq, tq)

    @pl.when(n == 0)
    def _():
        s_scr[...] = jnp.zeros_like(s_scr)
        p_scr[...] = jnp.zeros_like(p_scr)
        inv_scr[...] = jnp.zeros_like(inv_scr)

    lane = lax.broadcasted_iota(jnp.int32, (1, LANES), 1)
    low_head = lane < HEAD_DIM
    slabs = ATTN_WIDTH // LANES

    def step(new, old):
        vall = jnp.concatenate([v_ref[0, pl.ds(window(2), 3 * tq), :], vx_ref[0]], axis=0)
        for t in range(slabs):
            c0 = (t // (slabs // N_KV_HEADS)) * 2 * LANES
            o = (jnp.dot(p_scr[new, 2 * t], vall[:, c0:c0 + LANES], preferred_element_type=F32)
                 + jnp.dot(p_scr[new, 2 * t + 1], vall[:, c0 + LANES:c0 + 2 * LANES],
                           preferred_element_type=F32))
            o_ref[0, :, t * LANES:(t + 1) * LANES] = (o * inv_scr[new, t]).astype(BF16)

        kall = jnp.concatenate([k_ref[0, pl.ds(window(0), 3 * tq), :], kx_ref[0]], axis=0)
        for hk in range(N_KV_HEADS):
            t0 = hk * (slabs // N_KV_HEADS)
            for par, q_ref in enumerate((qe_ref, qo_ref)):
                lhs = jnp.concatenate([q_ref[0, :, t0 * LANES:(t0 + 1) * LANES],
                                       q_ref[0, :, (t0 + 1) * LANES:(t0 + 2) * LANES]], axis=0)
                kv = (hk + par) % 2
                s = lax.dot_general(lhs, kall[:, kv * LANES:(kv + 1) * LANES],
                                    (((1,), (1,)), ((), ())), preferred_element_type=F32)
                s_scr[new, 2 * t0 + par] = s[:tq]
                s_scr[new, 2 * t0 + 2 + par] = s[tq:]

        invs = []
        for h in range(N_Q_HEADS):
            p, inv = _softmax_parts(s_scr[old, h] + bias_ref[0], sink_ref[layer, h] * LOG2E)
            p_scr[old, h] = p.astype(BF16)
            invs.append(inv)
        for t in range(slabs):
            inv_scr[old, t] = jnp.where(low_head, invs[2 * t], invs[2 * t + 1])

    for k in range(2):
        pl.when(n % 2 == k)(functools.partial(step, k, 1 - k))


def _window_bias(tq, n_ctx):
    r = np.arange(tq)[:, None]
    c = np.arange(3 * tq + n_ctx)[None, :]
    variants = []
    for q_start in (0, tq, 2 * tq):
        ok = (c >= 3 * tq) | (np.abs(c - (q_start + r)) <= WINDOW)
        variants.append(np.where(ok, 0.0, NEG_INF))
    return jnp.asarray(np.stack(variants), F32)


def _attention_lat(sink, layer, qe, qo, k2, v4, k2x, v4x):
    b, n, _ = qe.shape
    n_ctx = k2x.shape[1]
    tq = Q_BLOCK
    nb = n // tq
    total = b * nb
    n_keys = 3 * tq + n_ctx
    assert tq == WINDOW and nb >= 3

    def block_of(step, lag):
        m = jnp.clip(step - lag, 0, total - 1)
        return m // nb, m % nb

    def rows(width, lag):
        return pl.BlockSpec((1, tq, width), lambda step: (*block_of(step, lag), 0))

    def whole(length, width, lag):
        return pl.BlockSpec((1, length, width), lambda step: (block_of(step, lag)[0], 0, 0))

    def bias_index(step):
        j = block_of(step, 1)[1]
        return jnp.where(j == 0, 0, jnp.where(j == nb - 1, 2, 1)), 0, 0

    kw, vw = 2 * KV_WIDTH, 4 * KV_WIDTH
    return pl.pallas_call(
        functools.partial(_attn_lat_kernel, tq=tq, layer=layer, n_blocks=nb, total=total),
        grid=(total + 2,),
        in_specs=[pl.BlockSpec(memory_space=pltpu.SMEM),
                  rows(ATTN_WIDTH, 0), rows(ATTN_WIDTH, 0),
                  whole(n, kw, 0), whole(n_ctx, kw, 0),
                  whole(n, vw, 2), whole(n_ctx, vw, 2),
                  pl.BlockSpec((1, tq, n_keys), bias_index)],
        out_specs=rows(ATTN_WIDTH, 2),
        out_shape=jax.ShapeDtypeStruct((b, n, ATTN_WIDTH), BF16),
        scratch_shapes=[pltpu.VMEM((2, N_Q_HEADS, tq, n_keys), F32),
                        pltpu.VMEM((2, N_Q_HEADS, tq, n_keys), BF16),
                        pltpu.VMEM((2, ATTN_WIDTH // LANES, tq, LANES), F32)],
        compiler_params=_params(1),
        name="attn_lat",
    )(sink, qe, qo, k2, k2x, v4, v4x, _window_bias(tq, n_ctx))


def _attention_ctx(sink, layer, q, k, v):
    b, n, _ = q.shape
    spec = lambda w: pl.BlockSpec((1, n, w), lambda i: (i, 0, 0))
    return pl.pallas_call(
        functools.partial(_attn_ctx_kernel, layer=layer),
        grid=(b,),
        in_specs=[pl.BlockSpec(memory_space=pltpu.SMEM), spec(ATTN_WIDTH), spec(KV_WIDTH),
                  spec(KV_WIDTH)],
        out_specs=spec(ATTN_WIDTH),
        out_shape=jax.ShapeDtypeStruct((b, n, ATTN_WIDTH), BF16),
        compiler_params=_params(1),
        name="attn_ctx",
    )(sink, q, k, v)


def _s5_exit_kernel(u_ref, we_ref, e_ref):
    half = e_ref.shape[1] // 2
    for gp in range(S5_PAIRS):
        r = jnp.dot(u_ref[:, gp * PAIR_IN:(gp + 1) * PAIR_IN], we_ref[gp],
                    preferred_element_type=F32)
        e_ref[:, gp * PAIR_STATE:(gp + 1) * PAIR_STATE] = r[:, :PAIR_STATE]
        e_ref[:, half + gp * PAIR_STATE:half + (gp + 1) * PAIR_STATE] = r[:, PAIR_STATE:]


def _s5_carry_kernel(e_ref, init_ref, lr_ref, li_ref, p_ref, fin_ref, state):
    @pl.when(pl.program_id(1) == 0)
    def _():
        state[...] = init_ref[0]

    def run(chunks):
        for gp in range(S5_PAIRS):
            re = slice(gp * PAIR_STATE, gp * PAIR_STATE + LANES)
            im = slice(gp * PAIR_STATE + LANES, (gp + 1) * PAIR_STATE)
            lr = lr_ref[0, :, gp * LANES:(gp + 1) * LANES]
            li = li_ref[0, :, gp * LANES:(gp + 1) * LANES]
            sr, si = state[:, re], state[:, im]
            for k in chunks:
                p_ref[:, k, re] = sr
                p_ref[:, k, im] = si
                sr, si = (lr * sr - li * si + e_ref[:, k, re],
                          lr * si + li * sr + e_ref[:, k, im])
            state[:, re] = sr
            state[:, im] = si

    n_chunks = e_ref.shape[1]
    direction = pl.program_id(0)
    pl.when(direction == 0)(lambda: run(tuple(range(n_chunks))))
    pl.when(direction == 1)(lambda: run(tuple(reversed(range(n_chunks)))))
    fin_ref[0] = state[...]


def _s5_out_kernel(u_ref, p_ref, toep_ref, csf_ref, csb_ref, y_ref):
    half = p_ref.shape[1] // 2
    grp = PAIR_IN // 2
    for gp in range(S5_PAIRS):
        pf = p_ref[:, gp * PAIR_STATE:(gp + 1) * PAIR_STATE].astype(BF16)
        pb = p_ref[:, half + gp * PAIR_STATE:half + (gp + 1) * PAIR_STATE].astype(BF16)
        st = (jnp.dot(pf, csf_ref[gp], preferred_element_type=F32)
              + jnp.dot(pb, csb_ref[gp], preferred_element_type=F32))
        for a in range(2):
            cols = slice(gp * PAIR_IN + a * grp, gp * PAIR_IN + (a + 1) * grp)
            y_ref[:, cols] = (jnp.dot(u_ref[:, cols], toep_ref[2 * gp + a],
                                      preferred_element_type=F32)
                              + st[:, a * grp:(a + 1) * grp])


def _s5_segment(u2, bsz, weights, init, layer):
    toep, we, csf, csb, lr, li = weights
    n_rows, width = u2.shape
    nc = n_rows // bsz
    tr = min(S5_ROW_TILE, n_rows)
    n_state = 2 * S5_PAIRS * PAIR_STATE
    row_blk = lambda w: pl.BlockSpec((tr, w), lambda j: (j, 0))
    exits = pl.pallas_call(
        _s5_exit_kernel,
        grid=(n_rows // tr,),
        in_specs=[row_blk(width), _layer_spec(we, layer)],
        out_specs=row_blk(n_state),
        out_shape=jax.ShapeDtypeStruct((n_rows, n_state), F32),
        compiler_params=_params(1),
        name="s5_exit",
    )(u2, we)

    nb = nc // S5_BLOCK_CHUNKS
    dir_blk = pl.BlockSpec((bsz, S5_BLOCK_CHUNKS, n_state // 2),
                           lambda d, k: (0, jnp.where(d == 0, k, nb - 1 - k), d))
    state_spec = pl.BlockSpec((1, bsz, n_state // 2), lambda d, k: (d, 0, 0))
    lam_spec = pl.BlockSpec((None, 1, 1, n_state // 4), lambda d, k: (layer, d, 0, 0))
    entering, leaving = pl.pallas_call(
        _s5_carry_kernel,
        grid=(2, nb),
        in_specs=[dir_blk, state_spec, lam_spec, lam_spec],
        out_specs=[dir_blk, state_spec],
        out_shape=[jax.ShapeDtypeStruct((bsz, nc, n_state), F32),
                   jax.ShapeDtypeStruct((2, bsz, n_state // 2), F32)],
        scratch_shapes=[pltpu.VMEM((bsz, n_state // 2), F32)],
        compiler_params=_params(2),
        name="s5_carry",
    )(exits.reshape(bsz, nc, n_state), init, lr, li)

    y2 = pl.pallas_call(
        _s5_out_kernel,
        grid=(n_rows // tr,),
        in_specs=[row_blk(width), row_blk(n_state), _layer_spec(toep, layer),
                  _layer_spec(csf, layer), _layer_spec(csb, layer)],
        out_specs=row_blk(width),
        out_shape=jax.ShapeDtypeStruct((n_rows, width), F32),
        compiler_params=_params(1),
        name="s5_out",
    )(u2, entering.reshape(n_rows, n_state), toep, csf, csb)
    return y2, leaving


def _pair_diag(m):
    g, r, c = m.shape
    eye = jnp.eye(2, dtype=m.dtype)
    return jnp.einsum('qarc,ab->qarbc', m.reshape(g // 2, 2, r, c), eye).reshape(g // 2, 2 * r, 2 * c)


def _s5_weights(lam_re, lam_im, log_dt, b_re, b_im, c_re, c_im):
    t = S5_T
    lam = lax.complex(lam_re.astype(F32), lam_im.astype(F32))
    lam_dt = lam * jnp.exp(log_dt.astype(F32))[..., None]
    lam_bar = jnp.exp(lam_dt)
    bb = ((lam_bar - 1) / lam)[..., None] * lax.complex(b_re.astype(F32), b_im.astype(F32))
    cm = lax.complex(c_re.astype(F32), c_im.astype(F32))
    steps = jnp.arange(t + 1, dtype=F32)
    pw = jnp.exp(lam_dt[None] * steps[:, None, None, None])

    kern = jnp.real(jnp.einsum('dgop,kdgp,dgpi->dkgoi', cm, pw[:t], bb))
    s_idx = jnp.arange(t)[:, None]
    t_idx = jnp.arange(t)[None, :]
    lag = t_idx - s_idx
    kf = jnp.where((lag >= 0)[..., None, None, None], kern[0][jnp.maximum(lag, 0)], 0.0)
    kb = jnp.where((lag <= 0)[..., None, None, None], kern[1][jnp.maximum(-lag, 0)], 0.0)
    toep = jnp.transpose(kf + kb, (2, 0, 4, 1, 3))
    toep = toep.reshape(SSM_GROUPS, t * SSM_GROUP, t * SSM_GROUP)

    ef = pw[t - 1 - jnp.arange(t), 0][..., None] * bb[0][None]
    eb = pw[jnp.arange(t), 1][..., None] * bb[1][None]
    to_cols = lambda m: jnp.transpose(m, (1, 0, 3, 2)).reshape(SSM_GROUPS, t * SSM_GROUP, SSM_STATE)
    we = jnp.concatenate([_pair_diag(to_cols(part(m))) for m in (ef, eb)
                          for part in (jnp.real, jnp.imag)], axis=-1)

    cf = cm[0][None] * pw[1 + jnp.arange(t), 0][:, :, None, :]
    cb = cm[1][None] * pw[t - jnp.arange(t), 1][:, :, None, :]
    to_rows = lambda m: jnp.transpose(m, (1, 3, 0, 2)).reshape(SSM_GROUPS, SSM_STATE, t * SSM_GROUP)
    cs = [jnp.concatenate([_pair_diag(to_rows(jnp.real(m))), _pair_diag(to_rows(-jnp.imag(m)))],
                          axis=1) for m in (cf, cb)]

    lam_t = pw[t].reshape(2, 1, N_STATE)
    return (toep.astype(BF16), we.astype(BF16), cs[0].astype(BF16), cs[1].astype(BF16),
            jnp.real(lam_t), jnp.imag(lam_t))


def _block_kernel(h_ref, attn_ref, cb_ref, z_ref, zp_ref, zn_ref, y2_ref, u_ref, mod_ref, g_ref,
                  cw_ref, dsk_ref, wglu_ref, bglu_ref, wout_ref, w1_ref, w2_ref, o_ref, y_scr,
                  mix_scr, *, tm, n_batch, n_tiles):
    step_id = pl.program_id(0)

    @pl.when(step_id == 0)
    def _():
        mix_scr[...] = jnp.zeros_like(mix_scr)

    j = jnp.minimum(step_id, n_tiles * n_batch - 1) // n_batch

    def step(new, old):
        mod = mod_ref[0]
        m = jnp.dot(mix_scr[old], wout_ref[...], preferred_element_type=F32)
        h1 = h_ref[0] + mod[2:3, :] * _rms(m, g_ref[1:2, :])
        a2 = (_rms(h1, g_ref[2:3, :]) * (1.0 + mod[4:5, :]) + mod[3:4, :]).astype(BF16)
        f = jnp.zeros((tm, D_MODEL), F32)
        for cidx in range(D_FF // FF_CHUNK):
            cs = slice(cidx * FF_CHUNK, (cidx + 1) * FF_CHUNK)
            t = jnp.maximum(jnp.dot(a2, w1_ref[:, cs], preferred_element_type=F32), 0.0)
            f = f + jnp.dot((t * t).astype(BF16), w2_ref[cs, :], preferred_element_type=F32)
        o_ref[0] = h1 + mod[5:6, :] * _rms(f, g_ref[3:4, :])

        z = z_ref[0]
        has_prev = jnp.where(j > 0, 1.0, 0.0)
        has_next = jnp.where(j < n_tiles - 1, 1.0, 0.0)
        z_before = zp_ref[0][SUBLANES - 1:SUBLANES, :] * has_prev
        z_after = zn_ref[0][0:1, :] * has_next
        row = lax.broadcasted_iota(jnp.int32, (tm, 1), 0)
        z_dn = jnp.where(row == 0, z_before, pltpu.roll(z, 1, axis=0))
        z_up = jnp.where(row == tm - 1, z_after, pltpu.roll(z, tm - 1, axis=0))
        conv = cb_ref[0] * (z_dn * cw_ref[0:1, :] + z * cw_ref[1:2, :] + z_up * cw_ref[2:3, :])
        n_chunks = y2_ref.shape[0]
        halves = S5_T * SSM_GROUP // LANES
        groups = [[y2_ref[:, (gi * halves + hf) * LANES:(gi * halves + hf + 1) * LANES]
                   for hf in range(halves)] for gi in range(SSM_GROUPS)]
        steps = _piece_transpose(groups)
        for s in range(S5_T):
            for hf in range(SSM_WIDTH // LANES):
                y_scr[hf, pl.ds(s, n_chunks, stride=S5_T), :] = steps[s][hf]
        y = jnp.concatenate([y_scr[hf] for hf in range(SSM_WIDTH // LANES)], axis=1)
        y = y + dsk_ref[...] * u_ref[0]
        gl = 0.5 * y * (1.0 + jnp.tanh(math.sqrt(2.0 / math.pi) * (y + 0.044715 * (y * y * y))))
        gate = jnp.dot(gl.astype(BF16), wglu_ref[...], preferred_element_type=F32) + bglu_ref[...]
        ssm = gl * _sigmoid(gate)
        a0, a1 = ATTN_WIDTH, ATTN_WIDTH + CONV_WIDTH
        mix_scr[new, :, 0:a0] = attn_ref[0]
        mix_scr[new, :, a0:a1] = conv.astype(BF16)
        mix_scr[new, :, a1:] = ssm.astype(BF16)

    for k in range(2):
        pl.when(step_id % 2 == k)(functools.partial(step, k, 1 - k))


def _block(h, attn, cb, z, y2, u, mods, mod_row, params, tm, layer):
    b, n, _ = h.shape
    nt = n // tm
    total = nt * b
    per = tm // SUBLANES
    n8 = n // SUBLANES

    def tile(step, lag):
        m = jnp.clip(step - lag, 0, total - 1)
        return m // b, m % b

    def row(w, lag):
        def index(step):
            j, i = tile(step, lag)
            return i, j, 0
        return pl.BlockSpec((1, tm, w), index)

    def halo(offset):
        def index(step):
            j, i = tile(step, 0)
            return i, jnp.clip(j * per + offset, 0, n8 - 1), 0
        return pl.BlockSpec((1, SUBLANES, CONV_WIDTH), index)

    def chunk_index(step):
        j, i = tile(step, 0)
        return i * nt + j, 0

    return pl.pallas_call(
        functools.partial(_block_kernel, tm=tm, n_batch=b, n_tiles=nt),
        grid=(total + 1,),
        in_specs=[row(D_MODEL, 1), row(ATTN_WIDTH, 0), row(CONV_WIDTH, 0), row(CONV_WIDTH, 0),
                  halo(-1), halo(per),
                  pl.BlockSpec((tm // S5_T, S5_T * SSM_WIDTH), chunk_index), row(SSM_WIDTH, 0),
                  pl.BlockSpec((1, N_MOD, D_MODEL), lambda step: (mod_row(tile(step, 1)[1]), 0, 0))]
        + [_layer_spec(p, layer) for p in params],
        out_specs=row(D_MODEL, 1),
        out_shape=jax.ShapeDtypeStruct((b, n, D_MODEL), F32),
        scratch_shapes=[pltpu.VMEM((SSM_WIDTH // LANES, tm, LANES), F32),
                        pltpu.VMEM((2, tm, MIX_WIDTH), BF16)],
        compiler_params=_params(1),
        name="mix_mlp_block",
    )(h, attn, cb, z, z, z, y2, u, mods, *params)


def _rope_tables(n_lat, n_ctx):
    f32 = np.float32
    rows = n_lat // GRID_W
    row = np.repeat(np.arange(rows), GRID_W).astype(f32)[:, None]
    col = np.tile(np.arange(GRID_W), rows).astype(f32)[:, None]
    freqs = (f32(ROPE_BASE) ** (-np.arange(ROPE_PAIRS, dtype=f32) / f32(ROPE_PAIRS))).astype(f32)
    ang_r, ang_c = row * freqs, col * freqs
    cos_h = np.concatenate([np.cos(ang_r)] * 2 + [np.cos(ang_c)] * 2, axis=-1)
    sin_h = np.concatenate([-np.sin(ang_r), np.sin(ang_r), -np.sin(ang_c), np.sin(ang_c)], axis=-1)
    scale = np.concatenate([np.full((ATTN_WIDTH,), LOG2E * HEAD_DIM ** -0.5, f32),
                            np.ones((KV_WIDTH,), f32)])
    n_heads = N_Q_HEADS + N_KV_HEADS
    cos = (np.tile(cos_h, (1, n_heads)) * scale).astype(f32)
    sin = (np.tile(sin_h, (1, n_heads)) * scale).astype(f32)
    cos_c = np.broadcast_to(scale, (n_ctx, ROPE_WIDTH)).astype(f32)
    sin_c = np.zeros((n_ctx, ROPE_WIDTH), f32)
    return tuple(jnp.asarray(t) for t in (cos, sin, cos_c, sin_c))


def kernel(x, c, ctx, c_ctx, w_ada, b_ada, norm_g, w_in, conv_w, attn_sink, ssm_lam_re, ssm_lam_im,
           ssm_log_dt, ssm_b_re, ssm_b_im, ssm_c_re, ssm_c_im, ssm_d, w_glu, b_glu, w_out,
           w_mlp_in, w_mlp_out):
    bsz, n_lat, _ = x.shape
    n_ctx = ctx.shape[1]
    tm_lat = min(ROW_TILE, n_lat)
    tm_ctx = min(ROW_TILE, n_ctx)
    cos, sin, cos_c, sin_c = _rope_tables(n_lat, n_ctx)

    pad = (-(bsz + 1)) % SUBLANES
    cvec = jnp.concatenate([c, c_ctx[None, :], jnp.zeros((pad, D_MODEL), F32)], axis=0)
    rows = cvec.shape[0]
    mods = _ada_mods(cvec, w_ada, b_ada).reshape(DEPTH * rows, N_MOD, D_MODEL)

    s5w = jax.vmap(_s5_weights)(ssm_lam_re, ssm_lam_im, ssm_log_dt, ssm_b_re, ssm_b_im,
                                ssm_c_re, ssm_c_im)
    w_in_b = w_in.astype(BF16)
    block_params = (norm_g, conv_w, ssm_d.reshape(DEPTH, 1, SSM_WIDTH), w_glu.astype(BF16),
                    b_glu.reshape(DEPTH, 1, SSM_WIDTH), w_out.astype(BF16),
                    w_mlp_in.astype(BF16), w_mlp_out.astype(BF16))
    zero_state = jnp.zeros((2, bsz, 2 * N_STATE), F32)

    h, hc = x, ctx
    for l in range(DEPTH):
        with_ctx_out = l < DEPTH - 1
        lat_row = lambda i, base=l * rows: base + i
        ctx_row = lambda i, base=l * rows + bsz: base
        qe, qo, k2, v4, cb, z, u, u2 = _in_proj(h, mods, lat_row, norm_g, w_in_b, cos, sin,
                                                tm_lat, True, l)
        qc, kc, vc, k2c, v4c, cbc, zc, uc, u2c = _in_proj(hc, mods, ctx_row, norm_g, w_in_b,
                                                          cos_c, sin_c, tm_ctx, False, l)

        attn = _attention_lat(attn_sink, l, qe, qo, k2, v4, k2c, v4c)

        y2c, ctx_state = _s5_segment(u2c, bsz, s5w, zero_state, l)
        y2, _ = _s5_segment(u2, bsz, s5w, ctx_state, l)

        h = _block(h, attn, cb, z, y2, u, mods, lat_row, block_params, tm_lat, l)
        if with_ctx_out:
            attn_c = _attention_ctx(attn_sink, l, qc, kc, vc)
            hc = _block(hc, attn_c, cbc, zc, y2c, uc, mods, ctx_row, block_params, tm_ctx, l)
    return h
```

```python
import functools
import math

import numpy as np
import jax
import jax.numpy as jnp
from jax import lax
from jax.experimental import pallas as pl
from jax.experimental.pallas import tpu as pltpu

F32 = jnp.float32
BF16 = jnp.bfloat16

D_MODEL = 1024
DEPTH = 4
GRID_W = 64
HEAD_DIM = 64
N_Q_HEADS = 8
N_KV_HEADS = 2
Q_PER_KV = N_Q_HEADS // N_KV_HEADS
ATTN_WIDTH = N_Q_HEADS * HEAD_DIM
KV_WIDTH = N_KV_HEADS * HEAD_DIM
WINDOW = 128
Q_BLOCK = 128
ROPE_BASE = 10000.0
ROPE_PAIRS = HEAD_DIM // 4
CONV_WIDTH = 256
SSM_WIDTH = 256
SSM_GROUP = 16
SSM_GROUPS = SSM_WIDTH // SSM_GROUP
SSM_STATE = 64
N_STATE = SSM_GROUPS * SSM_STATE
MIX_WIDTH = ATTN_WIDTH + CONV_WIDTH + SSM_WIDTH
IN_WIDTH = ATTN_WIDTH + 2 * KV_WIDTH + 3 * CONV_WIDTH + SSM_WIDTH
ROPE_WIDTH = ATTN_WIDTH + KV_WIDTH
D_FF = 4 * D_MODEL
N_MOD = 6
EPS = 1e-6
NEG_INF = -1e30
LOG2E = math.log2(math.e)

LANES = 128
SUBLANES = 8
VMEM_LIMIT = 56 * 1024 * 1024

ROW_TILE = 512
ATTN_GROUP = 2
S5_T = 16
S5_BLOCK_CHUNKS = 8
S5_ROW_TILE = 256
S5_PAIRS = SSM_GROUPS // 2
PAIR_IN = 2 * SSM_GROUP * S5_T
PAIR_STATE = 4 * SSM_STATE
FF_CHUNK = 1024


def _layer_spec(arr, layer):
    nd = arr.ndim
    return pl.BlockSpec((None,) + arr.shape[1:], lambda *_: (layer,) + (0,) * (nd - 1),
                        pipeline_mode=pl.Buffered(1))


def _params(n_grid):
    return pltpu.CompilerParams(dimension_semantics=("arbitrary",) * n_grid,
                                vmem_limit_bytes=VMEM_LIMIT)


def _rms(x, g):
    ms = jnp.mean(x * x, axis=-1, keepdims=True)
    return x * lax.rsqrt(ms + EPS) * g


def _sigmoid(x):
    return 1.0 / (1.0 + jnp.exp(-x))


def _ada_kernel(c_ref, w_ref, b_ref, o_ref):
    c = c_ref[...]
    act = (c * _sigmoid(c)).astype(BF16)
    o_ref[0] = jnp.dot(act, w_ref[0].astype(BF16), preferred_element_type=F32) + b_ref[0]


def _ada_mods(cvec, w_ada, b_ada):
    rows = cvec.shape[0]
    tn = 1024
    n_out = N_MOD * D_MODEL
    return pl.pallas_call(
        _ada_kernel,
        grid=(DEPTH, n_out // tn),
        in_specs=[pl.BlockSpec((rows, D_MODEL), lambda l, j: (0, 0)),
                  pl.BlockSpec((1, D_MODEL, tn), lambda l, j: (l, 0, j)),
                  pl.BlockSpec((1, 1, tn), lambda l, j: (l, 0, j))],
        out_specs=pl.BlockSpec((1, rows, tn), lambda l, j: (l, 0, j)),
        out_shape=jax.ShapeDtypeStruct((DEPTH, rows, n_out), F32),
        compiler_params=_params(2),
        name="ada_mods",
    )(cvec, w_ada, b_ada.reshape(DEPTH, 1, n_out))


def _piece_transpose(src):
    per = LANES // SSM_GROUP
    n = src[0][0].shape[0]
    slot = lax.broadcasted_iota(jnp.int32, (n, LANES), 1) // SSM_GROUP

    def moved(a, bh, k):
        return src[a][bh] if k == 0 else pltpu.roll(src[a][bh], k * SSM_GROUP, axis=1)

    dst = []
    for b in range(len(src)):
        bh, bl = divmod(b, per)
        halves = []
        for ah in range(len(src) // per):
            out = moved(ah * per, bh, -bl % per)
            for j in range(1, per):
                out = jnp.where(slot == j, moved(ah * per + j, bh, (j - bl) % per), out)
            halves.append(out)
        dst.append(halves)
    return dst


def _inproj_kernel(*refs, lat):
    h_ref, mod_ref, g_ref, w_ref, cos_ref, sin_ref = refs[:6]
    if lat:
        qe_ref, qo_ref, k2_ref, v4_ref, cb_ref, z_ref, u_ref, u2_ref = refs[6:14]
    else:
        q_ref, k_ref, v_ref, k2_ref, v4_ref, cb_ref, z_ref, u_ref, u2_ref = refs[6:15]
    u_scr, a_scr, p_scr = refs[-3:]
    step_id = pl.program_id(0)

    @pl.when(step_id == 0)
    def _():
        a_scr[...] = jnp.zeros_like(a_scr)
        p_scr[...] = jnp.zeros_like(p_scr)

    lane = lax.broadcasted_iota(jnp.int32, (1, LANES), 1)
    first_of_pair = (lane & ROPE_PAIRS) == 0
    low_head = lane < HEAD_DIM

    def step(new, old):
        p_scr[old] = jnp.dot(a_scr[old], w_ref[...], preferred_element_type=F32)

        mod = mod_ref[0]
        a = _rms(h_ref[0], g_ref[0:1, :]) * (1.0 + mod[1:2, :]) + mod[0:1, :]
        a_scr[new] = a.astype(BF16)

        for s in range(ROPE_WIDTH // LANES):
            sl = slice(s * LANES, (s + 1) * LANES)
            t = p_scr[new, :, sl]
            partner = jnp.where(first_of_pair,
                                pltpu.roll(t, LANES - ROPE_PAIRS, axis=1),
                                pltpu.roll(t, ROPE_PAIRS, axis=1))
            r = t * cos_ref[:, sl] + partner * sin_ref[:, sl]
            if s < ATTN_WIDTH // LANES:
                if lat:
                    qe_ref[0, :, sl] = jnp.where(low_head, r, 0.0).astype(BF16)
                    qo_ref[0, :, sl] = jnp.where(low_head, 0.0, r).astype(BF16)
                else:
                    q_ref[0, :, sl] = r.astype(BF16)
            else:
                k2_ref[0, :, 0:LANES] = r.astype(BF16)
                k2_ref[0, :, LANES:2 * LANES] = pltpu.roll(r, HEAD_DIM, axis=1).astype(BF16)
                if not lat:
                    k_ref[0] = r.astype(BF16)
        o = ROPE_WIDTH
        v = p_scr[new, :, o:o + KV_WIDTH]
        v_swapped = pltpu.roll(v, HEAD_DIM, axis=1)
        for idx, part in enumerate((jnp.where(low_head, v, 0.0),
                                    jnp.where(low_head, 0.0, v_swapped),
                                    jnp.where(low_head, v_swapped, 0.0),
                                    jnp.where(low_head, 0.0, v))):
            v4_ref[0, :, idx * LANES:(idx + 1) * LANES] = part.astype(BF16)
        if not lat:
            v_ref[0] = v.astype(BF16)
        o += KV_WIDTH
        cb_ref[0] = p_scr[new, :, o:o + CONV_WIDTH]
        o += CONV_WIDTH
        z_ref[0] = (p_scr[new, :, o:o + CONV_WIDTH]
                    * p_scr[new, :, o + CONV_WIDTH:o + 2 * CONV_WIDTH])
        o += 2 * CONV_WIDTH
        u_ref[0] = p_scr[new, :, o:o + SSM_WIDTH]
        n_chunks = u2_ref.shape[0]
        halves = SSM_WIDTH // LANES
        for hf in range(halves):
            u_scr[hf] = p_scr[new, :, o + hf * LANES:o + (hf + 1) * LANES]
        steps = [[u_scr[hf, pl.ds(s, n_chunks, stride=S5_T), :]
                  for hf in range(halves)] for s in range(S5_T)]
        groups = _piece_transpose(steps)
        for gi in range(SSM_GROUPS):
            for hf in range(halves):
                c0 = gi * SSM_GROUP * S5_T + hf * LANES
                u2_ref[:, c0:c0 + LANES] = groups[gi][hf].astype(BF16)

    for k in range(2):
        pl.when(step_id % 2 == k)(functools.partial(step, k, 1 - k))


def _in_proj(h, mods, mod_row, norm_g, w_in, cos, sin, tm, lat, layer):
    b, n, _ = h.shape
    nt = n // tm
    total = nt * b
    if lat:
        widths = [(ATTN_WIDTH, BF16), (ATTN_WIDTH, BF16)]
    else:
        widths = [(ATTN_WIDTH, BF16), (KV_WIDTH, BF16), (KV_WIDTH, BF16)]
    widths += [(2 * KV_WIDTH, BF16), (4 * KV_WIDTH, BF16),
               (CONV_WIDTH, F32), (CONV_WIDTH, F32), (SSM_WIDTH, F32)]

    def tile(step, lag):
        m = jnp.clip(step - lag, 0, total - 1)
        return m // b, m % b

    def row(w, lag):
        def index(step):
            j, i = tile(step, lag)
            return i, j, 0
        return pl.BlockSpec((1, tm, w), index)

    def chunk_index(step):
        j, i = tile(step, 2)
        return i * nt + j, 0

    table = pl.BlockSpec((tm, ROPE_WIDTH), lambda step: (tile(step, 2)[0], 0))
    return pl.pallas_call(
        functools.partial(_inproj_kernel, lat=lat),
        grid=(total + 2,),
        in_specs=[row(D_MODEL, 0),
                  pl.BlockSpec((1, N_MOD, D_MODEL), lambda step: (mod_row(tile(step, 0)[1]), 0, 0)),
                  _layer_spec(norm_g, layer),
                  _layer_spec(w_in, layer),
                  table, table],
        out_specs=[row(w, 2) for w, _ in widths]
        + [pl.BlockSpec((tm // S5_T, S5_T * SSM_WIDTH), chunk_index)],
        out_shape=[jax.ShapeDtypeStruct((b, n, w), dt) for w, dt in widths]
        + [jax.ShapeDtypeStruct((b * n // S5_T, S5_T * SSM_WIDTH), BF16)],
        scratch_shapes=[pltpu.VMEM((SSM_WIDTH // LANES, tm, LANES), F32),
                        pltpu.VMEM((2, tm, D_MODEL), BF16),
                        pltpu.VMEM((2, tm, IN_WIDTH), F32)],
        compiler_params=_params(1),
        name="in_proj",
    )(h, mods, norm_g, w_in, cos, sin)


def _softmax_parts(s, sink):
    m = jnp.maximum(jnp.max(s, axis=-1, keepdims=True), sink)
    p = jnp.exp2(s - m)
    denom = jnp.sum(p, axis=-1, keepdims=True) + jnp.exp2(sink - m)
    return p, 1.0 / denom


def _attn_ctx_kernel(sink_ref, q_ref, k_ref, v_ref, o_ref, *, layer):
    q, k, v = q_ref[0], k_ref[0], v_ref[0]
    outs = []
    for h in range(N_Q_HEADS):
        hk = h // Q_PER_KV
        ksl = slice(hk * HEAD_DIM, (hk + 1) * HEAD_DIM)
        s = lax.dot_general(q[:, h * HEAD_DIM:(h + 1) * HEAD_DIM], k[:, ksl],
                            (((1,), (1,)), ((), ())), preferred_element_type=F32)
        p, inv = _softmax_parts(s, sink_ref[layer, h] * LOG2E)
        outs.append(jnp.dot(p.astype(BF16), v[:, ksl], preferred_element_type=F32) * inv)
    o_ref[0] = jnp.concatenate(outs, axis=1).astype(BF16)


def _attn_lat_kernel(sink_ref, qe_ref, qo_ref, k_ref, kx_ref, v_ref, vx_ref, bias_ref, o_ref,
                     s_scr, p_scr, inv_scr, *, tq, layer, n_blocks, total):
    n = pl.program_id(0)
    groups_per_seq = n_blocks // ATTN_GROUP

    def block(lag, sub):
        return (jnp.clip(n - lag, 0, total - 1) % groups_per_seq) * ATTN_GROUP + sub

    def window(lag, sub):
        return pl.multiple_of(jnp.clip(block(lag, sub) - 1, 0, n_blocks - 3) * tq, tq)

    @pl.when(n == 0)
    def _():
        s_scr[...] = jnp.zeros_like(s_scr)
        p_scr[...] = jnp.zeros_like(p_scr)
        inv_scr[...] = jnp.zeros_like(inv_scr)

    lane = lax.broadcasted_iota(jnp.int32, (1, LANES), 1)
    low_head = lane < HEAD_DIM
    slabs = ATTN_WIDTH // LANES

    def step(new, old):
        for sub in range(ATTN_GROUP):
            rows = slice(sub * tq, (sub + 1) * tq)
            vall = jnp.concatenate([v_ref[0, pl.ds(window(2, sub), 3 * tq), :], vx_ref[0]], axis=0)
            for t in range(slabs):
                c0 = (t // (slabs // N_KV_HEADS)) * 2 * LANES
                o = (jnp.dot(p_scr[new, sub, 2 * t], vall[:, c0:c0 + LANES],
                             preferred_element_type=F32)
                     + jnp.dot(p_scr[new, sub, 2 * t + 1], vall[:, c0 + LANES:c0 + 2 * LANES],
                               preferred_element_type=F32))
                o_ref[0, rows, t * LANES:(t + 1) * LANES] = (o * inv_scr[new, sub, t]).astype(BF16)

            kall = jnp.concatenate([k_ref[0, pl.ds(window(0, sub), 3 * tq), :], kx_ref[0]], axis=0)
            for hk in range(N_KV_HEADS):
                t0 = hk * (slabs // N_KV_HEADS)
                for par, q_ref in enumerate((qe_ref, qo_ref)):
                    lhs = jnp.concatenate([q_ref[0, rows, t0 * LANES:(t0 + 1) * LANES],
                                           q_ref[0, rows, (t0 + 1) * LANES:(t0 + 2) * LANES]],
                                          axis=0)
                    kv = (hk + par) % 2
                    s = lax.dot_general(lhs, kall[:, kv * LANES:(kv + 1) * LANES],
                                        (((1,), (1,)), ((), ())), preferred_element_type=F32)
                    s_scr[new, sub, 2 * t0 + par] = s[:tq]
                    s_scr[new, sub, 2 * t0 + 2 + par] = s[tq:]

            j = block(1, sub)
            bias = bias_ref[jnp.where(j == 0, 0, jnp.where(j == n_blocks - 1, 2, 1))]
            invs = []
            for h in range(N_Q_HEADS):
                p, inv = _softmax_parts(s_scr[old, sub, h] + bias, sink_ref[layer, h] * LOG2E)
                p_scr[old, sub, h] = p.astype(BF16)
                invs.append(inv)
            for t in range(slabs):
                inv_scr[old, sub, t] = jnp.where(low_head, invs[2 * t], invs[2 * t + 1])

    for k in range(2):
        pl.when(n % 2 == k)(functools.partial(step, k, 1 - k))


def _window_bias(tq, n_ctx):
    r = np.arange(tq)[:, None]
    c = np.arange(3 * tq + n_ctx)[None, :]
    variants = []
    for q_start in (0, tq, 2 * tq):
        ok = (c >= 3 * tq) | (np.abs(c - (q_start + r)) <= WINDOW)
        variants.append(np.where(ok, 0.0, NEG_INF))
    return jnp.asarray(np.stack(variants), F32)


def _attention_lat(sink, layer, qe, qo, k2, v4, k2x, v4x):
    b, n, _ = qe.shape
    n_ctx = k2x.shape[1]
    tq = Q_BLOCK
    nb = n // tq
    groups = nb // ATTN_GROUP
    total = b * groups
    n_keys = 3 * tq + n_ctx
    assert tq == WINDOW and nb >= 3 and nb % ATTN_GROUP == 0

    def group_of(step, lag):
        m = jnp.clip(step - lag, 0, total - 1)
        return m // groups, m % groups

    def rows(width, lag):
        return pl.BlockSpec((1, ATTN_GROUP * tq, width), lambda step: (*group_of(step, lag), 0))

    def whole(length, width, lag):
        return pl.BlockSpec((1, length, width), lambda step: (group_of(step, lag)[0], 0, 0))

    bias = _window_bias(tq, n_ctx)
    kw, vw = 2 * KV_WIDTH, 4 * KV_WIDTH
    return pl.pallas_call(
        functools.partial(_attn_lat_kernel, tq=tq, layer=layer, n_blocks=nb, total=total),
        grid=(total + 2,),
        in_specs=[pl.BlockSpec(memory_space=pltpu.SMEM),
                  rows(ATTN_WIDTH, 0), rows(ATTN_WIDTH, 0),
                  whole(n, kw, 0), whole(n_ctx, kw, 0),
                  whole(n, vw, 2), whole(n_ctx, vw, 2),
                  pl.BlockSpec(bias.shape, lambda step: (0, 0, 0), pipeline_mode=pl.Buffered(1))],
        out_specs=rows(ATTN_WIDTH, 2),
        out_shape=jax.ShapeDtypeStruct((b, n, ATTN_WIDTH), BF16),
        scratch_shapes=[pltpu.VMEM((2, ATTN_GROUP, N_Q_HEADS, tq, n_keys), F32),
                        pltpu.VMEM((2, ATTN_GROUP, N_Q_HEADS, tq, n_keys), BF16),
                        pltpu.VMEM((2, ATTN_GROUP, ATTN_WIDTH // LANES, tq, LANES), F32)],
        compiler_params=_params(1),
        name="attn_lat",
    )(sink, qe, qo, k2, k2x, v4, v4x, bias)


def _attention_ctx(sink, layer, q, k, v):
    b, n, _ = q.shape
    spec = lambda w: pl.BlockSpec((1, n, w), lambda i: (i, 0, 0))
    return pl.pallas_call(
        functools.partial(_attn_ctx_kernel, layer=layer),
        grid=(b,),
        in_specs=[pl.BlockSpec(memory_space=pltpu.SMEM), spec(ATTN_WIDTH), spec(KV_WIDTH),
                  spec(KV_WIDTH)],
        out_specs=spec(ATTN_WIDTH),
        out_shape=jax.ShapeDtypeStruct((b, n, ATTN_WIDTH), BF16),
        compiler_params=_params(1),
        name="attn_ctx",
    )(sink, q, k, v)


def _s5_exit_kernel(u_ref, we_ref, e_ref):
    half = e_ref.shape[1] // 2
    for gp in range(S5_PAIRS):
        r = jnp.dot(u_ref[:, gp * PAIR_IN:(gp + 1) * PAIR_IN], we_ref[gp],
                    preferred_element_type=F32)
        e_ref[:, gp * PAIR_STATE:(gp + 1) * PAIR_STATE] = r[:, :PAIR_STATE]
        e_ref[:, half + gp * PAIR_STATE:half + (gp + 1) * PAIR_STATE] = r[:, PAIR_STATE:]


def _s5_carry_kernel(e_ref, init_ref, lr_ref, li_ref, p_ref, fin_ref, state):
    @pl.when(pl.program_id(1) == 0)
    def _():
        state[...] = init_ref[0]

    def run(chunks):
        for gp in range(S5_PAIRS):
            re = slice(gp * PAIR_STATE, gp * PAIR_STATE + LANES)
            im = slice(gp * PAIR_STATE + LANES, (gp + 1) * PAIR_STATE)
            lr = lr_ref[0, :, gp * LANES:(gp + 1) * LANES]
            li = li_ref[0, :, gp * LANES:(gp + 1) * LANES]
            sr, si = state[:, re], state[:, im]
            for k in chunks:
                p_ref[:, k, re] = sr
                p_ref[:, k, im] = si
                sr, si = (lr * sr - li * si + e_ref[:, k, re],
                          lr * si + li * sr + e_ref[:, k, im])
            state[:, re] = sr
            state[:, im] = si

    n_chunks = e_ref.shape[1]
    direction = pl.program_id(0)
    pl.when(direction == 0)(lambda: run(tuple(range(n_chunks))))
    pl.when(direction == 1)(lambda: run(tuple(reversed(range(n_chunks)))))
    fin_ref[0] = state[...]


def _s5_out_kernel(u_ref, p_ref, toep_ref, csf_ref, csb_ref, y_ref):
    half = p_ref.shape[1] // 2
    grp = PAIR_IN // 2
    for gp in range(S5_PAIRS):
        pf = p_ref[:, gp * PAIR_STATE:(gp + 1) * PAIR_STATE].astype(BF16)
        pb = p_ref[:, half + gp * PAIR_STATE:half + (gp + 1) * PAIR_STATE].astype(BF16)
        st = (jnp.dot(pf, csf_ref[gp], preferred_element_type=F32)
              + jnp.dot(pb, csb_ref[gp], preferred_element_type=F32))
        for a in range(2):
            cols = slice(gp * PAIR_IN + a * grp, gp * PAIR_IN + (a + 1) * grp)
            y_ref[:, cols] = (jnp.dot(u_ref[:, cols], toep_ref[2 * gp + a],
                                      preferred_element_type=F32)
                              + st[:, a * grp:(a + 1) * grp])


def _s5_segment(u2, bsz, weights, init, layer):
    toep, we, csf, csb, lr, li = weights
    n_rows, width = u2.shape
    nc = n_rows // bsz
    tr = min(S5_ROW_TILE, n_rows)
    n_state = 2 * S5_PAIRS * PAIR_STATE
    row_blk = lambda w: pl.BlockSpec((tr, w), lambda j: (j, 0))
    exits = pl.pallas_call(
        _s5_exit_kernel,
        grid=(n_rows // tr,),
        in_specs=[row_blk(width), _layer_spec(we, layer)],
        out_specs=row_blk(n_state),
        out_shape=jax.ShapeDtypeStruct((n_rows, n_state), F32),
        compiler_params=_params(1),
        name="s5_exit",
    )(u2, we)

    nb = nc // S5_BLOCK_CHUNKS
    dir_blk = pl.BlockSpec((bsz, S5_BLOCK_CHUNKS, n_state // 2),
                           lambda d, k: (0, jnp.where(d == 0, k, nb - 1 - k), d))
    state_spec = pl.BlockSpec((1, bsz, n_state // 2), lambda d, k: (d, 0, 0))
    lam_spec = pl.BlockSpec((None, 1, 1, n_state // 4), lambda d, k: (layer, d, 0, 0))
    entering, leaving = pl.pallas_call(
        _s5_carry_kernel,
        grid=(2, nb),
        in_specs=[dir_blk, state_spec, lam_spec, lam_spec],
        out_specs=[dir_blk, state_spec],
        out_shape=[jax.ShapeDtypeStruct((bsz, nc, n_state), F32),
                   jax.ShapeDtypeStruct((2, bsz, n_state // 2), F32)],
        scratch_shapes=[pltpu.VMEM((bsz, n_state // 2), F32)],
        compiler_params=_params(2),
        name="s5_carry",
    )(exits.reshape(bsz, nc, n_state), init, lr, li)

    y2 = pl.pallas_call(
        _s5_out_kernel,
        grid=(n_rows // tr,),
        in_specs=[row_blk(width), row_blk(n_state), _layer_spec(toep, layer),
                  _layer_spec(csf, layer), _layer_spec(csb, layer)],
        out_specs=row_blk(width),
        out_shape=jax.ShapeDtypeStruct((n_rows, width), F32),
        compiler_params=_params(1),
        name="s5_out",
    )(u2, entering.reshape(n_rows, n_state), toep, csf, csb)
    return y2, leaving


def _pair_diag(m):
    g, r, c = m.shape
    eye = jnp.eye(2, dtype=m.dtype)
    return jnp.einsum('qarc,ab->qarbc', m.reshape(g // 2, 2, r, c), eye).reshape(g // 2, 2 * r, 2 * c)


def _s5_weights(lam_re, lam_im, log_dt, b_re, b_im, c_re, c_im):
    t = S5_T
    lam = lax.complex(lam_re.astype(F32), lam_im.astype(F32))
    lam_dt = lam * jnp.exp(log_dt.astype(F32))[..., None]
    lam_bar = jnp.exp(lam_dt)
    bb = ((lam_bar - 1) / lam)[..., None] * lax.complex(b_re.astype(F32), b_im.astype(F32))
    cm = lax.complex(c_re.astype(F32), c_im.astype(F32))
    steps = jnp.arange(t + 1, dtype=F32)
    pw = jnp.exp(lam_dt[None] * steps[:, None, None, None])

    kern = jnp.real(jnp.einsum('dgop,kdgp,dgpi->dkgoi', cm, pw[:t], bb))
    s_idx = jnp.arange(t)[:, None]
    t_idx = jnp.arange(t)[None, :]
    lag = t_idx - s_idx
    kf = jnp.where((lag >= 0)[..., None, None, None], kern[0][jnp.maximum(lag, 0)], 0.0)
    kb = jnp.where((lag <= 0)[..., None, None, None], kern[1][jnp.maximum(-lag, 0)], 0.0)
    toep = jnp.transpose(kf + kb, (2, 0, 4, 1, 3))
    toep = toep.reshape(SSM_GROUPS, t * SSM_GROUP, t * SSM_GROUP)

    ef = pw[t - 1 - jnp.arange(t), 0][..., None] * bb[0][None]
    eb = pw[jnp.arange(t), 1][..., None] * bb[1][None]
    to_cols = lambda m: jnp.transpose(m, (1, 0, 3, 2)).reshape(SSM_GROUPS, t * SSM_GROUP, SSM_STATE)
    we = jnp.concatenate([_pair_diag(to_cols(part(m))) for m in (ef, eb)
                          for part in (jnp.real, jnp.imag)], axis=-1)

    cf = cm[0][None] * pw[1 + jnp.arange(t), 0][:, :, None, :]
    cb = cm[1][None] * pw[t - jnp.arange(t), 1][:, :, None, :]
    to_rows = lambda m: jnp.transpose(m, (1, 3, 0, 2)).reshape(SSM_GROUPS, SSM_STATE, t * SSM_GROUP)
    cs = [jnp.concatenate([_pair_diag(to_rows(jnp.real(m))), _pair_diag(to_rows(-jnp.imag(m)))],
                          axis=1) for m in (cf, cb)]

    lam_t = pw[t].reshape(2, 1, N_STATE)
    return (toep.astype(BF16), we.astype(BF16), cs[0].astype(BF16), cs[1].astype(BF16),
            jnp.real(lam_t), jnp.imag(lam_t))


def _block_kernel(h_ref, attn_ref, cb_ref, z_ref, zp_ref, zn_ref, y2_ref, u_ref, mod_ref, g_ref,
                  cw_ref, dsk_ref, wglu_ref, bglu_ref, wout_ref, w1_ref, w2_ref, o_ref, y_scr,
                  mix_scr, *, tm, n_batch, n_tiles):
    step_id = pl.program_id(0)

    @pl.when(step_id == 0)
    def _():
        mix_scr[...] = jnp.zeros_like(mix_scr)

    j = jnp.minimum(step_id, n_tiles * n_batch - 1) // n_batch

    def step(new, old):
        mod = mod_ref[0]
        m = jnp.dot(mix_scr[old], wout_ref[...], preferred_element_type=F32)
        h1 = h_ref[0] + mod[2:3, :] * _rms(m, g_ref[1:2, :])
        a2 = (_rms(h1, g_ref[2:3, :]) * (1.0 + mod[4:5, :]) + mod[3:4, :]).astype(BF16)
        f = jnp.zeros((tm, D_MODEL), F32)
        for cidx in range(D_FF // FF_CHUNK):
            cs = slice(cidx * FF_CHUNK, (cidx + 1) * FF_CHUNK)
            t = jnp.maximum(jnp.dot(a2, w1_ref[:, cs], preferred_element_type=F32), 0.0)
            f = f + jnp.dot((t * t).astype(BF16), w2_ref[cs, :], preferred_element_type=F32)
        o_ref[0] = h1 + mod[5:6, :] * _rms(f, g_ref[3:4, :])

        z = z_ref[0]
        has_prev = jnp.where(j > 0, 1.0, 0.0)
        has_next = jnp.where(j < n_tiles - 1, 1.0, 0.0)
        z_before = zp_ref[0][SUBLANES - 1:SUBLANES, :] * has_prev
        z_after = zn_ref[0][0:1, :] * has_next
        row = lax.broadcasted_iota(jnp.int32, (tm, 1), 0)
        z_dn = jnp.where(row == 0, z_before, pltpu.roll(z, 1, axis=0))
        z_up = jnp.where(row == tm - 1, z_after, pltpu.roll(z, tm - 1, axis=0))
        conv = cb_ref[0] * (z_dn * cw_ref[0:1, :] + z * cw_ref[1:2, :] + z_up * cw_ref[2:3, :])
        n_chunks = y2_ref.shape[0]
        halves = S5_T * SSM_GROUP // LANES
        groups = [[y2_ref[:, (gi * halves + hf) * LANES:(gi * halves + hf + 1) * LANES]
                   for hf in range(halves)] for gi in range(SSM_GROUPS)]
        steps = _piece_transpose(groups)
        for s in range(S5_T):
            for hf in range(SSM_WIDTH // LANES):
                y_scr[hf, pl.ds(s, n_chunks, stride=S5_T), :] = steps[s][hf]
        y = jnp.concatenate([y_scr[hf] for hf in range(SSM_WIDTH // LANES)], axis=1)
        y = y + dsk_ref[...] * u_ref[0]
        gl = 0.5 * y * (1.0 + jnp.tanh(math.sqrt(2.0 / math.pi) * (y + 0.044715 * (y * y * y))))
        gate = jnp.dot(gl.astype(BF16), wglu_ref[...], preferred_element_type=F32) + bglu_ref[...]
        ssm = gl * _sigmoid(gate)
        a0, a1 = ATTN_WIDTH, ATTN_WIDTH + CONV_WIDTH
        mix_scr[new, :, 0:a0] = attn_ref[0]
        mix_scr[new, :, a0:a1] = conv.astype(BF16)
        mix_scr[new, :, a1:] = ssm.astype(BF16)

    for k in range(2):
        pl.when(step_id % 2 == k)(functools.partial(step, k, 1 - k))


def _block(h, attn, cb, z, y2, u, mods, mod_row, params, tm, layer):
    b, n, _ = h.shape
    nt = n // tm
    total = nt * b
    per = tm // SUBLANES
    n8 = n // SUBLANES

    def tile(step, lag):
        m = jnp.clip(step - lag, 0, total - 1)
        return m // b, m % b

    def row(w, lag):
        def index(step):
            j, i = tile(step, lag)
            return i, j, 0
        return pl.BlockSpec((1, tm, w), index)

    def halo(offset):
        def index(step):
            j, i = tile(step, 0)
            return i, jnp.clip(j * per + offset, 0, n8 - 1), 0
        return pl.BlockSpec((1, SUBLANES, CONV_WIDTH), index)

    def chunk_index(step):
        j, i = tile(step, 0)
        return i * nt + j, 0

    return pl.pallas_call(
        functools.partial(_block_kernel, tm=tm, n_batch=b, n_tiles=nt),
        grid=(total + 1,),
        in_specs=[row(D_MODEL, 1), row(ATTN_WIDTH, 0), row(CONV_WIDTH, 0), row(CONV_WIDTH, 0),
                  halo(-1), halo(per),
                  pl.BlockSpec((tm // S5_T, S5_T * SSM_WIDTH), chunk_index), row(SSM_WIDTH, 0),
                  pl.BlockSpec((1, N_MOD, D_MODEL), lambda step: (mod_row(tile(step, 1)[1]), 0, 0))]
        + [_layer_spec(p, layer) for p in params],
        out_specs=row(D_MODEL, 1),
        out_shape=jax.ShapeDtypeStruct((b, n, D_MODEL), F32),
        scratch_shapes=[pltpu.VMEM((SSM_WIDTH // LANES, tm, LANES), F32),
                        pltpu.VMEM((2, tm, MIX_WIDTH), BF16)],
        compiler_params=_params(1),
        name="mix_mlp_block",
    )(h, attn, cb, z, z, z, y2, u, mods, *params)


def _rope_tables(n_lat, n_ctx):
    f32 = np.float32
    rows = n_lat // GRID_W
    row = np.repeat(np.arange(rows), GRID_W).astype(f32)[:, None]
    col = np.tile(np.arange(GRID_W), rows).astype(f32)[:, None]
    freqs = (f32(ROPE_BASE) ** (-np.arange(ROPE_PAIRS, dtype=f32) / f32(ROPE_PAIRS))).astype(f32)
    ang_r, ang_c = row * freqs, col * freqs
    cos_h = np.concatenate([np.cos(ang_r)] * 2 + [np.cos(ang_c)] * 2, axis=-1)
    sin_h = np.concatenate([-np.sin(ang_r), np.sin(ang_r), -np.sin(ang_c), np.sin(ang_c)], axis=-1)
    scale = np.concatenate([np.full((ATTN_WIDTH,), LOG2E * HEAD_DIM ** -0.5, f32),
                            np.ones((KV_WIDTH,), f32)])
    n_heads = N_Q_HEADS + N_KV_HEADS
    cos = (np.tile(cos_h, (1, n_heads)) * scale).astype(f32)
    sin = (np.tile(sin_h, (1, n_heads)) * scale).astype(f32)
    cos_c = np.broadcast_to(scale, (n_ctx, ROPE_WIDTH)).astype(f32)
    sin_c = np.zeros((n_ctx, ROPE_WIDTH), f32)
    return tuple(jnp.asarray(t) for t in (cos, sin, cos_c, sin_c))


def kernel(x, c, ctx, c_ctx, w_ada, b_ada, norm_g, w_in, conv_w, attn_sink, ssm_lam_re, ssm_lam_im,
           ssm_log_dt, ssm_b_re, ssm_b_im, ssm_c_re, ssm_c_im, ssm_d, w_glu, b_glu, w_out,
           w_mlp_in, w_mlp_out):
    bsz, n_lat, _ = x.shape
    n_ctx = ctx.shape[1]
    tm_lat = min(ROW_TILE, n_lat)
    tm_ctx = min(ROW_TILE, n_ctx)
    cos, sin, cos_c, sin_c = _rope_tables(n_lat, n_ctx)

    pad = (-(bsz + 1)) % SUBLANES
    cvec = jnp.concatenate([c, c_ctx[None, :], jnp.zeros((pad, D_MODEL), F32)], axis=0)
    rows = cvec.shape[0]
    mods = _ada_mods(cvec, w_ada, b_ada).reshape(DEPTH * rows, N_MOD, D_MODEL)

    s5w = jax.vmap(_s5_weights)(ssm_lam_re, ssm_lam_im, ssm_log_dt, ssm_b_re, ssm_b_im,
                                ssm_c_re, ssm_c_im)
    w_in_b = w_in.astype(BF16)
    block_params = (norm_g, conv_w, ssm_d.reshape(DEPTH, 1, SSM_WIDTH), w_glu.astype(BF16),
                    b_glu.reshape(DEPTH, 1, SSM_WIDTH), w_out.astype(BF16),
                    w_mlp_in.astype(BF16), w_mlp_out.astype(BF16))
    zero_state = jnp.zeros((2, bsz, 2 * N_STATE), F32)

    h, hc = x, ctx
    for l in range(DEPTH):
        with_ctx_out = l < DEPTH - 1
        lat_row = lambda i, base=l * rows: base + i
        ctx_row = lambda i, base=l * rows + bsz: base
        qe, qo, k2, v4, cb, z, u, u2 = _in_proj(h, mods, lat_row, norm_g, w_in_b, cos, sin,
                                                tm_lat, True, l)
        qc, kc, vc, k2c, v4c, cbc, zc, uc, u2c = _in_proj(hc, mods, ctx_row, norm_g, w_in_b,
                                                          cos_c, sin_c, tm_ctx, False, l)

        attn = _attention_lat(attn_sink, l, qe, qo, k2, v4, k2c, v4c)

        y2c, ctx_state = _s5_segment(u2c, bsz, s5w, zero_state, l)
        y2, _ = _s5_segment(u2, bsz, s5w, ctx_state, l)

        h = _block(h, attn, cb, z, y2, u, mods, lat_row, block_params, tm_lat, l)
        if with_ctx_out:
            attn_c = _attention_ctx(attn_sink, l, qc, kc, vc)
            hc = _block(hc, attn_c, cbc, zc, y2c, uc, mods, ctx_row, block_params, tm_ctx, l)
    return h
```

```python
import functools
import math

import numpy as np
import jax
import jax.numpy as jnp
from jax import lax
from jax.experimental import pallas as pl
from jax.experimental.pallas import tpu as pltpu

F32 = jnp.float32
BF16 = jnp.bfloat16

D_MODEL = 1024
DEPTH = 4
GRID_W = 64
HEAD_DIM = 64
N_Q_HEADS = 8
N_KV_HEADS = 2
Q_PER_KV = N_Q_HEADS // N_KV_HEADS
ATTN_WIDTH = N_Q_HEADS * HEAD_DIM
KV_WIDTH = N_KV_HEADS * HEAD_DIM
WINDOW = 128
Q_BLOCK = 128
ROPE_BASE = 10000.0
ROPE_PAIRS = HEAD_DIM // 4
CONV_WIDTH = 256
SSM_WIDTH = 256
SSM_GROUP = 16
SSM_GROUPS = SSM_WIDTH // SSM_GROUP
SSM_STATE = 64
N_STATE = SSM_GROUPS * SSM_STATE
MIX_WIDTH = ATTN_WIDTH + CONV_WIDTH + SSM_WIDTH
IN_WIDTH = ATTN_WIDTH + 2 * KV_WIDTH + 3 * CONV_WIDTH + SSM_WIDTH
ROPE_WIDTH = ATTN_WIDTH + KV_WIDTH
D_FF = 4 * D_MODEL
N_MOD = 6
EPS = 1e-6
NEG_INF = -1e30
LOG2E = math.log2(math.e)

LANES = 128
SUBLANES = 8
VMEM_LIMIT = 56 * 1024 * 1024

ROW_TILE = 512
ATTN_GROUP = 4
HALO_ROWS = 16
S5_T = 16
S5_BLOCK_CHUNKS = 8
S5_ROW_TILE = 256
S5_PAIRS = SSM_GROUPS // 2
PAIR_IN = 2 * SSM_GROUP * S5_T
PAIR_STATE = 4 * SSM_STATE
FF_CHUNK = 1024


def _layer_spec(arr, layer):
    nd = arr.ndim
    return pl.BlockSpec((None,) + arr.shape[1:], lambda *_: (layer,) + (0,) * (nd - 1),
                        pipeline_mode=pl.Buffered(1))


def _params(n_grid):
    return pltpu.CompilerParams(dimension_semantics=("arbitrary",) * n_grid,
                                vmem_limit_bytes=VMEM_LIMIT)


def _rms(x, g):
    ms = jnp.mean(x * x, axis=-1, keepdims=True)
    return x * lax.rsqrt(ms + EPS) * g


def _sigmoid(x):
    return 1.0 / (1.0 + jnp.exp(-x))


def _ada_kernel(c_ref, w_ref, b_ref, o_ref):
    c = c_ref[...]
    act = (c * _sigmoid(c)).astype(BF16)
    o_ref[0] = jnp.dot(act, w_ref[0].astype(BF16), preferred_element_type=F32) + b_ref[0]


def _ada_mods(cvec, w_ada, b_ada):
    rows = cvec.shape[0]
    tn = 1024
    n_out = N_MOD * D_MODEL
    return pl.pallas_call(
        _ada_kernel,
        grid=(DEPTH, n_out // tn),
        in_specs=[pl.BlockSpec((rows, D_MODEL), lambda l, j: (0, 0)),
                  pl.BlockSpec((1, D_MODEL, tn), lambda l, j: (l, 0, j)),
                  pl.BlockSpec((1, 1, tn), lambda l, j: (l, 0, j))],
        out_specs=pl.BlockSpec((1, rows, tn), lambda l, j: (l, 0, j)),
        out_shape=jax.ShapeDtypeStruct((DEPTH, rows, n_out), F32),
        compiler_params=_params(2),
        name="ada_mods",
    )(cvec, w_ada, b_ada.reshape(DEPTH, 1, n_out))


def _piece_transpose(src):
    per = LANES // SSM_GROUP
    n = src[0][0].shape[0]
    slot = lax.broadcasted_iota(jnp.int32, (n, LANES), 1) // SSM_GROUP

    def moved(a, bh, k):
        return src[a][bh] if k == 0 else pltpu.roll(src[a][bh], k * SSM_GROUP, axis=1)

    dst = []
    for b in range(len(src)):
        bh, bl = divmod(b, per)
        halves = []
        for ah in range(len(src) // per):
            out = moved(ah * per, bh, -bl % per)
            for j in range(1, per):
                out = jnp.where(slot == j, moved(ah * per + j, bh, (j - bl) % per), out)
            halves.append(out)
        dst.append(halves)
    return dst


def _inproj_kernel(*refs, lat):
    h_ref, mod_ref, g_ref, w_ref, cos_ref, sin_ref = refs[:6]
    if lat:
        qe_ref, qo_ref, k2_ref, v4_ref, cb_ref, z_ref, u2_ref = refs[6:13]
    else:
        q_ref, k_ref, v_ref, k2_ref, v4_ref, cb_ref, z_ref, u2_ref = refs[6:14]
    u_scr, a_scr, p_scr = refs[-3:]
    step_id = pl.program_id(0)

    @pl.when(step_id == 0)
    def _():
        a_scr[...] = jnp.zeros_like(a_scr)
        p_scr[...] = jnp.zeros_like(p_scr)

    lane = lax.broadcasted_iota(jnp.int32, (1, LANES), 1)
    first_of_pair = (lane & ROPE_PAIRS) == 0
    low_head = lane < HEAD_DIM

    def step(new, old):
        p_scr[old] = jnp.dot(a_scr[old], w_ref[...], preferred_element_type=F32)

        mod = mod_ref[0]
        a = _rms(h_ref[0], g_ref[0:1, :]) * (1.0 + mod[1:2, :]) + mod[0:1, :]
        a_scr[new] = a.astype(BF16)

        for s in range(ROPE_WIDTH // LANES):
            sl = slice(s * LANES, (s + 1) * LANES)
            t = p_scr[new, :, sl]
            partner = jnp.where(first_of_pair,
                                pltpu.roll(t, LANES - ROPE_PAIRS, axis=1),
                                pltpu.roll(t, ROPE_PAIRS, axis=1))
            r = t * cos_ref[:, sl] + partner * sin_ref[:, sl]
            if s < ATTN_WIDTH // LANES:
                if lat:
                    qe_ref[0, :, sl] = jnp.where(low_head, r, 0.0).astype(BF16)
                    qo_ref[0, :, sl] = jnp.where(low_head, 0.0, r).astype(BF16)
                else:
                    q_ref[0, :, sl] = r.astype(BF16)
            else:
                k2_ref[0, :, 0:LANES] = r.astype(BF16)
                k2_ref[0, :, LANES:2 * LANES] = pltpu.roll(r, HEAD_DIM, axis=1).astype(BF16)
                if not lat:
                    k_ref[0] = r.astype(BF16)
        o = ROPE_WIDTH
        v = p_scr[new, :, o:o + KV_WIDTH]
        v_swapped = pltpu.roll(v, HEAD_DIM, axis=1)
        for idx, part in enumerate((jnp.where(low_head, v, 0.0),
                                    jnp.where(low_head, 0.0, v_swapped),
                                    jnp.where(low_head, v_swapped, 0.0),
                                    jnp.where(low_head, 0.0, v))):
            v4_ref[0, :, idx * LANES:(idx + 1) * LANES] = part.astype(BF16)
        if not lat:
            v_ref[0] = v.astype(BF16)
        o += KV_WIDTH
        cb_ref[0] = p_scr[new, :, o:o + CONV_WIDTH].astype(BF16)
        o += CONV_WIDTH
        z_ref[0] = (p_scr[new, :, o:o + CONV_WIDTH]
                    * p_scr[new, :, o + CONV_WIDTH:o + 2 * CONV_WIDTH]).astype(BF16)
        o += 2 * CONV_WIDTH
        n_chunks = u2_ref.shape[0]
        halves = SSM_WIDTH // LANES
        for hf in range(halves):
            u_scr[hf] = p_scr[new, :, o + hf * LANES:o + (hf + 1) * LANES]
        steps = [[u_scr[hf, pl.ds(s, n_chunks, stride=S5_T), :]
                  for hf in range(halves)] for s in range(S5_T)]
        groups = _piece_transpose(steps)
        for gi in range(SSM_GROUPS):
            for hf in range(halves):
                c0 = gi * SSM_GROUP * S5_T + hf * LANES
                u2_ref[:, c0:c0 + LANES] = groups[gi][hf].astype(BF16)

    for k in range(2):
        pl.when(step_id % 2 == k)(functools.partial(step, k, 1 - k))


def _in_proj(h, mods, mod_row, norm_g, w_in, cos, sin, tm, lat, layer):
    b, n, _ = h.shape
    nt = n // tm
    total = nt * b
    if lat:
        widths = [(ATTN_WIDTH, BF16), (ATTN_WIDTH, BF16)]
    else:
        widths = [(ATTN_WIDTH, BF16), (KV_WIDTH, BF16), (KV_WIDTH, BF16)]
    widths += [(2 * KV_WIDTH, BF16), (4 * KV_WIDTH, BF16), (CONV_WIDTH, BF16), (CONV_WIDTH, BF16)]

    def tile(step, lag):
        m = jnp.clip(step - lag, 0, total - 1)
        return m // b, m % b

    def row(w, lag):
        def index(step):
            j, i = tile(step, lag)
            return i, j, 0
        return pl.BlockSpec((1, tm, w), index)

    def chunk_index(step):
        j, i = tile(step, 2)
        return i * nt + j, 0

    table = pl.BlockSpec((tm, ROPE_WIDTH), lambda step: (tile(step, 2)[0], 0))
    return pl.pallas_call(
        functools.partial(_inproj_kernel, lat=lat),
        grid=(total + 2,),
        in_specs=[row(D_MODEL, 0),
                  pl.BlockSpec((1, N_MOD, D_MODEL), lambda step: (mod_row(tile(step, 0)[1]), 0, 0)),
                  _layer_spec(norm_g, layer),
                  _layer_spec(w_in, layer),
                  table, table],
        out_specs=[row(w, 2) for w, _ in widths]
        + [pl.BlockSpec((tm // S5_T, S5_T * SSM_WIDTH), chunk_index)],
        out_shape=[jax.ShapeDtypeStruct((b, n, w), dt) for w, dt in widths]
        + [jax.ShapeDtypeStruct((b * n // S5_T, S5_T * SSM_WIDTH), BF16)],
        scratch_shapes=[pltpu.VMEM((SSM_WIDTH // LANES, tm, LANES), F32),
                        pltpu.VMEM((2, tm, D_MODEL), BF16),
                        pltpu.VMEM((2, tm, IN_WIDTH), F32)],
        compiler_params=_params(1),
        name="in_proj",
    )(h, mods, norm_g, w_in, cos, sin)


def _softmax_parts(s, sink):
    m = jnp.maximum(jnp.max(s, axis=-1, keepdims=True), sink)
    p = jnp.exp2(s - m)
    denom = jnp.sum(p, axis=-1, keepdims=True) + jnp.exp2(sink - m)
    return p, 1.0 / denom


def _attn_ctx_kernel(sink_ref, q_ref, k_ref, v_ref, o_ref, *, layer):
    q, k, v = q_ref[0], k_ref[0], v_ref[0]
    outs = []
    for h in range(N_Q_HEADS):
        hk = h // Q_PER_KV
        ksl = slice(hk * HEAD_DIM, (hk + 1) * HEAD_DIM)
        s = lax.dot_general(q[:, h * HEAD_DIM:(h + 1) * HEAD_DIM], k[:, ksl],
                            (((1,), (1,)), ((), ())), preferred_element_type=F32)
        p, inv = _softmax_parts(s, sink_ref[layer, h] * LOG2E)
        outs.append(jnp.dot(p.astype(BF16), v[:, ksl], preferred_element_type=F32) * inv)
    o_ref[0] = jnp.concatenate(outs, axis=1).astype(BF16)


def _attn_lat_kernel(sink_ref, qe_ref, qo_ref, k_ref, kx_ref, v_ref, vx_ref, bias_ref, o_ref,
                     s_scr, p_scr, inv_scr, *, tq, layer, n_blocks, total):
    n = pl.program_id(0)
    groups_per_seq = n_blocks // ATTN_GROUP

    def block(lag, sub):
        return (jnp.clip(n - lag, 0, total - 1) % groups_per_seq) * ATTN_GROUP + sub

    def window(lag, sub):
        return pl.multiple_of(jnp.clip(block(lag, sub) - 1, 0, n_blocks - 3) * tq, tq)

    @pl.when(n == 0)
    def _():
        s_scr[...] = jnp.zeros_like(s_scr)
        p_scr[...] = jnp.zeros_like(p_scr)
        inv_scr[...] = jnp.zeros_like(inv_scr)

    lane = lax.broadcasted_iota(jnp.int32, (1, LANES), 1)
    low_head = lane < HEAD_DIM
    slabs = ATTN_WIDTH // LANES

    def step(new, old):
        for sub in range(ATTN_GROUP):
            rows = slice(sub * tq, (sub + 1) * tq)
            vall = jnp.concatenate([v_ref[0, pl.ds(window(2, sub), 3 * tq), :], vx_ref[0]], axis=0)
            for t in range(slabs):
                c0 = (t // (slabs // N_KV_HEADS)) * 2 * LANES
                o = (jnp.dot(p_scr[new, sub, 2 * t], vall[:, c0:c0 + LANES],
                             preferred_element_type=F32)
                     + jnp.dot(p_scr[new, sub, 2 * t + 1], vall[:, c0 + LANES:c0 + 2 * LANES],
                               preferred_element_type=F32))
                o_ref[0, rows, t * LANES:(t + 1) * LANES] = (o * inv_scr[new, sub, t]).astype(BF16)

            kall = jnp.concatenate([k_ref[0, pl.ds(window(0, sub), 3 * tq), :], kx_ref[0]], axis=0)
            for hk in range(N_KV_HEADS):
                t0 = hk * (slabs // N_KV_HEADS)
                for par, q_ref in enumerate((qe_ref, qo_ref)):
                    lhs = jnp.concatenate([q_ref[0, rows, t0 * LANES:(t0 + 1) * LANES],
                                           q_ref[0, rows, (t0 + 1) * LANES:(t0 + 2) * LANES]],
                                          axis=0)
                    kv = (hk + par) % 2
                    s = lax.dot_general(lhs, kall[:, kv * LANES:(kv + 1) * LANES],
                                        (((1,), (1,)), ((), ())), preferred_element_type=F32)
                    s_scr[new, sub, 2 * t0 + par] = s[:tq]
                    s_scr[new, sub, 2 * t0 + 2 + par] = s[tq:]

            j = block(1, sub)
            bias = bias_ref[jnp.where(j == 0, 0, jnp.where(j == n_blocks - 1, 2, 1))]
            invs = []
            for h in range(N_Q_HEADS):
                p, inv = _softmax_parts(s_scr[old, sub, h] + bias, sink_ref[layer, h] * LOG2E)
                p_scr[old, sub, h] = p.astype(BF16)
                invs.append(inv)
            for t in range(slabs):
                inv_scr[old, sub, t] = jnp.where(low_head, invs[2 * t], invs[2 * t + 1])

    for k in range(2):
        pl.when(n % 2 == k)(functools.partial(step, k, 1 - k))


def _window_bias(tq, n_ctx):
    r = np.arange(tq)[:, None]
    c = np.arange(3 * tq + n_ctx)[None, :]
    variants = []
    for q_start in (0, tq, 2 * tq):
        ok = (c >= 3 * tq) | (np.abs(c - (q_start + r)) <= WINDOW)
        variants.append(np.where(ok, 0.0, NEG_INF))
    return jnp.asarray(np.stack(variants), F32)


def _attention_lat(sink, layer, qe, qo, k2, v4, k2x, v4x):
    b, n, _ = qe.shape
    n_ctx = k2x.shape[1]
    tq = Q_BLOCK
    nb = n // tq
    groups = nb // ATTN_GROUP
    total = b * groups
    n_keys = 3 * tq + n_ctx
    assert tq == WINDOW and nb >= 3 and nb % ATTN_GROUP == 0

    def group_of(step, lag):
        m = jnp.clip(step - lag, 0, total - 1)
        return m // groups, m % groups

    def rows(width, lag):
        return pl.BlockSpec((1, ATTN_GROUP * tq, width), lambda step: (*group_of(step, lag), 0))

    def whole(length, width, lag):
        return pl.BlockSpec((1, length, width), lambda step: (group_of(step, lag)[0], 0, 0))

    bias = _window_bias(tq, n_ctx)
    kw, vw = 2 * KV_WIDTH, 4 * KV_WIDTH
    return pl.pallas_call(
        functools.partial(_attn_lat_kernel, tq=tq, layer=layer, n_blocks=nb, total=total),
        grid=(total + 2,),
        in_specs=[pl.BlockSpec(memory_space=pltpu.SMEM),
                  rows(ATTN_WIDTH, 0), rows(ATTN_WIDTH, 0),
                  whole(n, kw, 0), whole(n_ctx, kw, 0),
                  whole(n, vw, 2), whole(n_ctx, vw, 2),
                  pl.BlockSpec(bias.shape, lambda step: (0, 0, 0), pipeline_mode=pl.Buffered(1))],
        out_specs=rows(ATTN_WIDTH, 2),
        out_shape=jax.ShapeDtypeStruct((b, n, ATTN_WIDTH), BF16),
        scratch_shapes=[pltpu.VMEM((2, ATTN_GROUP, N_Q_HEADS, tq, n_keys), F32),
                        pltpu.VMEM((2, ATTN_GROUP, N_Q_HEADS, tq, n_keys), BF16),
                        pltpu.VMEM((2, ATTN_GROUP, ATTN_WIDTH // LANES, tq, LANES), F32)],
        compiler_params=_params(1),
        name="attn_lat",
    )(sink, qe, qo, k2, k2x, v4, v4x, bias)


def _attention_ctx(sink, layer, q, k, v):
    b, n, _ = q.shape
    spec = lambda w: pl.BlockSpec((1, n, w), lambda i: (i, 0, 0))
    return pl.pallas_call(
        functools.partial(_attn_ctx_kernel, layer=layer),
        grid=(b,),
        in_specs=[pl.BlockSpec(memory_space=pltpu.SMEM), spec(ATTN_WIDTH), spec(KV_WIDTH),
                  spec(KV_WIDTH)],
        out_specs=spec(ATTN_WIDTH),
        out_shape=jax.ShapeDtypeStruct((b, n, ATTN_WIDTH), BF16),
        compiler_params=_params(1),
        name="attn_ctx",
    )(sink, q, k, v)


def _s5_exit_kernel(u_ref, we_ref, e_ref):
    half = e_ref.shape[1] // 2
    for gp in range(S5_PAIRS):
        r = jnp.dot(u_ref[:, gp * PAIR_IN:(gp + 1) * PAIR_IN], we_ref[gp],
                    preferred_element_type=F32)
        e_ref[:, gp * PAIR_STATE:(gp + 1) * PAIR_STATE] = r[:, :PAIR_STATE]
        e_ref[:, half + gp * PAIR_STATE:half + (gp + 1) * PAIR_STATE] = r[:, PAIR_STATE:]


def _s5_carry_kernel(e_ref, init_ref, lr_ref, li_ref, p_ref, fin_ref, state):
    @pl.when(pl.program_id(1) == 0)
    def _():
        state[...] = init_ref[0]

    def run(chunks):
        for gp in range(S5_PAIRS):
            re = slice(gp * PAIR_STATE, gp * PAIR_STATE + LANES)
            im = slice(gp * PAIR_STATE + LANES, (gp + 1) * PAIR_STATE)
            lr = lr_ref[0, :, gp * LANES:(gp + 1) * LANES]
            li = li_ref[0, :, gp * LANES:(gp + 1) * LANES]
            sr, si = state[:, re], state[:, im]
            for k in chunks:
                p_ref[:, k, re] = sr
                p_ref[:, k, im] = si
                sr, si = (lr * sr - li * si + e_ref[:, k, re],
                          lr * si + li * sr + e_ref[:, k, im])
            state[:, re] = sr
            state[:, im] = si

    n_chunks = e_ref.shape[1]
    direction = pl.program_id(0)
    pl.when(direction == 0)(lambda: run(tuple(range(n_chunks))))
    pl.when(direction == 1)(lambda: run(tuple(reversed(range(n_chunks)))))
    fin_ref[0] = state[...]


def _s5_out_kernel(u_ref, p_ref, toep_ref, csf_ref, csb_ref, y_ref):
    half = p_ref.shape[1] // 2
    grp = PAIR_IN // 2
    for gp in range(S5_PAIRS):
        pf = p_ref[:, gp * PAIR_STATE:(gp + 1) * PAIR_STATE].astype(BF16)
        pb = p_ref[:, half + gp * PAIR_STATE:half + (gp + 1) * PAIR_STATE].astype(BF16)
        st = (jnp.dot(pf, csf_ref[gp], preferred_element_type=F32)
              + jnp.dot(pb, csb_ref[gp], preferred_element_type=F32))
        for a in range(2):
            cols = slice(gp * PAIR_IN + a * grp, gp * PAIR_IN + (a + 1) * grp)
            y_ref[:, cols] = (jnp.dot(u_ref[:, cols], toep_ref[2 * gp + a],
                                      preferred_element_type=F32)
                              + st[:, a * grp:(a + 1) * grp])


def _s5_segment(u2, bsz, weights, init, layer):
    toep, we, csf, csb, lr, li = weights
    n_rows, width = u2.shape
    nc = n_rows // bsz
    tr = min(S5_ROW_TILE, n_rows)
    n_state = 2 * S5_PAIRS * PAIR_STATE
    row_blk = lambda w: pl.BlockSpec((tr, w), lambda j: (j, 0))
    exits = pl.pallas_call(
        _s5_exit_kernel,
        grid=(n_rows // tr,),
        in_specs=[row_blk(width), _layer_spec(we, layer)],
        out_specs=row_blk(n_state),
        out_shape=jax.ShapeDtypeStruct((n_rows, n_state), F32),
        compiler_params=_params(1),
        name="s5_exit",
    )(u2, we)

    nb = nc // S5_BLOCK_CHUNKS
    dir_blk = pl.BlockSpec((bsz, S5_BLOCK_CHUNKS, n_state // 2),
                           lambda d, k: (0, jnp.where(d == 0, k, nb - 1 - k), d))
    state_spec = pl.BlockSpec((1, bsz, n_state // 2), lambda d, k: (d, 0, 0))
    lam_spec = pl.BlockSpec((None, 1, 1, n_state // 4), lambda d, k: (layer, d, 0, 0))
    entering, leaving = pl.pallas_call(
        _s5_carry_kernel,
        grid=(2, nb),
        in_specs=[dir_blk, state_spec, lam_spec, lam_spec],
        out_specs=[dir_blk, state_spec],
        out_shape=[jax.ShapeDtypeStruct((bsz, nc, n_state), F32),
                   jax.ShapeDtypeStruct((2, bsz, n_state // 2), F32)],
        scratch_shapes=[pltpu.VMEM((bsz, n_state // 2), F32)],
        compiler_params=_params(2),
        name="s5_carry",
    )(exits.reshape(bsz, nc, n_state), init, lr, li)

    y2 = pl.pallas_call(
        _s5_out_kernel,
        grid=(n_rows // tr,),
        in_specs=[row_blk(width), row_blk(n_state), _layer_spec(toep, layer),
                  _layer_spec(csf, layer), _layer_spec(csb, layer)],
        out_specs=row_blk(width),
        out_shape=jax.ShapeDtypeStruct((n_rows, width), F32),
        compiler_params=_params(1),
        name="s5_out",
    )(u2, entering.reshape(n_rows, n_state), toep, csf, csb)
    return y2, leaving


def _pair_diag(m):
    g, r, c = m.shape
    eye = jnp.eye(2, dtype=m.dtype)
    return jnp.einsum('qarc,ab->qarbc', m.reshape(g // 2, 2, r, c), eye).reshape(g // 2, 2 * r, 2 * c)


def _s5_weights(lam_re, lam_im, log_dt, b_re, b_im, c_re, c_im, d_skip):
    t = S5_T
    lam = lax.complex(lam_re.astype(F32), lam_im.astype(F32))
    lam_dt = lam * jnp.exp(log_dt.astype(F32))[..., None]
    lam_bar = jnp.exp(lam_dt)
    bb = ((lam_bar - 1) / lam)[..., None] * lax.complex(b_re.astype(F32), b_im.astype(F32))
    cm = lax.complex(c_re.astype(F32), c_im.astype(F32))
    steps = jnp.arange(t + 1, dtype=F32)
    pw = jnp.exp(lam_dt[None] * steps[:, None, None, None])

    kern = jnp.real(jnp.einsum('dgop,kdgp,dgpi->dkgoi', cm, pw[:t], bb))
    s_idx = jnp.arange(t)[:, None]
    t_idx = jnp.arange(t)[None, :]
    lag = t_idx - s_idx
    kf = jnp.where((lag >= 0)[..., None, None, None], kern[0][jnp.maximum(lag, 0)], 0.0)
    kb = jnp.where((lag <= 0)[..., None, None, None], kern[1][jnp.maximum(-lag, 0)], 0.0)
    skip = (jnp.eye(t, dtype=F32)[:, :, None, None, None]
            * (jnp.eye(SSM_GROUP, dtype=F32) * d_skip.reshape(SSM_GROUPS, 1, SSM_GROUP).astype(F32)))
    toep = jnp.transpose(kf + kb + skip, (2, 0, 4, 1, 3))
    toep = toep.reshape(SSM_GROUPS, t * SSM_GROUP, t * SSM_GROUP)

    ef = pw[t - 1 - jnp.arange(t), 0][..., None] * bb[0][None]
    eb = pw[jnp.arange(t), 1][..., None] * bb[1][None]
    to_cols = lambda m: jnp.transpose(m, (1, 0, 3, 2)).reshape(SSM_GROUPS, t * SSM_GROUP, SSM_STATE)
    we = jnp.concatenate([_pair_diag(to_cols(part(m))) for m in (ef, eb)
                          for part in (jnp.real, jnp.imag)], axis=-1)

    cf = cm[0][None] * pw[1 + jnp.arange(t), 0][:, :, None, :]
    cb = cm[1][None] * pw[t - jnp.arange(t), 1][:, :, None, :]
    to_rows = lambda m: jnp.transpose(m, (1, 3, 0, 2)).reshape(SSM_GROUPS, SSM_STATE, t * SSM_GROUP)
    cs = [jnp.concatenate([_pair_diag(to_rows(jnp.real(m))), _pair_diag(to_rows(-jnp.imag(m)))],
                          axis=1) for m in (cf, cb)]

    lam_t = pw[t].reshape(2, 1, N_STATE)
    return (toep.astype(BF16), we.astype(BF16), cs[0].astype(BF16), cs[1].astype(BF16),
            jnp.real(lam_t), jnp.imag(lam_t))


def _block_kernel(h_ref, attn_ref, cb_ref, z_ref, zp_ref, zn_ref, y2_ref, mod_ref, g_ref,
                  cw_ref, wglu_ref, bglu_ref, wout_ref, w1_ref, w2_ref, o_ref, y_scr,
                  mix_scr, *, tm, n_batch, n_tiles):
    step_id = pl.program_id(0)

    @pl.when(step_id == 0)
    def _():
        mix_scr[...] = jnp.zeros_like(mix_scr)

    j = jnp.minimum(step_id, n_tiles * n_batch - 1) // n_batch

    def step(new, old):
        mod = mod_ref[0]
        m = jnp.dot(mix_scr[old], wout_ref[...], preferred_element_type=F32)
        h1 = h_ref[0] + mod[2:3, :] * _rms(m, g_ref[1:2, :])
        a2 = (_rms(h1, g_ref[2:3, :]) * (1.0 + mod[4:5, :]) + mod[3:4, :]).astype(BF16)
        f = jnp.zeros((tm, D_MODEL), F32)
        for cidx in range(D_FF // FF_CHUNK):
            cs = slice(cidx * FF_CHUNK, (cidx + 1) * FF_CHUNK)
            t = jnp.maximum(jnp.dot(a2, w1_ref[:, cs], preferred_element_type=F32), 0.0)
            f = f + jnp.dot((t * t).astype(BF16), w2_ref[cs, :], preferred_element_type=F32)
        o_ref[0] = h1 + mod[5:6, :] * _rms(f, g_ref[3:4, :])

        z = z_ref[0].astype(F32)
        has_prev = jnp.where(j > 0, 1.0, 0.0)
        has_next = jnp.where(j < n_tiles - 1, 1.0, 0.0)
        z_before = zp_ref[0][HALO_ROWS - 1:HALO_ROWS, :].astype(F32) * has_prev
        z_after = zn_ref[0][0:1, :].astype(F32) * has_next
        row = lax.broadcasted_iota(jnp.int32, (tm, 1), 0)
        z_dn = jnp.where(row == 0, z_before, pltpu.roll(z, 1, axis=0))
        z_up = jnp.where(row == tm - 1, z_after, pltpu.roll(z, tm - 1, axis=0))
        conv = cb_ref[0].astype(F32) * (z_dn * cw_ref[0:1, :] + z * cw_ref[1:2, :]
                                        + z_up * cw_ref[2:3, :])
        n_chunks = y2_ref.shape[0]
        halves = S5_T * SSM_GROUP // LANES
        groups = [[y2_ref[:, (gi * halves + hf) * LANES:(gi * halves + hf + 1) * LANES]
                   for hf in range(halves)] for gi in range(SSM_GROUPS)]
        steps = _piece_transpose(groups)
        for s in range(S5_T):
            for hf in range(SSM_WIDTH // LANES):
                y_scr[hf, pl.ds(s, n_chunks, stride=S5_T), :] = steps[s][hf]
        y = jnp.concatenate([y_scr[hf] for hf in range(SSM_WIDTH // LANES)], axis=1)
        gl = 0.5 * y * (1.0 + jnp.tanh(math.sqrt(2.0 / math.pi) * (y + 0.044715 * (y * y * y))))
        gate = jnp.dot(gl.astype(BF16), wglu_ref[...], preferred_element_type=F32) + bglu_ref[...]
        ssm = gl * _sigmoid(gate)
        a0, a1 = ATTN_WIDTH, ATTN_WIDTH + CONV_WIDTH
        mix_scr[new, :, 0:a0] = attn_ref[0]
        mix_scr[new, :, a0:a1] = conv.astype(BF16)
        mix_scr[new, :, a1:] = ssm.astype(BF16)

    for k in range(2):
        pl.when(step_id % 2 == k)(functools.partial(step, k, 1 - k))


def _block(h, attn, cb, z, y2, mods, mod_row, params, tm, layer):
    b, n, _ = h.shape
    nt = n // tm
    total = nt * b
    per = tm // HALO_ROWS
    n_halo = n // HALO_ROWS

    def tile(step, lag):
        m = jnp.clip(step - lag, 0, total - 1)
        return m // b, m % b

    def row(w, lag):
        def index(step):
            j, i = tile(step, lag)
            return i, j, 0
        return pl.BlockSpec((1, tm, w), index)

    def halo(offset):
        def index(step):
            j, i = tile(step, 0)
            return i, jnp.clip(j * per + offset, 0, n_halo - 1), 0
        return pl.BlockSpec((1, HALO_ROWS, CONV_WIDTH), index)

    def chunk_index(step):
        j, i = tile(step, 0)
        return i * nt + j, 0

    return pl.pallas_call(
        functools.partial(_block_kernel, tm=tm, n_batch=b, n_tiles=nt),
        grid=(total + 1,),
        in_specs=[row(D_MODEL, 1), row(ATTN_WIDTH, 0), row(CONV_WIDTH, 0), row(CONV_WIDTH, 0),
                  halo(-1), halo(per),
                  pl.BlockSpec((tm // S5_T, S5_T * SSM_WIDTH), chunk_index),
                  pl.BlockSpec((1, N_MOD, D_MODEL), lambda step: (mod_row(tile(step, 1)[1]), 0, 0))]
        + [_layer_spec(p, layer) for p in params],
        out_specs=row(D_MODEL, 1),
        out_shape=jax.ShapeDtypeStruct((b, n, D_MODEL), F32),
        scratch_shapes=[pltpu.VMEM((SSM_WIDTH // LANES, tm, LANES), F32),
                        pltpu.VMEM((2, tm, MIX_WIDTH), BF16)],
        compiler_params=_params(1),
        name="mix_mlp_block",
    )(h, attn, cb, z, z, z, y2, mods, *params)


def _rope_tables(n_lat, n_ctx):
    f32 = np.float32
    rows = n_lat // GRID_W
    row = np.repeat(np.arange(rows), GRID_W).astype(f32)[:, None]
    col = np.tile(np.arange(GRID_W), rows).astype(f32)[:, None]
    freqs = (f32(ROPE_BASE) ** (-np.arange(ROPE_PAIRS, dtype=f32) / f32(ROPE_PAIRS))).astype(f32)
    ang_r, ang_c = row * freqs, col * freqs
    cos_h = np.concatenate([np.cos(ang_r)] * 2 + [np.cos(ang_c)] * 2, axis=-1)
    sin_h = np.concatenate([-np.sin(ang_r), np.sin(ang_r), -np.sin(ang_c), np.sin(ang_c)], axis=-1)
    scale = np.concatenate([np.full((ATTN_WIDTH,), LOG2E * HEAD_DIM ** -0.5, f32),
                            np.ones((KV_WIDTH,), f32)])
    n_heads = N_Q_HEADS + N_KV_HEADS
    cos = (np.tile(cos_h, (1, n_heads)) * scale).astype(f32)
    sin = (np.tile(sin_h, (1, n_heads)) * scale).astype(f32)
    cos_c = np.broadcast_to(scale, (n_ctx, ROPE_WIDTH)).astype(f32)
    sin_c = np.zeros((n_ctx, ROPE_WIDTH), f32)
    return tuple(jnp.asarray(t) for t in (cos, sin, cos_c, sin_c))


def kernel(x, c, ctx, c_ctx, w_ada, b_ada, norm_g, w_in, conv_w, attn_sink, ssm_lam_re, ssm_lam_im,
           ssm_log_dt, ssm_b_re, ssm_b_im, ssm_c_re, ssm_c_im, ssm_d, w_glu, b_glu, w_out,
           w_mlp_in, w_mlp_out):
    bsz, n_lat, _ = x.shape
    n_ctx = ctx.shape[1]
    tm_lat = min(ROW_TILE, n_lat)
    tm_ctx = min(ROW_TILE, n_ctx)
    cos, sin, cos_c, sin_c = _rope_tables(n_lat, n_ctx)

    pad = (-(bsz + 1)) % SUBLANES
    cvec = jnp.concatenate([c, c_ctx[None, :], jnp.zeros((pad, D_MODEL), F32)], axis=0)
    rows = cvec.shape[0]
    mods = _ada_mods(cvec, w_ada, b_ada).reshape(DEPTH * rows, N_MOD, D_MODEL)

    s5w = jax.vmap(_s5_weights)(ssm_lam_re, ssm_lam_im, ssm_log_dt, ssm_b_re, ssm_b_im,
                                ssm_c_re, ssm_c_im, ssm_d)
    w_in_b = w_in.astype(BF16)
    block_params = (norm_g, conv_w, w_glu.astype(BF16),
                    b_glu.reshape(DEPTH, 1, SSM_WIDTH), w_out.astype(BF16),
                    w_mlp_in.astype(BF16), w_mlp_out.astype(BF16))
    zero_state = jnp.zeros((2, bsz, 2 * N_STATE), F32)

    h, hc = x, ctx
    for l in range(DEPTH):
        with_ctx_out = l < DEPTH - 1
        lat_row = lambda i, base=l * rows: base + i
        ctx_row = lambda i, base=l * rows + bsz: base
        qe, qo, k2, v4, cb, z, u2 = _in_proj(h, mods, lat_row, norm_g, w_in_b, cos, sin,
                                             tm_lat, True, l)
        qc, kc, vc, k2c, v4c, cbc, zc, u2c = _in_proj(hc, mods, ctx_row, norm_g, w_in_b,
                                                      cos_c, sin_c, tm_ctx, False, l)

        attn = _attention_lat(attn_sink, l, qe, qo, k2, v4, k2c, v4c)

        y2c, ctx_state = _s5_segment(u2c, bsz, s5w, zero_state, l)
        y2, _ = _s5_segment(u2, bsz, s5w, ctx_state, l)

        h = _block(h, attn, cb, z, y2, mods, lat_row, block_params, tm_lat, l)
        if with_ctx_out:
            attn_c = _attention_ctx(attn_sink, l, qc, kc, vc)
            hc = _block(hc, attn_c, cbc, zc, y2c, mods, ctx_row, block_params, tm_ctx, l)
    return h
```

```python
import functools
import math

import numpy as np
import jax
import jax.numpy as jnp
from jax import lax
from jax.experimental import pallas as pl
from jax.experimental.pallas import tpu as pltpu

F32 = jnp.float32
BF16 = jnp.bfloat16

D_MODEL = 1024
DEPTH = 4
GRID_W = 64
HEAD_DIM = 64
N_Q_HEADS = 8
N_KV_HEADS = 2
Q_PER_KV = N_Q_HEADS // N_KV_HEADS
ATTN_WIDTH = N_Q_HEADS * HEAD_DIM
KV_WIDTH = N_KV_HEADS * HEAD_DIM
WINDOW = 128
Q_BLOCK = 128
ROPE_BASE = 10000.0
ROPE_PAIRS = HEAD_DIM // 4
CONV_WIDTH = 256
SSM_WIDTH = 256
SSM_GROUP = 16
SSM_GROUPS = SSM_WIDTH // SSM_GROUP
SSM_STATE = 64
N_STATE = SSM_GROUPS * SSM_STATE
MIX_WIDTH = ATTN_WIDTH + CONV_WIDTH + SSM_WIDTH
IN_WIDTH = ATTN_WIDTH + 2 * KV_WIDTH + 3 * CONV_WIDTH + SSM_WIDTH
ROPE_WIDTH = ATTN_WIDTH + KV_WIDTH
D_FF = 4 * D_MODEL
N_MOD = 6
EPS = 1e-6
NEG_INF = -1e30
LOG2E = math.log2(math.e)

LANES = 128
SUBLANES = 8
VMEM_LIMIT = 56 * 1024 * 1024

ROW_TILE = 512
ATTN_GROUP = 2
S5_T = 16
S5_BLOCK_CHUNKS = 8
S5_ROW_TILE = 256
S5_PAIRS = SSM_GROUPS // 2
PAIR_IN = 2 * SSM_GROUP * S5_T
PAIR_STATE = 4 * SSM_STATE
FF_CHUNK = 1024


def _layer_spec(arr, layer):
    nd = arr.ndim
    return pl.BlockSpec((None,) + arr.shape[1:], lambda *_: (layer,) + (0,) * (nd - 1),
                        pipeline_mode=pl.Buffered(1))


def _params(n_grid):
    return pltpu.CompilerParams(dimension_semantics=("arbitrary",) * n_grid,
                                vmem_limit_bytes=VMEM_LIMIT)


def _rms(x, g):
    ms = jnp.mean(x * x, axis=-1, keepdims=True)
    return x * lax.rsqrt(ms + EPS) * g


def _sigmoid(x):
    return 1.0 / (1.0 + jnp.exp(-x))


def _ada_kernel(c_ref, w_ref, b_ref, o_ref):
    c = c_ref[...]
    act = (c * _sigmoid(c)).astype(BF16)
    o_ref[0] = jnp.dot(act, w_ref[0].astype(BF16), preferred_element_type=F32) + b_ref[0]


def _ada_mods(cvec, w_ada, b_ada):
    rows = cvec.shape[0]
    tn = 1024
    n_out = N_MOD * D_MODEL
    return pl.pallas_call(
        _ada_kernel,
        grid=(DEPTH, n_out // tn),
        in_specs=[pl.BlockSpec((rows, D_MODEL), lambda l, j: (0, 0)),
                  pl.BlockSpec((1, D_MODEL, tn), lambda l, j: (l, 0, j)),
                  pl.BlockSpec((1, 1, tn), lambda l, j: (l, 0, j))],
        out_specs=pl.BlockSpec((1, rows, tn), lambda l, j: (l, 0, j)),
        out_shape=jax.ShapeDtypeStruct((DEPTH, rows, n_out), F32),
        compiler_params=_params(2),
        name="ada_mods",
    )(cvec, w_ada, b_ada.reshape(DEPTH, 1, n_out))


def _piece_transpose(src):
    per = LANES // SSM_GROUP
    size = len(src)
    assert size == 2 * per and len(src[0]) == 2
    n = src[0][0].shape[0]
    slot = lax.broadcasted_iota(jnp.int32, (n, LANES), 1) // SSM_GROUP
    rows = [list(r) for r in src]
    for r in range(per):
        rows[r][1], rows[r + per][0] = rows[r + per][0], rows[r][1]
    d = per // 2
    while d >= 1:
        keep = (slot & d) == 0
        for r in range(size):
            if r & d:
                continue
            for hf in range(2):
                lo, hi = rows[r][hf], rows[r + d][hf]
                rows[r][hf] = jnp.where(keep, lo, pltpu.roll(hi, d * SSM_GROUP, axis=1))
                rows[r + d][hf] = jnp.where(keep, pltpu.roll(lo, LANES - d * SSM_GROUP, axis=1), hi)
        d //= 2
    return rows


def _inproj_kernel(*refs, lat):
    h_ref, mod_ref, g_ref, w_ref, cos_ref, sin_ref = refs[:6]
    if lat:
        qe_ref, qo_ref, k2_ref, v4_ref, cb_ref, z_ref, u_ref, u2_ref = refs[6:14]
    else:
        q_ref, k_ref, v_ref, k2_ref, v4_ref, cb_ref, z_ref, u_ref, u2_ref = refs[6:15]
    u_scr, a_scr, p_scr = refs[-3:]
    step_id = pl.program_id(0)

    @pl.when(step_id == 0)
    def _():
        a_scr[...] = jnp.zeros_like(a_scr)
        p_scr[...] = jnp.zeros_like(p_scr)

    lane = lax.broadcasted_iota(jnp.int32, (1, LANES), 1)
    first_of_pair = (lane & ROPE_PAIRS) == 0
    low_head = lane < HEAD_DIM

    def step(new, old):
        p_scr[old] = jnp.dot(a_scr[old], w_ref[...], preferred_element_type=F32)

        mod = mod_ref[0]
        a = _rms(h_ref[0], g_ref[0:1, :]) * (1.0 + mod[1:2, :]) + mod[0:1, :]
        a_scr[new] = a.astype(BF16)

        for s in range(ROPE_WIDTH // LANES):
            sl = slice(s * LANES, (s + 1) * LANES)
            t = p_scr[new, :, sl]
            partner = jnp.where(first_of_pair,
                                pltpu.roll(t, LANES - ROPE_PAIRS, axis=1),
                                pltpu.roll(t, ROPE_PAIRS, axis=1))
            r = t * cos_ref[:, sl] + partner * sin_ref[:, sl]
            if s < ATTN_WIDTH // LANES:
                if lat:
                    qe_ref[0, :, sl] = jnp.where(low_head, r, 0.0).astype(BF16)
                    qo_ref[0, :, sl] = jnp.where(low_head, 0.0, r).astype(BF16)
                else:
                    q_ref[0, :, sl] = r.astype(BF16)
            else:
                k2_ref[0, :, 0:LANES] = r.astype(BF16)
                k2_ref[0, :, LANES:2 * LANES] = pltpu.roll(r, HEAD_DIM, axis=1).astype(BF16)
                if not lat:
                    k_ref[0] = r.astype(BF16)
        o = ROPE_WIDTH
        v = p_scr[new, :, o:o + KV_WIDTH]
        v_swapped = pltpu.roll(v, HEAD_DIM, axis=1)
        for idx, part in enumerate((jnp.where(low_head, v, 0.0),
                                    jnp.where(low_head, 0.0, v_swapped),
                                    jnp.where(low_head, v_swapped, 0.0),
                                    jnp.where(low_head, 0.0, v))):
            v4_ref[0, :, idx * LANES:(idx + 1) * LANES] = part.astype(BF16)
        if not lat:
            v_ref[0] = v.astype(BF16)
        o += KV_WIDTH
        cb_ref[0] = p_scr[new, :, o:o + CONV_WIDTH]
        o += CONV_WIDTH
        z_ref[0] = (p_scr[new, :, o:o + CONV_WIDTH]
                    * p_scr[new, :, o + CONV_WIDTH:o + 2 * CONV_WIDTH])
        o += 2 * CONV_WIDTH
        u_ref[0] = p_scr[new, :, o:o + SSM_WIDTH]
        n_chunks = u2_ref.shape[0]
        halves = SSM_WIDTH // LANES
        for hf in range(halves):
            u_scr[hf] = p_scr[new, :, o + hf * LANES:o + (hf + 1) * LANES]
        steps = [[u_scr[hf, pl.ds(s, n_chunks, stride=S5_T), :]
                  for hf in range(halves)] for s in range(S5_T)]
        groups = _piece_transpose(steps)
        for gi in range(SSM_GROUPS):
            for hf in range(halves):
                c0 = gi * SSM_GROUP * S5_T + hf * LANES
                u2_ref[:, c0:c0 + LANES] = groups[gi][hf].astype(BF16)

    for k in range(2):
        pl.when(step_id % 2 == k)(functools.partial(step, k, 1 - k))


def _in_proj(h, mods, mod_row, norm_g, w_in, cos, sin, tm, lat, layer):
    b, n, _ = h.shape
    nt = n // tm
    total = nt * b
    if lat:
        widths = [(ATTN_WIDTH, BF16), (ATTN_WIDTH, BF16)]
    else:
        widths = [(ATTN_WIDTH, BF16), (KV_WIDTH, BF16), (KV_WIDTH, BF16)]
    widths += [(2 * KV_WIDTH, BF16), (4 * KV_WIDTH, BF16),
               (CONV_WIDTH, F32), (CONV_WIDTH, F32), (SSM_WIDTH, F32)]

    def tile(step, lag):
        m = jnp.clip(step - lag, 0, total - 1)
        return m // b, m % b

    def row(w, lag):
        def index(step):
            j, i = tile(step, lag)
            return i, j, 0
        return pl.BlockSpec((1, tm, w), index)

    def chunk_index(step):
        j, i = tile(step, 2)
        return i * nt + j, 0

    table = pl.BlockSpec((tm, ROPE_WIDTH), lambda step: (tile(step, 2)[0], 0))
    return pl.pallas_call(
        functools.partial(_inproj_kernel, lat=lat),
        grid=(total + 2,),
        in_specs=[row(D_MODEL, 0),
                  pl.BlockSpec((1, N_MOD, D_MODEL), lambda step: (mod_row(tile(step, 0)[1]), 0, 0)),
                  _layer_spec(norm_g, layer),
                  _layer_spec(w_in, layer),
                  table, table],
        out_specs=[row(w, 2) for w, _ in widths]
        + [pl.BlockSpec((tm // S5_T, S5_T * SSM_WIDTH), chunk_index)],
        out_shape=[jax.ShapeDtypeStruct((b, n, w), dt) for w, dt in widths]
        + [jax.ShapeDtypeStruct((b * n // S5_T, S5_T * SSM_WIDTH), BF16)],
        scratch_shapes=[pltpu.VMEM((SSM_WIDTH // LANES, tm, LANES), F32),
                        pltpu.VMEM((2, tm, D_MODEL), BF16),
                        pltpu.VMEM((2, tm, IN_WIDTH), F32)],
        compiler_params=_params(1),
        name="in_proj",
    )(h, mods, norm_g, w_in, cos, sin)


def _softmax_parts(s, sink):
    m = jnp.maximum(jnp.max(s, axis=-1, keepdims=True), sink)
    p = jnp.exp2(s - m)
    denom = jnp.sum(p, axis=-1, keepdims=True) + jnp.exp2(sink - m)
    return p, 1.0 / denom


def _attn_ctx_kernel(sink_ref, q_ref, k_ref, v_ref, o_ref, *, layer):
    q, k, v = q_ref[0], k_ref[0], v_ref[0]
    outs = []
    for h in range(N_Q_HEADS):
        hk = h // Q_PER_KV
        ksl = slice(hk * HEAD_DIM, (hk + 1) * HEAD_DIM)
        s = lax.dot_general(q[:, h * HEAD_DIM:(h + 1) * HEAD_DIM], k[:, ksl],
                            (((1,), (1,)), ((), ())), preferred_element_type=F32)
        p, inv = _softmax_parts(s, sink_ref[layer, h] * LOG2E)
        outs.append(jnp.dot(p.astype(BF16), v[:, ksl], preferred_element_type=F32) * inv)
    o_ref[0] = jnp.concatenate(outs, axis=1).astype(BF16)


def _attn_lat_kernel(sink_ref, qe_ref, qo_ref, k_ref, kx_ref, v_ref, vx_ref, bias_ref, o_ref,
                     s_scr, p_scr, inv_scr, *, tq, layer, n_blocks, total):
    n = pl.program_id(0)
    groups_per_seq = n_blocks // ATTN_GROUP

    def block(lag, sub):
        return (jnp.clip(n - lag, 0, total - 1) % groups_per_seq) * ATTN_GROUP + sub

    def window(lag, sub):
        return pl.multiple_of(jnp.clip(block(lag, sub) - 1, 0, n_blocks - 3) * tq, tq)

    @pl.when(n == 0)
    def _():
        s_scr[...] = jnp.zeros_like(s_scr)
        p_scr[...] = jnp.zeros_like(p_scr)
        inv_scr[...] = jnp.zeros_like(inv_scr)

    lane = lax.broadcasted_iota(jnp.int32, (1, LANES), 1)
    low_head = lane < HEAD_DIM
    slabs = ATTN_WIDTH // LANES

    def step(new, old):
        for sub in range(ATTN_GROUP):
            rows = slice(sub * tq, (sub + 1) * tq)
            vall = jnp.concatenate([v_ref[0, pl.ds(window(2, sub), 3 * tq), :], vx_ref[0]], axis=0)
            for t in range(slabs):
                c0 = (t // (slabs // N_KV_HEADS)) * 2 * LANES
                o = (jnp.dot(p_scr[new, sub, 2 * t], vall[:, c0:c0 + LANES],
                             preferred_element_type=F32)
                     + jnp.dot(p_scr[new, sub, 2 * t + 1], vall[:, c0 + LANES:c0 + 2 * LANES],
                               preferred_element_type=F32))
                o_ref[0, rows, t * LANES:(t + 1) * LANES] = (o * inv_scr[new, sub, t]).astype(BF16)

            kall = jnp.concatenate([k_ref[0, pl.ds(window(0, sub), 3 * tq), :], kx_ref[0]], axis=0)
            for hk in range(N_KV_HEADS):
                t0 = hk * (slabs // N_KV_HEADS)
                for par, q_ref in enumerate((qe_ref, qo_ref)):
                    lhs = jnp.concatenate([q_ref[0, rows, t0 * LANES:(t0 + 1) * LANES],
                                           q_ref[0, rows, (t0 + 1) * LANES:(t0 + 2) * LANES]],
                                          axis=0)
                    kv = (hk + par) % 2
                    s = lax.dot_general(lhs, kall[:, kv * LANES:(kv + 1) * LANES],
                                        (((1,), (1,)), ((), ())), preferred_element_type=F32)
                    s_scr[new, sub, 2 * t0 + par] = s[:tq]
                    s_scr[new, sub, 2 * t0 + 2 + par] = s[tq:]

            j = block(1, sub)
            variant = jnp.where(j == 0, 0, jnp.where(j == n_blocks - 1, 2, 1))
            bias = bias_ref[variant, :, 0:3 * tq]
            invs = []
            for h in range(N_Q_HEADS):
                s = jnp.concatenate([s_scr[old, sub, h, :, 0:3 * tq] + bias,
                                     s_scr[old, sub, h, :, 3 * tq:]], axis=1)
                p, inv = _softmax_parts(s, sink_ref[layer, h] * LOG2E)
                p_scr[old, sub, h] = p.astype(BF16)
                invs.append(inv)
            for t in range(slabs):
                inv_scr[old, sub, t] = jnp.where(low_head, invs[2 * t], invs[2 * t + 1])

    for k in range(2):
        pl.when(n % 2 == k)(functools.partial(step, k, 1 - k))


def _window_bias(tq, n_ctx):
    r = np.arange(tq)[:, None]
    c = np.arange(3 * tq + n_ctx)[None, :]
    variants = []
    for q_start in (0, tq, 2 * tq):
        ok = (c >= 3 * tq) | (np.abs(c - (q_start + r)) <= WINDOW)
        variants.append(np.where(ok, 0.0, NEG_INF))
    return jnp.asarray(np.stack(variants), F32)


def _attention_lat(sink, layer, qe, qo, k2, v4, k2x, v4x):
    b, n, _ = qe.shape
    n_ctx = k2x.shape[1]
    tq = Q_BLOCK
    nb = n // tq
    groups = nb // ATTN_GROUP
    total = b * groups
    n_keys = 3 * tq + n_ctx
    assert tq == WINDOW and nb >= 3 and nb % ATTN_GROUP == 0

    def group_of(step, lag):
        m = jnp.clip(step - lag, 0, total - 1)
        return m // groups, m % groups

    def rows(width, lag):
        return pl.BlockSpec((1, ATTN_GROUP * tq, width), lambda step: (*group_of(step, lag), 0))

    def whole(length, width, lag):
        return pl.BlockSpec((1, length, width), lambda step: (group_of(step, lag)[0], 0, 0))

    bias = _window_bias(tq, n_ctx)
    kw, vw = 2 * KV_WIDTH, 4 * KV_WIDTH
    return pl.pallas_call(
        functools.partial(_attn_lat_kernel, tq=tq, layer=layer, n_blocks=nb, total=total),
        grid=(total + 2,),
        in_specs=[pl.BlockSpec(memory_space=pltpu.SMEM),
                  rows(ATTN_WIDTH, 0), rows(ATTN_WIDTH, 0),
                  whole(n, kw, 0), whole(n_ctx, kw, 0),
                  whole(n, vw, 2), whole(n_ctx, vw, 2),
                  pl.BlockSpec(bias.shape, lambda step: (0, 0, 0), pipeline_mode=pl.Buffered(1))],
        out_specs=rows(ATTN_WIDTH, 2),
        out_shape=jax.ShapeDtypeStruct((b, n, ATTN_WIDTH), BF16),
        scratch_shapes=[pltpu.VMEM((2, ATTN_GROUP, N_Q_HEADS, tq, n_keys), F32),
                        pltpu.VMEM((2, ATTN_GROUP, N_Q_HEADS, tq, n_keys), BF16),
                        pltpu.VMEM((2, ATTN_GROUP, ATTN_WIDTH // LANES, tq, LANES), F32)],
        compiler_params=_params(1),
        name="attn_lat",
    )(sink, qe, qo, k2, k2x, v4, v4x, bias)


def _attention_ctx(sink, layer, q, k, v):
    b, n, _ = q.shape
    spec = lambda w: pl.BlockSpec((1, n, w), lambda i: (i, 0, 0))
    return pl.pallas_call(
        functools.partial(_attn_ctx_kernel, layer=layer),
        grid=(b,),
        in_specs=[pl.BlockSpec(memory_space=pltpu.SMEM), spec(ATTN_WIDTH), spec(KV_WIDTH),
                  spec(KV_WIDTH)],
        out_specs=spec(ATTN_WIDTH),
        out_shape=jax.ShapeDtypeStruct((b, n, ATTN_WIDTH), BF16),
        compiler_params=_params(1),
        name="attn_ctx",
    )(sink, q, k, v)


def _s5_exit_kernel(u_ref, we_ref, e_ref):
    half = e_ref.shape[1] // 2
    for gp in range(S5_PAIRS):
        r = jnp.dot(u_ref[:, gp * PAIR_IN:(gp + 1) * PAIR_IN], we_ref[gp],
                    preferred_element_type=F32)
        e_ref[:, gp * PAIR_STATE:(gp + 1) * PAIR_STATE] = r[:, :PAIR_STATE]
        e_ref[:, half + gp * PAIR_STATE:half + (gp + 1) * PAIR_STATE] = r[:, PAIR_STATE:]


def _s5_carry_kernel(e_ref, init_ref, lr_ref, li_ref, p_ref, fin_ref, state):
    @pl.when(pl.program_id(1) == 0)
    def _():
        state[...] = init_ref[0]

    def run(chunks):
        for gp in range(S5_PAIRS):
            re = slice(gp * PAIR_STATE, gp * PAIR_STATE + LANES)
            im = slice(gp * PAIR_STATE + LANES, (gp + 1) * PAIR_STATE)
            lr = lr_ref[0, :, gp * LANES:(gp + 1) * LANES]
            li = li_ref[0, :, gp * LANES:(gp + 1) * LANES]
            sr, si = state[:, re], state[:, im]
            for k in chunks:
                p_ref[:, k, re] = sr
                p_ref[:, k, im] = si
                sr, si = (lr * sr - li * si + e_ref[:, k, re],
                          lr * si + li * sr + e_ref[:, k, im])
            state[:, re] = sr
            state[:, im] = si

    n_chunks = e_ref.shape[1]
    direction = pl.program_id(0)
    pl.when(direction == 0)(lambda: run(tuple(range(n_chunks))))
    pl.when(direction == 1)(lambda: run(tuple(reversed(range(n_chunks)))))
    fin_ref[0] = state[...]


def _s5_out_kernel(u_ref, p_ref, toep_ref, csf_ref, csb_ref, y_ref):
    half = p_ref.shape[1] // 2
    grp = PAIR_IN // 2
    for gp in range(S5_PAIRS):
        pf = p_ref[:, gp * PAIR_STATE:(gp + 1) * PAIR_STATE].astype(BF16)
        pb = p_ref[:, half + gp * PAIR_STATE:half + (gp + 1) * PAIR_STATE].astype(BF16)
        st = (jnp.dot(pf, csf_ref[gp], preferred_element_type=F32)
              + jnp.dot(pb, csb_ref[gp], preferred_element_type=F32))
        for a in range(2):
            cols = slice(gp * PAIR_IN + a * grp, gp * PAIR_IN + (a + 1) * grp)
            y_ref[:, cols] = (jnp.dot(u_ref[:, cols], toep_ref[2 * gp + a],
                                      preferred_element_type=F32)
                              + st[:, a * grp:(a + 1) * grp])


def _s5_segment(u2, bsz, weights, init, layer):
    toep, we, csf, csb, lr, li = weights
    n_rows, width = u2.shape
    nc = n_rows // bsz
    tr = min(S5_ROW_TILE, n_rows)
    n_state = 2 * S5_PAIRS * PAIR_STATE
    row_blk = lambda w: pl.BlockSpec((tr, w), lambda j: (j, 0))
    exits = pl.pallas_call(
        _s5_exit_kernel,
        grid=(n_rows // tr,),
        in_specs=[row_blk(width), _layer_spec(we, layer)],
        out_specs=row_blk(n_state),
        out_shape=jax.ShapeDtypeStruct((n_rows, n_state), F32),
        compiler_params=_params(1),
        name="s5_exit",
    )(u2, we)

    nb = nc // S5_BLOCK_CHUNKS
    dir_blk = pl.BlockSpec((bsz, S5_BLOCK_CHUNKS, n_state // 2),
                           lambda d, k: (0, jnp.where(d == 0, k, nb - 1 - k), d))
    state_spec = pl.BlockSpec((1, bsz, n_state // 2), lambda d, k: (d, 0, 0))
    lam_spec = pl.BlockSpec((None, 1, 1, n_state // 4), lambda d, k: (layer, d, 0, 0))
    entering, leaving = pl.pallas_call(
        _s5_carry_kernel,
        grid=(2, nb),
        in_specs=[dir_blk, state_spec, lam_spec, lam_spec],
        out_specs=[dir_blk, state_spec],
        out_shape=[jax.ShapeDtypeStruct((bsz, nc, n_state), F32),
                   jax.ShapeDtypeStruct((2, bsz, n_state // 2), F32)],
        scratch_shapes=[pltpu.VMEM((bsz, n_state // 2), F32)],
        compiler_params=_params(2),
        name="s5_carry",
    )(exits.reshape(bsz, nc, n_state), init, lr, li)

    y2 = pl.pallas_call(
        _s5_out_kernel,
        grid=(n_rows // tr,),
        in_specs=[row_blk(width), row_blk(n_state), _layer_spec(toep, layer),
                  _layer_spec(csf, layer), _layer_spec(csb, layer)],
        out_specs=row_blk(width),
        out_shape=jax.ShapeDtypeStruct((n_rows, width), F32),
        compiler_params=_params(1),
        name="s5_out",
    )(u2, entering.reshape(n_rows, n_state), toep, csf, csb)
    return y2, leaving


def _pair_diag(m):
    g, r, c = m.shape
    eye = jnp.eye(2, dtype=m.dtype)
    return jnp.einsum('qarc,ab->qarbc', m.reshape(g // 2, 2, r, c), eye).reshape(g // 2, 2 * r, 2 * c)


def _s5_weights(lam_re, lam_im, log_dt, b_re, b_im, c_re, c_im):
    t = S5_T
    lam = lax.complex(lam_re.astype(F32), lam_im.astype(F32))
    lam_dt = lam * jnp.exp(log_dt.astype(F32))[..., None]
    lam_bar = jnp.exp(lam_dt)
    bb = ((lam_bar - 1) / lam)[..., None] * lax.complex(b_re.astype(F32), b_im.astype(F32))
    cm = lax.complex(c_re.astype(F32), c_im.astype(F32))
    steps = jnp.arange(t + 1, dtype=F32)
    pw = jnp.exp(lam_dt[None] * steps[:, None, None, None])

    kern = jnp.real(jnp.einsum('dgop,kdgp,dgpi->dkgoi', cm, pw[:t], bb))
    s_idx = jnp.arange(t)[:, None]
    t_idx = jnp.arange(t)[None, :]
    lag = t_idx - s_idx
    kf = jnp.where((lag >= 0)[..., None, None, None], kern[0][jnp.maximum(lag, 0)], 0.0)
    kb = jnp.where((lag <= 0)[..., None, None, None], kern[1][jnp.maximum(-lag, 0)], 0.0)
    toep = jnp.transpose(kf + kb, (2, 0, 4, 1, 3))
    toep = toep.reshape(SSM_GROUPS, t * SSM_GROUP, t * SSM_GROUP)

    ef = pw[t - 1 - jnp.arange(t), 0][..., None] * bb[0][None]
    eb = pw[jnp.arange(t), 1][..., None] * bb[1][None]
    to_cols = lambda m: jnp.transpose(m, (1, 0, 3, 2)).reshape(SSM_GROUPS, t * SSM_GROUP, SSM_STATE)
    we = jnp.concatenate([_pair_diag(to_cols(part(m))) for m in (ef, eb)
                          for part in (jnp.real, jnp.imag)], axis=-1)

    cf = cm[0][None] * pw[1 + jnp.arange(t), 0][:, :, None, :]
    cb = cm[1][None] * pw[t - jnp.arange(t), 1][:, :, None, :]
    to_rows = lambda m: jnp.transpose(m, (1, 3, 0, 2)).reshape(SSM_GROUPS, SSM_STATE, t * SSM_GROUP)
    cs = [jnp.concatenate([_pair_diag(to_rows(jnp.real(m))), _pair_diag(to_rows(-jnp.imag(m)))],
                          axis=1) for m in (cf, cb)]

    lam_t = pw[t].reshape(2, 1, N_STATE)
    return (toep.astype(BF16), we.astype(BF16), cs[0].astype(BF16), cs[1].astype(BF16),
            jnp.real(lam_t), jnp.imag(lam_t))


def _block_kernel(h_ref, attn_ref, cb_ref, z_ref, zp_ref, zn_ref, y2_ref, u_ref, mod_ref, g_ref,
                  cw_ref, dsk_ref, wglu_ref, bglu_ref, wout_ref, w1_ref, w2_ref, o_ref, y_scr,
                  mix_scr, *, tm, n_batch, n_tiles):
    step_id = pl.program_id(0)

    @pl.when(step_id == 0)
    def _():
        mix_scr[...] = jnp.zeros_like(mix_scr)

    j = jnp.minimum(step_id, n_tiles * n_batch - 1) // n_batch

    def step(new, old):
        mod = mod_ref[0]
        m = jnp.dot(mix_scr[old], wout_ref[...], preferred_element_type=F32)
        h1 = h_ref[0] + mod[2:3, :] * _rms(m, g_ref[1:2, :])
        a2 = (_rms(h1, g_ref[2:3, :]) * (1.0 + mod[4:5, :]) + mod[3:4, :]).astype(BF16)
        f = jnp.zeros((tm, D_MODEL), F32)
        for cidx in range(D_FF // FF_CHUNK):
            cs = slice(cidx * FF_CHUNK, (cidx + 1) * FF_CHUNK)
            t = jnp.maximum(jnp.dot(a2, w1_ref[:, cs], preferred_element_type=F32), 0.0)
            f = f + jnp.dot((t * t).astype(BF16), w2_ref[cs, :], preferred_element_type=F32)
        o_ref[0] = h1 + mod[5:6, :] * _rms(f, g_ref[3:4, :])

        z = z_ref[0]
        has_prev = jnp.where(j > 0, 1.0, 0.0)
        has_next = jnp.where(j < n_tiles - 1, 1.0, 0.0)
        z_before = zp_ref[0][SUBLANES - 1:SUBLANES, :] * has_prev
        z_after = zn_ref[0][0:1, :] * has_next
        row = lax.broadcasted_iota(jnp.int32, (tm, 1), 0)
        z_dn = jnp.where(row == 0, z_before, pltpu.roll(z, 1, axis=0))
        z_up = jnp.where(row == tm - 1, z_after, pltpu.roll(z, tm - 1, axis=0))
        conv = cb_ref[0] * (z_dn * cw_ref[0:1, :] + z * cw_ref[1:2, :] + z_up * cw_ref[2:3, :])
        n_chunks = y2_ref.shape[0]
        halves = S5_T * SSM_GROUP // LANES
        groups = [[y2_ref[:, (gi * halves + hf) * LANES:(gi * halves + hf + 1) * LANES]
                   for hf in range(halves)] for gi in range(SSM_GROUPS)]
        steps = _piece_transpose(groups)
        for s in range(S5_T):
            for hf in range(SSM_WIDTH // LANES):
                y_scr[hf, pl.ds(s, n_chunks, stride=S5_T), :] = steps[s][hf]
        y = jnp.concatenate([y_scr[hf] for hf in range(SSM_WIDTH // LANES)], axis=1)
        y = y + dsk_ref[...] * u_ref[0]
        gl = 0.5 * y * (1.0 + jnp.tanh(math.sqrt(2.0 / math.pi) * (y + 0.044715 * (y * y * y))))
        gate = jnp.dot(gl.astype(BF16), wglu_ref[...], preferred_element_type=F32) + bglu_ref[...]
        ssm = gl * _sigmoid(gate)
        a0, a1 = ATTN_WIDTH, ATTN_WIDTH + CONV_WIDTH
        mix_scr[new, :, 0:a0] = attn_ref[0]
        mix_scr[new, :, a0:a1] = conv.astype(BF16)
        mix_scr[new, :, a1:] = ssm.astype(BF16)

    for k in range(2):
        pl.when(step_id % 2 == k)(functools.partial(step, k, 1 - k))


def _block(h, attn, cb, z, y2, u, mods, mod_row, params, tm, layer):
    b, n, _ = h.shape
    nt = n // tm
    total = nt * b
    per = tm // SUBLANES
    n8 = n // SUBLANES

    def tile(step, lag):
        m = jnp.clip(step - lag, 0, total - 1)
        return m // b, m % b

    def row(w, lag):
        def index(step):
            j, i = tile(step, lag)
            return i, j, 0
        return pl.BlockSpec((1, tm, w), index)

    def halo(offset):
        def index(step):
            j, i = tile(step, 0)
            return i, jnp.clip(j * per + offset, 0, n8 - 1), 0
        return pl.BlockSpec((1, SUBLANES, CONV_WIDTH), index)

    def chunk_index(step):
        j, i = tile(step, 0)
        return i * nt + j, 0

    return pl.pallas_call(
        functools.partial(_block_kernel, tm=tm, n_batch=b, n_tiles=nt),
        grid=(total + 1,),
        in_specs=[row(D_MODEL, 1), row(ATTN_WIDTH, 0), row(CONV_WIDTH, 0), row(CONV_WIDTH, 0),
                  halo(-1), halo(per),
                  pl.BlockSpec((tm // S5_T, S5_T * SSM_WIDTH), chunk_index), row(SSM_WIDTH, 0),
                  pl.BlockSpec((1, N_MOD, D_MODEL), lambda step: (mod_row(tile(step, 1)[1]), 0, 0))]
        + [_layer_spec(p, layer) for p in params],
        out_specs=row(D_MODEL, 1),
        out_shape=jax.ShapeDtypeStruct((b, n, D_MODEL), F32),
        scratch_shapes=[pltpu.VMEM((SSM_WIDTH // LANES, tm, LANES), F32),
                        pltpu.VMEM((2, tm, MIX_WIDTH), BF16)],
        compiler_params=_params(1),
        name="mix_mlp_block",
    )(h, attn, cb, z, z, z, y2, u, mods, *params)


def _rope_tables(n_lat, n_ctx):
    f32 = np.float32
    rows = n_lat // GRID_W
    row = np.repeat(np.arange(rows), GRID_W).astype(f32)[:, None]
    col = np.tile(np.arange(GRID_W), rows).astype(f32)[:, None]
    freqs = (f32(ROPE_BASE) ** (-np.arange(ROPE_PAIRS, dtype=f32) / f32(ROPE_PAIRS))).astype(f32)
    ang_r, ang_c = row * freqs, col * freqs
    cos_h = np.concatenate([np.cos(ang_r)] * 2 + [np.cos(ang_c)] * 2, axis=-1)
    sin_h = np.concatenate([-np.sin(ang_r), np.sin(ang_r), -np.sin(ang_c), np.sin(ang_c)], axis=-1)
    scale = np.concatenate([np.full((ATTN_WIDTH,), LOG2E * HEAD_DIM ** -0.5, f32),
                            np.ones((KV_WIDTH,), f32)])
    n_heads = N_Q_HEADS + N_KV_HEADS
    cos = (np.tile(cos_h, (1, n_heads)) * scale).astype(f32)
    sin = (np.tile(sin_h, (1, n_heads)) * scale).astype(f32)
    cos_c = np.broadcast_to(scale, (n_ctx, ROPE_WIDTH)).astype(f32)
    sin_c = np.zeros((n_ctx, ROPE_WIDTH), f32)
    return tuple(jnp.asarray(t) for t in (cos, sin, cos_c, sin_c))


def kernel(x, c, ctx, c_ctx, w_ada, b_ada, norm_g, w_in, conv_w, attn_sink, ssm_lam_re, ssm_lam_im,
           ssm_log_dt, ssm_b_re, ssm_b_im, ssm_c_re, ssm_c_im, ssm_d, w_glu, b_glu, w_out,
           w_mlp_in, w_mlp_out):
    bsz, n_lat, _ = x.shape
    n_ctx = ctx.shape[1]
    tm_lat = min(ROW_TILE, n_lat)
    tm_ctx = min(ROW_TILE, n_ctx)
    cos, sin, cos_c, sin_c = _rope_tables(n_lat, n_ctx)

    pad = (-(bsz + 1)) % SUBLANES
    cvec = jnp.concatenate([c, c_ctx[None, :], jnp.zeros((pad, D_MODEL), F32)], axis=0)
    rows = cvec.shape[0]
    mods = _ada_mods(cvec, w_ada, b_ada).reshape(DEPTH * rows, N_MOD, D_MODEL)

    s5w = jax.vmap(_s5_weights)(ssm_lam_re, ssm_lam_im, ssm_log_dt, ssm_b_re, ssm_b_im,
                                ssm_c_re, ssm_c_im)
    w_in_b = w_in.astype(BF16)
    block_params = (norm_g, conv_w, ssm_d.reshape(DEPTH, 1, SSM_WIDTH), w_glu.astype(BF16),
                    b_glu.reshape(DEPTH, 1, SSM_WIDTH), w_out.astype(BF16),
                    w_mlp_in.astype(BF16), w_mlp_out.astype(BF16))
    zero_state = jnp.zeros((2, bsz, 2 * N_STATE), F32)

    h, hc = x, ctx
    for l in range(DEPTH):
        with_ctx_out = l < DEPTH - 1
        lat_row = lambda i, base=l * rows: base + i
        ctx_row = lambda i, base=l * rows + bsz: base
        qe, qo, k2, v4, cb, z, u, u2 = _in_proj(h, mods, lat_row, norm_g, w_in_b, cos, sin,
                                                tm_lat, True, l)
        qc, kc, vc, k2c, v4c, cbc, zc, uc, u2c = _in_proj(hc, mods, ctx_row, norm_g, w_in_b,
                                                          cos_c, sin_c, tm_ctx, False, l)

        attn = _attention_lat(attn_sink, l, qe, qo, k2, v4, k2c, v4c)

        y2c, ctx_state = _s5_segment(u2c, bsz, s5w, zero_state, l)
        y2, _ = _s5_segment(u2, bsz, s5w, ctx_state, l)

        h = _block(h, attn, cb, z, y2, u, mods, lat_row, block_params, tm_lat, l)
        if with_ctx_out:
            attn_c = _attention_ctx(attn_sink, l, qc, kc, vc)
            hc = _block(hc, attn_c, cbc, zc, y2c, uc, mods, ctx_row, block_params, tm_ctx, l)
    return h
```

```python
import functools
import math

import numpy as np
import jax
import jax.numpy as jnp
from jax import lax
from jax.experimental import pallas as pl
from jax.experimental.pallas import tpu as pltpu

F32 = jnp.float32
BF16 = jnp.bfloat16

D_MODEL = 1024
DEPTH = 4
GRID_W = 64
HEAD_DIM = 64
N_Q_HEADS = 8
N_KV_HEADS = 2
Q_PER_KV = N_Q_HEADS // N_KV_HEADS
ATTN_WIDTH = N_Q_HEADS * HEAD_DIM
KV_WIDTH = N_KV_HEADS * HEAD_DIM
WINDOW = 128
Q_BLOCK = 128
ROPE_BASE = 10000.0
ROPE_PAIRS = HEAD_DIM // 4
CONV_WIDTH = 256
SSM_WIDTH = 256
SSM_GROUP = 16
SSM_GROUPS = SSM_WIDTH // SSM_GROUP
SSM_STATE = 64
N_STATE = SSM_GROUPS * SSM_STATE
MIX_WIDTH = ATTN_WIDTH + CONV_WIDTH + SSM_WIDTH
IN_WIDTH = ATTN_WIDTH + 2 * KV_WIDTH + 3 * CONV_WIDTH + SSM_WIDTH
ROPE_WIDTH = ATTN_WIDTH + KV_WIDTH
D_FF = 4 * D_MODEL
N_MOD = 6
EPS = 1e-6
NEG_INF = -1e30
LOG2E = math.log2(math.e)

LANES = 128
SUBLANES = 8
VMEM_LIMIT = 56 * 1024 * 1024

ROW_TILE = 512
ATTN_GROUP = 2
S5_T = 16
S5_BLOCK_CHUNKS = 8
S5_ROW_TILE = 256
S5_PAIRS = SSM_GROUPS // 2
PAIR_IN = 2 * SSM_GROUP * S5_T
PAIR_STATE = 4 * SSM_STATE
FF_CHUNK = 1024


def _layer_spec(arr, layer):
    nd = arr.ndim
    return pl.BlockSpec((None,) + arr.shape[1:], lambda *_: (layer,) + (0,) * (nd - 1),
                        pipeline_mode=pl.Buffered(1))


def _params(n_grid):
    return pltpu.CompilerParams(dimension_semantics=("arbitrary",) * n_grid,
                                vmem_limit_bytes=VMEM_LIMIT)


def _rms(x, g):
    ms = jnp.mean(x * x, axis=-1, keepdims=True)
    return x * lax.rsqrt(ms + EPS) * g


def _sigmoid(x):
    return 1.0 / (1.0 + jnp.exp(-x))


def _ada_kernel(c_ref, w_ref, b_ref, o_ref):
    c = c_ref[...]
    act = (c * _sigmoid(c)).astype(BF16)
    o_ref[0] = jnp.dot(act, w_ref[0].astype(BF16), preferred_element_type=F32) + b_ref[0]


def _ada_mods(cvec, w_ada, b_ada):
    rows = cvec.shape[0]
    tn = 1024
    n_out = N_MOD * D_MODEL
    return pl.pallas_call(
        _ada_kernel,
        grid=(DEPTH, n_out // tn),
        in_specs=[pl.BlockSpec((rows, D_MODEL), lambda l, j: (0, 0)),
                  pl.BlockSpec((1, D_MODEL, tn), lambda l, j: (l, 0, j)),
                  pl.BlockSpec((1, 1, tn), lambda l, j: (l, 0, j))],
        out_specs=pl.BlockSpec((1, rows, tn), lambda l, j: (l, 0, j)),
        out_shape=jax.ShapeDtypeStruct((DEPTH, rows, n_out), F32),
        compiler_params=_params(2),
        name="ada_mods",
    )(cvec, w_ada, b_ada.reshape(DEPTH, 1, n_out))


def _piece_transpose(src):
    per = LANES // SSM_GROUP
    size = len(src)
    assert size == 2 * per and len(src[0]) == 2
    n = src[0][0].shape[0]
    slot = lax.broadcasted_iota(jnp.int32, (n, LANES), 1) // SSM_GROUP
    rows = [list(r) for r in src]
    for r in range(per):
        rows[r][1], rows[r + per][0] = rows[r + per][0], rows[r][1]
    d = per // 2
    while d >= 1:
        keep = (slot & d) == 0
        for r in range(size):
            if r & d:
                continue
            for hf in range(2):
                lo, hi = rows[r][hf], rows[r + d][hf]
                rows[r][hf] = jnp.where(keep, lo, pltpu.roll(hi, d * SSM_GROUP, axis=1))
                rows[r + d][hf] = jnp.where(keep, pltpu.roll(lo, LANES - d * SSM_GROUP, axis=1), hi)
        d //= 2
    return rows


def _inproj_kernel(*refs, lat):
    h_ref, mod_ref, g_ref, w_ref, cos_ref, sin_ref = refs[:6]
    if lat:
        qe_ref, qo_ref, k2_ref, v4_ref, cb_ref, z_ref, u_ref, u2_ref = refs[6:14]
    else:
        q_ref, k_ref, v_ref, k2_ref, v4_ref, cb_ref, z_ref, u_ref, u2_ref = refs[6:15]
    u_scr, a_scr, p_scr = refs[-3:]
    step_id = pl.program_id(0)

    @pl.when(step_id == 0)
    def _():
        a_scr[...] = jnp.zeros_like(a_scr)
        p_scr[...] = jnp.zeros_like(p_scr)

    lane = lax.broadcasted_iota(jnp.int32, (1, LANES), 1)
    first_of_pair = (lane & ROPE_PAIRS) == 0
    low_head = lane < HEAD_DIM

    def step(new, old):
        p_scr[old] = jnp.dot(a_scr[old], w_ref[...], preferred_element_type=F32)

        mod = mod_ref[0]
        a = _rms(h_ref[0], g_ref[0:1, :]) * (1.0 + mod[1:2, :]) + mod[0:1, :]
        a_scr[new] = a.astype(BF16)

        for s in range(ROPE_WIDTH // LANES):
            sl = slice(s * LANES, (s + 1) * LANES)
            t = p_scr[new, :, sl]
            partner = jnp.where(first_of_pair,
                                pltpu.roll(t, LANES - ROPE_PAIRS, axis=1),
                                pltpu.roll(t, ROPE_PAIRS, axis=1))
            r = t * cos_ref[:, sl] + partner * sin_ref[:, sl]
            if s < ATTN_WIDTH // LANES:
                if lat:
                    qe_ref[0, :, sl] = jnp.where(low_head, r, 0.0).astype(BF16)
                    qo_ref[0, :, sl] = jnp.where(low_head, 0.0, r).astype(BF16)
                else:
                    q_ref[0, :, sl] = r.astype(BF16)
            else:
                k2_ref[0, :, 0:LANES] = r.astype(BF16)
                k2_ref[0, :, LANES:2 * LANES] = pltpu.roll(r, HEAD_DIM, axis=1).astype(BF16)
                if not lat:
                    k_ref[0] = r.astype(BF16)
        o = ROPE_WIDTH
        v = p_scr[new, :, o:o + KV_WIDTH]
        v_swapped = pltpu.roll(v, HEAD_DIM, axis=1)
        for idx, part in enumerate((jnp.where(low_head, v, 0.0),
                                    jnp.where(low_head, 0.0, v_swapped),
                                    jnp.where(low_head, v_swapped, 0.0),
                                    jnp.where(low_head, 0.0, v))):
            v4_ref[0, :, idx * LANES:(idx + 1) * LANES] = part.astype(BF16)
        if not lat:
            v_ref[0] = v.astype(BF16)
        o += KV_WIDTH
        cb_ref[0] = p_scr[new, :, o:o + CONV_WIDTH]
        o += CONV_WIDTH
        z_ref[0] = (p_scr[new, :, o:o + CONV_WIDTH]
                    * p_scr[new, :, o + CONV_WIDTH:o + 2 * CONV_WIDTH])
        o += 2 * CONV_WIDTH
        u_ref[0] = p_scr[new, :, o:o + SSM_WIDTH]
        n_chunks = u2_ref.shape[0]
        halves = SSM_WIDTH // LANES
        for hf in range(halves):
            u_scr[hf] = p_scr[new, :, o + hf * LANES:o + (hf + 1) * LANES]
        steps = [[u_scr[hf, pl.ds(s, n_chunks, stride=S5_T), :]
                  for hf in range(halves)] for s in range(S5_T)]
        groups = _piece_transpose(steps)
        for gi in range(SSM_GROUPS):
            for hf in range(halves):
                c0 = gi * SSM_GROUP * S5_T + hf * LANES
                u2_ref[:, c0:c0 + LANES] = groups[gi][hf].astype(BF16)

    for k in range(2):
        pl.when(step_id % 2 == k)(functools.partial(step, k, 1 - k))


def _in_proj(h, mods, mod_row, norm_g, w_in, cos, sin, tm, lat, layer):
    b, n, _ = h.shape
    nt = n // tm
    total = nt * b
    if lat:
        widths = [(ATTN_WIDTH, BF16), (ATTN_WIDTH, BF16)]
    else:
        widths = [(ATTN_WIDTH, BF16), (KV_WIDTH, BF16), (KV_WIDTH, BF16)]
    widths += [(2 * KV_WIDTH, BF16), (4 * KV_WIDTH, BF16),
               (CONV_WIDTH, F32), (CONV_WIDTH, F32), (SSM_WIDTH, F32)]

    def tile(step, lag):
        m = jnp.clip(step - lag, 0, total - 1)
        return m // b, m % b

    def row(w, lag):
        def index(step):
            j, i = tile(step, lag)
            return i, j, 0
        return pl.BlockSpec((1, tm, w), index)

    def chunk_index(step):
        j, i = tile(step, 2)
        return i * nt + j, 0

    table = pl.BlockSpec((tm, ROPE_WIDTH), lambda step: (tile(step, 2)[0], 0))
    return pl.pallas_call(
        functools.partial(_inproj_kernel, lat=lat),
        grid=(total + 2,),
        in_specs=[row(D_MODEL, 0),
                  pl.BlockSpec((1, N_MOD, D_MODEL), lambda step: (mod_row(tile(step, 0)[1]), 0, 0)),
                  _layer_spec(norm_g, layer),
                  _layer_spec(w_in, layer),
                  table, table],
        out_specs=[row(w, 2) for w, _ in widths]
        + [pl.BlockSpec((tm // S5_T, S5_T * SSM_WIDTH), chunk_index)],
        out_shape=[jax.ShapeDtypeStruct((b, n, w), dt) for w, dt in widths]
        + [jax.ShapeDtypeStruct((b * n // S5_T, S5_T * SSM_WIDTH), BF16)],
        scratch_shapes=[pltpu.VMEM((SSM_WIDTH // LANES, tm, LANES), F32),
                        pltpu.VMEM((2, tm, D_MODEL), BF16),
                        pltpu.VMEM((2, tm, IN_WIDTH), F32)],
        compiler_params=_params(1),
        name="in_proj",
    )(h, mods, norm_g, w_in, cos, sin)


def _softmax_parts(s, sink):
    m = jnp.maximum(jnp.max(s, axis=-1, keepdims=True), sink)
    p = jnp.exp2(s - m)
    denom = jnp.sum(p, axis=-1, keepdims=True) + jnp.exp2(sink - m)
    return p, 1.0 / denom


def _attn_ctx_kernel(sink_ref, q_ref, k_ref, v_ref, o_ref, *, layer):
    q, k, v = q_ref[0], k_ref[0], v_ref[0]
    outs = []
    for h in range(N_Q_HEADS):
        hk = h // Q_PER_KV
        ksl = slice(hk * HEAD_DIM, (hk + 1) * HEAD_DIM)
        s = lax.dot_general(q[:, h * HEAD_DIM:(h + 1) * HEAD_DIM], k[:, ksl],
                            (((1,), (1,)), ((), ())), preferred_element_type=F32)
        p, inv = _softmax_parts(s, sink_ref[layer, h] * LOG2E)
        outs.append(jnp.dot(p.astype(BF16), v[:, ksl], preferred_element_type=F32) * inv)
    o_ref[0] = jnp.concatenate(outs, axis=1).astype(BF16)


def _attn_lat_kernel(sink_ref, qe_ref, qo_ref, k_ref, kx_ref, v_ref, vx_ref, bias_ref, o_ref,
                     s_scr, p_scr, inv_scr, *, tq, layer, n_blocks, total):
    n = pl.program_id(0)
    groups_per_seq = n_blocks // ATTN_GROUP

    def block(lag, sub):
        return (jnp.clip(n - lag, 0, total - 1) % groups_per_seq) * ATTN_GROUP + sub

    def window(lag, sub):
        return pl.multiple_of(jnp.clip(block(lag, sub) - 1, 0, n_blocks - 3) * tq, tq)

    @pl.when(n == 0)
    def _():
        s_scr[...] = jnp.zeros_like(s_scr)
        p_scr[...] = jnp.zeros_like(p_scr)
        inv_scr[...] = jnp.zeros_like(inv_scr)

    lane = lax.broadcasted_iota(jnp.int32, (1, LANES), 1)
    low_head = lane < HEAD_DIM
    slabs = ATTN_WIDTH // LANES

    def step(new, old):
        for sub in range(ATTN_GROUP):
            rows = slice(sub * tq, (sub + 1) * tq)
            vall = jnp.concatenate([v_ref[0, pl.ds(window(2, sub), 3 * tq), :], vx_ref[0]], axis=0)
            for t in range(slabs):
                c0 = (t // (slabs // N_KV_HEADS)) * 2 * LANES
                o = (jnp.dot(p_scr[new, sub, 2 * t], vall[:, c0:c0 + LANES],
                             preferred_element_type=F32)
                     + jnp.dot(p_scr[new, sub, 2 * t + 1], vall[:, c0 + LANES:c0 + 2 * LANES],
                               preferred_element_type=F32))
                o_ref[0, rows, t * LANES:(t + 1) * LANES] = (o * inv_scr[new, sub, t]).astype(BF16)

            kall = jnp.concatenate([k_ref[0, pl.ds(window(0, sub), 3 * tq), :], kx_ref[0]], axis=0)
            for hk in range(N_KV_HEADS):
                t0 = hk * (slabs // N_KV_HEADS)
                for par, q_ref in enumerate((qe_ref, qo_ref)):
                    lhs = jnp.concatenate([q_ref[0, rows, t0 * LANES:(t0 + 1) * LANES],
                                           q_ref[0, rows, (t0 + 1) * LANES:(t0 + 2) * LANES]],
                                          axis=0)
                    kv = (hk + par) % 2
                    s = lax.dot_general(lhs, kall[:, kv * LANES:(kv + 1) * LANES],
                                        (((1,), (1,)), ((), ())), preferred_element_type=F32)
                    s_scr[new, sub, 2 * t0 + par] = s[:tq]
                    s_scr[new, sub, 2 * t0 + 2 + par] = s[tq:]

            j = block(1, sub)
            variant = jnp.where(j == 0, 0, jnp.where(j == n_blocks - 1, 2, 1))
            bias = bias_ref[variant, :, 0:3 * tq]
            invs = []
            for h in range(N_Q_HEADS):
                s = jnp.concatenate([s_scr[old, sub, h, :, 0:3 * tq] + bias,
                                     s_scr[old, sub, h, :, 3 * tq:]], axis=1)
                p, inv = _softmax_parts(s, sink_ref[layer, h] * LOG2E)
                p_scr[old, sub, h] = p.astype(BF16)
                invs.append(inv)
            for t in range(slabs):
                inv_scr[old, sub, t] = jnp.where(low_head, invs[2 * t], invs[2 * t + 1])

    for k in range(2):
        pl.when(n % 2 == k)(functools.partial(step, k, 1 - k))


def _window_bias(tq, n_ctx):
    r = np.arange(tq)[:, None]
    c = np.arange(3 * tq + n_ctx)[None, :]
    variants = []
    for q_start in (0, tq, 2 * tq):
        ok = (c >= 3 * tq) | (np.abs(c - (q_start + r)) <= WINDOW)
        variants.append(np.where(ok, 0.0, NEG_INF))
    return jnp.asarray(np.stack(variants), F32)


def _attention_lat(sink, layer, qe, qo, k2, v4, k2x, v4x):
    b, n, _ = qe.shape
    n_ctx = k2x.shape[1]
    tq = Q_BLOCK
    nb = n // tq
    groups = nb // ATTN_GROUP
    total = b * groups
    n_keys = 3 * tq + n_ctx
    assert tq == WINDOW and nb >= 3 and nb % ATTN_GROUP == 0

    def group_of(step, lag):
        m = jnp.clip(step - lag, 0, total - 1)
        return m // groups, m % groups

    def rows(width, lag):
        return pl.BlockSpec((1, ATTN_GROUP * tq, width), lambda step: (*group_of(step, lag), 0))

    def whole(length, width, lag):
        return pl.BlockSpec((1, length, width), lambda step: (group_of(step, lag)[0], 0, 0))

    bias = _window_bias(tq, n_ctx)
    kw, vw = 2 * KV_WIDTH, 4 * KV_WIDTH
    return pl.pallas_call(
        functools.partial(_attn_lat_kernel, tq=tq, layer=layer, n_blocks=nb, total=total),
        grid=(total + 2,),
        in_specs=[pl.BlockSpec(memory_space=pltpu.SMEM),
                  rows(ATTN_WIDTH, 0), rows(ATTN_WIDTH, 0),
                  whole(n, kw, 0), whole(n_ctx, kw, 0),
                  whole(n, vw, 2), whole(n_ctx, vw, 2),
                  pl.BlockSpec(bias.shape, lambda step: (0, 0, 0), pipeline_mode=pl.Buffered(1))],
        out_specs=rows(ATTN_WIDTH, 2),
        out_shape=jax.ShapeDtypeStruct((b, n, ATTN_WIDTH), BF16),
        scratch_shapes=[pltpu.VMEM((2, ATTN_GROUP, N_Q_HEADS, tq, n_keys), F32),
                        pltpu.VMEM((2, ATTN_GROUP, N_Q_HEADS, tq, n_keys), BF16),
                        pltpu.VMEM((2, ATTN_GROUP, ATTN_WIDTH // LANES, tq, LANES), F32)],
        compiler_params=_params(1),
        name="attn_lat",
    )(sink, qe, qo, k2, k2x, v4, v4x, bias)


def _attention_ctx(sink, layer, q, k, v):
    b, n, _ = q.shape
    spec = lambda w: pl.BlockSpec((1, n, w), lambda i: (i, 0, 0))
    return pl.pallas_call(
        functools.partial(_attn_ctx_kernel, layer=layer),
        grid=(b,),
        in_specs=[pl.BlockSpec(memory_space=pltpu.SMEM), spec(ATTN_WIDTH), spec(KV_WIDTH),
                  spec(KV_WIDTH)],
        out_specs=spec(ATTN_WIDTH),
        out_shape=jax.ShapeDtypeStruct((b, n, ATTN_WIDTH), BF16),
        compiler_params=_params(1),
        name="attn_ctx",
    )(sink, q, k, v)


def _s5_exit_kernel(u_ref, we_ref, e_ref):
    half = e_ref.shape[1] // 2
    for gp in range(S5_PAIRS):
        r = jnp.dot(u_ref[:, gp * PAIR_IN:(gp + 1) * PAIR_IN], we_ref[gp],
                    preferred_element_type=F32)
        e_ref[:, gp * PAIR_STATE:(gp + 1) * PAIR_STATE] = r[:, :PAIR_STATE]
        e_ref[:, half + gp * PAIR_STATE:half + (gp + 1) * PAIR_STATE] = r[:, PAIR_STATE:]


def _s5_carry_kernel(e_ref, init_ref, lr_ref, li_ref, p_ref, fin_ref, state):
    @pl.when(pl.program_id(1) == 0)
    def _():
        state[...] = init_ref[0]

    def run(chunks):
        for gp in range(S5_PAIRS):
            re = slice(gp * PAIR_STATE, gp * PAIR_STATE + LANES)
            im = slice(gp * PAIR_STATE + LANES, (gp + 1) * PAIR_STATE)
            lr = lr_ref[0, :, gp * LANES:(gp + 1) * LANES]
            li = li_ref[0, :, gp * LANES:(gp + 1) * LANES]
            sr, si = state[:, re], state[:, im]
            for k in chunks:
                p_ref[:, k, re] = sr
                p_ref[:, k, im] = si
                sr, si = (lr * sr - li * si + e_ref[:, k, re],
                          lr * si + li * sr + e_ref[:, k, im])
            state[:, re] = sr
            state[:, im] = si

    n_chunks = e_ref.shape[1]
    direction = pl.program_id(0)
    pl.when(direction == 0)(lambda: run(tuple(range(n_chunks))))
    pl.when(direction == 1)(lambda: run(tuple(reversed(range(n_chunks)))))
    fin_ref[0] = state[...]


def _s5_out_kernel(u_ref, p_ref, toep_ref, csf_ref, csb_ref, y_ref):
    half = p_ref.shape[1] // 2
    grp = PAIR_IN // 2
    for gp in range(S5_PAIRS):
        pf = p_ref[:, gp * PAIR_STATE:(gp + 1) * PAIR_STATE].astype(BF16)
        pb = p_ref[:, half + gp * PAIR_STATE:half + (gp + 1) * PAIR_STATE].astype(BF16)
        st = (jnp.dot(pf, csf_ref[gp], preferred_element_type=F32)
              + jnp.dot(pb, csb_ref[gp], preferred_element_type=F32))
        for a in range(2):
            cols = slice(gp * PAIR_IN + a * grp, gp * PAIR_IN + (a + 1) * grp)
            y_ref[:, cols] = (jnp.dot(u_ref[:, cols], toep_ref[2 * gp + a],
                                      preferred_element_type=F32)
                              + st[:, a * grp:(a + 1) * grp])


def _s5_segment(u2, bsz, weights, init, layer):
    toep, we, csf, csb, lr, li = weights
    n_rows, width = u2.shape
    nc = n_rows // bsz
    tr = min(S5_ROW_TILE, n_rows)
    n_state = 2 * S5_PAIRS * PAIR_STATE
    row_blk = lambda w: pl.BlockSpec((tr, w), lambda j: (j, 0))
    exits = pl.pallas_call(
        _s5_exit_kernel,
        grid=(n_rows // tr,),
        in_specs=[row_blk(width), _layer_spec(we, layer)],
        out_specs=row_blk(n_state),
        out_shape=jax.ShapeDtypeStruct((n_rows, n_state), F32),
        compiler_params=_params(1),
        name="s5_exit",
    )(u2, we)

    nb = nc // S5_BLOCK_CHUNKS
    dir_blk = pl.BlockSpec((bsz, S5_BLOCK_CHUNKS, n_state // 2),
                           lambda d, k: (0, jnp.where(d == 0, k, nb - 1 - k), d))
    state_spec = pl.BlockSpec((1, bsz, n_state // 2), lambda d, k: (d, 0, 0))
    lam_spec = pl.BlockSpec((None, 1, 1, n_state // 4), lambda d, k: (layer, d, 0, 0))
    entering, leaving = pl.pallas_call(
        _s5_carry_kernel,
        grid=(2, nb),
        in_specs=[dir_blk, state_spec, lam_spec, lam_spec],
        out_specs=[dir_blk, state_spec],
        out_shape=[jax.ShapeDtypeStruct((bsz, nc, n_state), F32),
                   jax.ShapeDtypeStruct((2, bsz, n_state // 2), F32)],
        scratch_shapes=[pltpu.VMEM((bsz, n_state // 2), F32)],
        compiler_params=_params(2),
        name="s5_carry",
    )(exits.reshape(bsz, nc, n_state), init, lr, li)

    y2 = pl.pallas_call(
        _s5_out_kernel,
        grid=(n_rows // tr,),
        in_specs=[row_blk(width), row_blk(n_state), _layer_spec(toep, layer),
                  _layer_spec(csf, layer), _layer_spec(csb, layer)],
        out_specs=row_blk(width),
        out_shape=jax.ShapeDtypeStruct((n_rows, width), F32),
        compiler_params=_params(1),
        name="s5_out",
    )(u2, entering.reshape(n_rows, n_state), toep, csf, csb)
    return y2, leaving


def _pair_diag(m):
    g, r, c = m.shape
    eye = jnp.eye(2, dtype=m.dtype)
    return jnp.einsum('qarc,ab->qarbc', m.reshape(g // 2, 2, r, c), eye).reshape(g // 2, 2 * r, 2 * c)


def _toeplitz_indices(t):
    s, i, tt, o = np.meshgrid(np.arange(t), np.arange(SSM_GROUP), np.arange(t),
                              np.arange(SSM_GROUP), indexing='ij')
    lag = tt - s
    zero = 2 * t * SSM_GROUP * SSM_GROUP
    flat = lambda d, k: ((d * t + k) * SSM_GROUP + o) * SSM_GROUP + i
    idx_f = np.where(lag >= 0, flat(0, np.maximum(lag, 0)), zero)
    idx_b = np.where(lag <= 0, flat(1, np.maximum(-lag, 0)), zero)
    side = t * SSM_GROUP
    return (jnp.asarray(idx_f.reshape(side, side), jnp.int32),
            jnp.asarray(idx_b.reshape(side, side), jnp.int32))


def _s5_weights(lam_re, lam_im, log_dt, b_re, b_im, c_re, c_im):
    t = S5_T
    lam = lax.complex(lam_re.astype(F32), lam_im.astype(F32))
    lam_dt = lam * jnp.exp(log_dt.astype(F32))[..., None]
    lam_bar = jnp.exp(lam_dt)
    bb = ((lam_bar - 1) / lam)[..., None] * lax.complex(b_re.astype(F32), b_im.astype(F32))
    cm = lax.complex(c_re.astype(F32), c_im.astype(F32))
    steps = jnp.arange(t + 1, dtype=F32)
    pw = jnp.exp(lam_dt[None] * steps[:, None, None, None])

    kern = jnp.real(jnp.einsum('dgop,kdgp,dgpi->gdkoi', cm, pw[:t], bb))
    flat = jnp.concatenate([kern.reshape(SSM_GROUPS, -1), jnp.zeros((SSM_GROUPS, 1), F32)], axis=1)
    idx_f, idx_b = _toeplitz_indices(t)
    toep = jnp.take(flat, idx_f, axis=1) + jnp.take(flat, idx_b, axis=1)

    ef = pw[t - 1 - jnp.arange(t), 0][..., None] * bb[0][None]
    eb = pw[jnp.arange(t), 1][..., None] * bb[1][None]
    to_cols = lambda m: jnp.transpose(m, (1, 0, 3, 2)).reshape(SSM_GROUPS, t * SSM_GROUP, SSM_STATE)
    we = jnp.concatenate([_pair_diag(to_cols(part(m))) for m in (ef, eb)
                          for part in (jnp.real, jnp.imag)], axis=-1)

    cf = cm[0][None] * pw[1 + jnp.arange(t), 0][:, :, None, :]
    cb = cm[1][None] * pw[t - jnp.arange(t), 1][:, :, None, :]
    to_rows = lambda m: jnp.transpose(m, (1, 3, 0, 2)).reshape(SSM_GROUPS, SSM_STATE, t * SSM_GROUP)
    cs = [jnp.concatenate([_pair_diag(to_rows(jnp.real(m))), _pair_diag(to_rows(-jnp.imag(m)))],
                          axis=1) for m in (cf, cb)]

    lam_t = pw[t].reshape(2, 1, N_STATE)
    return (toep.astype(BF16), we.astype(BF16), cs[0].astype(BF16), cs[1].astype(BF16),
            jnp.real(lam_t), jnp.imag(lam_t))


def _block_kernel(h_ref, attn_ref, cb_ref, z_ref, zp_ref, zn_ref, y2_ref, u_ref, mod_ref, g_ref,
                  cw_ref, dsk_ref, wglu_ref, bglu_ref, wout_ref, w1_ref, w2_ref, o_ref, y_scr,
                  mix_scr, *, tm, n_batch, n_tiles):
    step_id = pl.program_id(0)

    @pl.when(step_id == 0)
    def _():
        mix_scr[...] = jnp.zeros_like(mix_scr)

    j = jnp.minimum(step_id, n_tiles * n_batch - 1) // n_batch

    def step(new, old):
        mod = mod_ref[0]
        m = jnp.dot(mix_scr[old], wout_ref[...], preferred_element_type=F32)
        h1 = h_ref[0] + mod[2:3, :] * _rms(m, g_ref[1:2, :])
        a2 = (_rms(h1, g_ref[2:3, :]) * (1.0 + mod[4:5, :]) + mod[3:4, :]).astype(BF16)
        f = jnp.zeros((tm, D_MODEL), F32)
        for cidx in range(D_FF // FF_CHUNK):
            cs = slice(cidx * FF_CHUNK, (cidx + 1) * FF_CHUNK)
            t = jnp.maximum(jnp.dot(a2, w1_ref[:, cs], preferred_element_type=F32), 0.0)
            f = f + jnp.dot((t * t).astype(BF16), w2_ref[cs, :], preferred_element_type=F32)
        o_ref[0] = h1 + mod[5:6, :] * _rms(f, g_ref[3:4, :])

        z = z_ref[0]
        has_prev = jnp.where(j > 0, 1.0, 0.0)
        has_next = jnp.where(j < n_tiles - 1, 1.0, 0.0)
        z_before = zp_ref[0][SUBLANES - 1:SUBLANES, :] * has_prev
        z_after = zn_ref[0][0:1, :] * has_next
        row = lax.broadcasted_iota(jnp.int32, (tm, 1), 0)
        z_dn = jnp.where(row == 0, z_before, pltpu.roll(z, 1, axis=0))
        z_up = jnp.where(row == tm - 1, z_after, pltpu.roll(z, tm - 1, axis=0))
        conv = cb_ref[0] * (z_dn * cw_ref[0:1, :] + z * cw_ref[1:2, :] + z_up * cw_ref[2:3, :])
        n_chunks = y2_ref.shape[0]
        halves = S5_T * SSM_GROUP // LANES
        groups = [[y2_ref[:, (gi * halves + hf) * LANES:(gi * halves + hf + 1) * LANES]
                   for hf in range(halves)] for gi in range(SSM_GROUPS)]
        steps = _piece_transpose(groups)
        for s in range(S5_T):
            for hf in range(SSM_WIDTH // LANES):
                y_scr[hf, pl.ds(s, n_chunks, stride=S5_T), :] = steps[s][hf]
        y = jnp.concatenate([y_scr[hf] for hf in range(SSM_WIDTH // LANES)], axis=1)
        y = y + dsk_ref[...] * u_ref[0]
        gl = 0.5 * y * (1.0 + jnp.tanh(math.sqrt(2.0 / math.pi) * (y + 0.044715 * (y * y * y))))
        gate = jnp.dot(gl.astype(BF16), wglu_ref[...], preferred_element_type=F32) + bglu_ref[...]
        ssm = gl * _sigmoid(gate)
        a0, a1 = ATTN_WIDTH, ATTN_WIDTH + CONV_WIDTH
        mix_scr[new, :, 0:a0] = attn_ref[0]
        mix_scr[new, :, a0:a1] = conv.astype(BF16)
        mix_scr[new, :, a1:] = ssm.astype(BF16)

    for k in range(2):
        pl.when(step_id % 2 == k)(functools.partial(step, k, 1 - k))


def _block(h, attn, cb, z, y2, u, mods, mod_row, params, tm, layer):
    b, n, _ = h.shape
    nt = n // tm
    total = nt * b
    per = tm // SUBLANES
    n8 = n // SUBLANES

    def tile(step, lag):
        m = jnp.clip(step - lag, 0, total - 1)
        return m // b, m % b

    def row(w, lag):
        def index(step):
            j, i = tile(step, lag)
            return i, j, 0
        return pl.BlockSpec((1, tm, w), index)

    def halo(offset):
        def index(step):
            j, i = tile(step, 0)
            return i, jnp.clip(j * per + offset, 0, n8 - 1), 0
        return pl.BlockSpec((1, SUBLANES, CONV_WIDTH), index)

    def chunk_index(step):
        j, i = tile(step, 0)
        return i * nt + j, 0

    return pl.pallas_call(
        functools.partial(_block_kernel, tm=tm, n_batch=b, n_tiles=nt),
        grid=(total + 1,),
        in_specs=[row(D_MODEL, 1), row(ATTN_WIDTH, 0), row(CONV_WIDTH, 0), row(CONV_WIDTH, 0),
                  halo(-1), halo(per),
                  pl.BlockSpec((tm // S5_T, S5_T * SSM_WIDTH), chunk_index), row(SSM_WIDTH, 0),
                  pl.BlockSpec((1, N_MOD, D_MODEL), lambda step: (mod_row(tile(step, 1)[1]), 0, 0))]
        + [_layer_spec(p, layer) for p in params],
        out_specs=row(D_MODEL, 1),
        out_shape=jax.ShapeDtypeStruct((b, n, D_MODEL), F32),
        scratch_shapes=[pltpu.VMEM((SSM_WIDTH // LANES, tm, LANES), F32),
                        pltpu.VMEM((2, tm, MIX_WIDTH), BF16)],
        compiler_params=_params(1),
        name="mix_mlp_block",
    )(h, attn, cb, z, z, z, y2, u, mods, *params)


def _rope_tables(n_lat, n_ctx):
    f32 = np.float32
    rows = n_lat // GRID_W
    row = np.repeat(np.arange(rows), GRID_W).astype(f32)[:, None]
    col = np.tile(np.arange(GRID_W), rows).astype(f32)[:, None]
    freqs = (f32(ROPE_BASE) ** (-np.arange(ROPE_PAIRS, dtype=f32) / f32(ROPE_PAIRS))).astype(f32)
    ang_r, ang_c = row * freqs, col * freqs
    cos_h = np.concatenate([np.cos(ang_r)] * 2 + [np.cos(ang_c)] * 2, axis=-1)
    sin_h = np.concatenate([-np.sin(ang_r), np.sin(ang_r), -np.sin(ang_c), np.sin(ang_c)], axis=-1)
    scale = np.concatenate([np.full((ATTN_WIDTH,), LOG2E * HEAD_DIM ** -0.5, f32),
                            np.ones((KV_WIDTH,), f32)])
    n_heads = N_Q_HEADS + N_KV_HEADS
    cos = (np.tile(cos_h, (1, n_heads)) * scale).astype(f32)
    sin = (np.tile(sin_h, (1, n_heads)) * scale).astype(f32)
    cos_c = np.broadcast_to(scale, (n_ctx, ROPE_WIDTH)).astype(f32)
    sin_c = np.zeros((n_ctx, ROPE_WIDTH), f32)
    return tuple(jnp.asarray(t) for t in (cos, sin, cos_c, sin_c))


def kernel(x, c, ctx, c_ctx, w_ada, b_ada, norm_g, w_in, conv_w, attn_sink, ssm_lam_re, ssm_lam_im,
           ssm_log_dt, ssm_b_re, ssm_b_im, ssm_c_re, ssm_c_im, ssm_d, w_glu, b_glu, w_out,
           w_mlp_in, w_mlp_out):
    bsz, n_lat, _ = x.shape
    n_ctx = ctx.shape[1]
    tm_lat = min(ROW_TILE, n_lat)
    tm_ctx = min(ROW_TILE, n_ctx)
    cos, sin, cos_c, sin_c = _rope_tables(n_lat, n_ctx)

    pad = (-(bsz + 1)) % SUBLANES
    cvec = jnp.concatenate([c, c_ctx[None, :], jnp.zeros((pad, D_MODEL), F32)], axis=0)
    rows = cvec.shape[0]
    mods = _ada_mods(cvec, w_ada, b_ada).reshape(DEPTH * rows, N_MOD, D_MODEL)

    s5w = jax.vmap(_s5_weights)(ssm_lam_re, ssm_lam_im, ssm_log_dt, ssm_b_re, ssm_b_im,
                                ssm_c_re, ssm_c_im)
    w_in_b = w_in.astype(BF16)
    block_params = (norm_g, conv_w, ssm_d.reshape(DEPTH, 1, SSM_WIDTH), w_glu.astype(BF16),
                    b_glu.reshape(DEPTH, 1, SSM_WIDTH), w_out.astype(BF16),
                    w_mlp_in.astype(BF16), w_mlp_out.astype(BF16))
    zero_state = jnp.zeros((2, bsz, 2 * N_STATE), F32)

    h, hc = x, ctx
    for l in range(DEPTH):
        with_ctx_out = l < DEPTH - 1
        lat_row = lambda i, base=l * rows: base + i
        ctx_row = lambda i, base=l * rows + bsz: base
        qe, qo, k2, v4, cb, z, u, u2 = _in_proj(h, mods, lat_row, norm_g, w_in_b, cos, sin,
                                                tm_lat, True, l)
        qc, kc, vc, k2c, v4c, cbc, zc, uc, u2c = _in_proj(hc, mods, ctx_row, norm_g, w_in_b,
                                                          cos_c, sin_c, tm_ctx, False, l)

        attn = _attention_lat(attn_sink, l, qe, qo, k2, v4, k2c, v4c)

        y2c, ctx_state = _s5_segment(u2c, bsz, s5w, zero_state, l)
        y2, _ = _s5_segment(u2, bsz, s5w, ctx_state, l)

        h = _block(h, attn, cb, z, y2, u, mods, lat_row, block_params, tm_lat, l)
        if with_ctx_out:
            attn_c = _attention_ctx(attn_sink, l, qc, kc, vc)
            hc = _block(hc, attn_c, cbc, zc, y2c, uc, mods, ctx_row, block_params, tm_ctx, l)
    return h
```

```python
import functools
import math

import numpy as np
import jax
import jax.numpy as jnp
from jax import lax
from jax.experimental import pallas as pl
from jax.experimental.pallas import tpu as pltpu

F32 = jnp.float32
BF16 = jnp.bfloat16

D_MODEL = 1024
DEPTH = 4
GRID_W = 64
HEAD_DIM = 64
N_Q_HEADS = 8
N_KV_HEADS = 2
Q_PER_KV = N_Q_HEADS // N_KV_HEADS
ATTN_WIDTH = N_Q_HEADS * HEAD_DIM
KV_WIDTH = N_KV_HEADS * HEAD_DIM
WINDOW = 128
Q_BLOCK = 128
ROPE_BASE = 10000.0
ROPE_PAIRS = HEAD_DIM // 4
CONV_WIDTH = 256
SSM_WIDTH = 256
SSM_GROUP = 16
SSM_GROUPS = SSM_WIDTH // SSM_GROUP
SSM_STATE = 64
N_STATE = SSM_GROUPS * SSM_STATE
MIX_WIDTH = ATTN_WIDTH + CONV_WIDTH + SSM_WIDTH
IN_WIDTH = ATTN_WIDTH + 2 * KV_WIDTH + 3 * CONV_WIDTH + SSM_WIDTH
ROPE_WIDTH = ATTN_WIDTH + KV_WIDTH
D_FF = 4 * D_MODEL
N_MOD = 6
EPS = 1e-6
NEG_INF = -1e30
LOG2E = math.log2(math.e)

LANES = 128
SUBLANES = 8
VMEM_LIMIT = 56 * 1024 * 1024

ROW_TILE = 512
ATTN_GROUP = 2
S5_T = 16
S5_BLOCK_CHUNKS = 8
S5_BATCH_TILE = 8
S5_CHUNK_TILE = 32
S5_PAIRS = SSM_GROUPS // 2
PAIR_IN = 2 * SSM_GROUP * S5_T
PAIR_STATE = 4 * SSM_STATE
FF_CHUNK = 1024


def _layer_spec(arr, layer):
    nd = arr.ndim
    return pl.BlockSpec((None,) + arr.shape[1:], lambda *_: (layer,) + (0,) * (nd - 1),
                        pipeline_mode=pl.Buffered(1))


def _params(n_grid):
    return pltpu.CompilerParams(dimension_semantics=("arbitrary",) * n_grid,
                                vmem_limit_bytes=VMEM_LIMIT)


def _rms(x, g):
    ms = jnp.mean(x * x, axis=-1, keepdims=True)
    return x * lax.rsqrt(ms + EPS) * g


def _sigmoid(x):
    return 1.0 / (1.0 + jnp.exp(-x))


def _ada_kernel(c_ref, w_ref, b_ref, o_ref):
    c = c_ref[...]
    act = (c * _sigmoid(c)).astype(BF16)
    o_ref[0] = jnp.dot(act, w_ref[0].astype(BF16), preferred_element_type=F32) + b_ref[0]


def _ada_mods(cvec, w_ada, b_ada):
    rows = cvec.shape[0]
    tn = 1024
    n_out = N_MOD * D_MODEL
    return pl.pallas_call(
        _ada_kernel,
        grid=(DEPTH, n_out // tn),
        in_specs=[pl.BlockSpec((rows, D_MODEL), lambda l, j: (0, 0)),
                  pl.BlockSpec((1, D_MODEL, tn), lambda l, j: (l, 0, j)),
                  pl.BlockSpec((1, 1, tn), lambda l, j: (l, 0, j))],
        out_specs=pl.BlockSpec((1, rows, tn), lambda l, j: (l, 0, j)),
        out_shape=jax.ShapeDtypeStruct((DEPTH, rows, n_out), F32),
        compiler_params=_params(2),
        name="ada_mods",
    )(cvec, w_ada, b_ada.reshape(DEPTH, 1, n_out))


def _piece_transpose(src):
    per = LANES // SSM_GROUP
    size = len(src)
    assert size == 2 * per and len(src[0]) == 2
    n = src[0][0].shape[0]
    slot = lax.broadcasted_iota(jnp.int32, (n, LANES), 1) // SSM_GROUP
    rows = [list(r) for r in src]
    for r in range(per):
        rows[r][1], rows[r + per][0] = rows[r + per][0], rows[r][1]
    d = per // 2
    while d >= 1:
        keep = (slot & d) == 0
        for r in range(size):
            if r & d:
                continue
            for hf in range(2):
                lo, hi = rows[r][hf], rows[r + d][hf]
                rows[r][hf] = jnp.where(keep, lo, pltpu.roll(hi, d * SSM_GROUP, axis=1))
                rows[r + d][hf] = jnp.where(keep, pltpu.roll(lo, LANES - d * SSM_GROUP, axis=1), hi)
        d //= 2
    return rows


def _inproj_kernel(*refs, lat):
    h_ref, mod_ref, g_ref, w_ref, cos_ref, sin_ref = refs[:6]
    if lat:
        qe_ref, qo_ref, k2_ref, v4_ref, cb_ref, z_ref, u_ref, u2_ref = refs[6:14]
    else:
        q_ref, k_ref, v_ref, k2_ref, v4_ref, cb_ref, z_ref, u_ref, u2_ref = refs[6:15]
    u_scr, a_scr, p_scr = refs[-3:]
    step_id = pl.program_id(0)

    @pl.when(step_id == 0)
    def _():
        a_scr[...] = jnp.zeros_like(a_scr)
        p_scr[...] = jnp.zeros_like(p_scr)

    lane = lax.broadcasted_iota(jnp.int32, (1, LANES), 1)
    first_of_pair = (lane & ROPE_PAIRS) == 0
    low_head = lane < HEAD_DIM

    def step(new, old):
        p_scr[old] = jnp.dot(a_scr[old], w_ref[...], preferred_element_type=F32)

        mod = mod_ref[0]
        a = _rms(h_ref[0], g_ref[0:1, :]) * (1.0 + mod[1:2, :]) + mod[0:1, :]
        a_scr[new] = a.astype(BF16)

        for s in range(ROPE_WIDTH // LANES):
            sl = slice(s * LANES, (s + 1) * LANES)
            t = p_scr[new, :, sl]
            partner = jnp.where(first_of_pair,
                                pltpu.roll(t, LANES - ROPE_PAIRS, axis=1),
                                pltpu.roll(t, ROPE_PAIRS, axis=1))
            r = t * cos_ref[:, sl] + partner * sin_ref[:, sl]
            if s < ATTN_WIDTH // LANES:
                if lat:
                    qe_ref[0, :, sl] = jnp.where(low_head, r, 0.0).astype(BF16)
                    qo_ref[0, :, sl] = jnp.where(low_head, 0.0, r).astype(BF16)
                else:
                    q_ref[0, :, sl] = r.astype(BF16)
            else:
                k2_ref[0, :, 0:LANES] = r.astype(BF16)
                k2_ref[0, :, LANES:2 * LANES] = pltpu.roll(r, HEAD_DIM, axis=1).astype(BF16)
                if not lat:
                    k_ref[0] = r.astype(BF16)
        o = ROPE_WIDTH
        v = p_scr[new, :, o:o + KV_WIDTH]
        v_swapped = pltpu.roll(v, HEAD_DIM, axis=1)
        for idx, part in enumerate((jnp.where(low_head, v, 0.0),
                                    jnp.where(low_head, 0.0, v_swapped),
                                    jnp.where(low_head, v_swapped, 0.0),
                                    jnp.where(low_head, 0.0, v))):
            v4_ref[0, :, idx * LANES:(idx + 1) * LANES] = part.astype(BF16)
        if not lat:
            v_ref[0] = v.astype(BF16)
        o += KV_WIDTH
        cb_ref[0] = p_scr[new, :, o:o + CONV_WIDTH]
        o += CONV_WIDTH
        z_ref[0] = (p_scr[new, :, o:o + CONV_WIDTH]
                    * p_scr[new, :, o + CONV_WIDTH:o + 2 * CONV_WIDTH])
        o += 2 * CONV_WIDTH
        u_ref[0] = p_scr[new, :, o:o + SSM_WIDTH]
        n_chunks = u2_ref.shape[0]
        halves = SSM_WIDTH // LANES
        for hf in range(halves):
            u_scr[hf] = p_scr[new, :, o + hf * LANES:o + (hf + 1) * LANES]
        steps = [[u_scr[hf, pl.ds(s, n_chunks, stride=S5_T), :]
                  for hf in range(halves)] for s in range(S5_T)]
        groups = _piece_transpose(steps)
        for gi in range(SSM_GROUPS):
            for hf in range(halves):
                c0 = gi * SSM_GROUP * S5_T + hf * LANES
                u2_ref[:, c0:c0 + LANES] = groups[gi][hf].astype(BF16)

    for k in range(2):
        pl.when(step_id % 2 == k)(functools.partial(step, k, 1 - k))


def _in_proj(h, mods, mod_row, norm_g, w_in, cos, sin, tm, lat, layer):
    b, n, _ = h.shape
    nt = n // tm
    total = nt * b
    if lat:
        widths = [(ATTN_WIDTH, BF16), (ATTN_WIDTH, BF16)]
    else:
        widths = [(ATTN_WIDTH, BF16), (KV_WIDTH, BF16), (KV_WIDTH, BF16)]
    widths += [(2 * KV_WIDTH, BF16), (4 * KV_WIDTH, BF16),
               (CONV_WIDTH, F32), (CONV_WIDTH, F32), (SSM_WIDTH, F32)]

    def tile(step, lag):
        m = jnp.clip(step - lag, 0, total - 1)
        return m // b, m % b

    def row(w, lag):
        def index(step):
            j, i = tile(step, lag)
            return i, j, 0
        return pl.BlockSpec((1, tm, w), index)

    def chunk_index(step):
        j, i = tile(step, 2)
        return i * nt + j, 0

    table = pl.BlockSpec((tm, ROPE_WIDTH), lambda step: (tile(step, 2)[0], 0))
    return pl.pallas_call(
        functools.partial(_inproj_kernel, lat=lat),
        grid=(total + 2,),
        in_specs=[row(D_MODEL, 0),
                  pl.BlockSpec((1, N_MOD, D_MODEL), lambda step: (mod_row(tile(step, 0)[1]), 0, 0)),
                  _layer_spec(norm_g, layer),
                  _layer_spec(w_in, layer),
                  table, table],
        out_specs=[row(w, 2) for w, _ in widths]
        + [pl.BlockSpec((tm // S5_T, S5_T * SSM_WIDTH), chunk_index)],
        out_shape=[jax.ShapeDtypeStruct((b, n, w), dt) for w, dt in widths]
        + [jax.ShapeDtypeStruct((b * n // S5_T, S5_T * SSM_WIDTH), BF16)],
        scratch_shapes=[pltpu.VMEM((SSM_WIDTH // LANES, tm, LANES), F32),
                        pltpu.VMEM((2, tm, D_MODEL), BF16),
                        pltpu.VMEM((2, tm, IN_WIDTH), F32)],
        compiler_params=_params(1),
        name="in_proj",
    )(h, mods, norm_g, w_in, cos, sin)


def _softmax_parts(s, sink):
    m = jnp.maximum(jnp.max(s, axis=-1, keepdims=True), sink)
    p = jnp.exp2(s - m)
    denom = jnp.sum(p, axis=-1, keepdims=True) + jnp.exp2(sink - m)
    return p, 1.0 / denom


def _attn_ctx_kernel(sink_ref, q_ref, k_ref, v_ref, o_ref, *, layer):
    q, k, v = q_ref[0], k_ref[0], v_ref[0]
    outs = []
    for h in range(N_Q_HEADS):
        hk = h // Q_PER_KV
        ksl = slice(hk * HEAD_DIM, (hk + 1) * HEAD_DIM)
        s = lax.dot_general(q[:, h * HEAD_DIM:(h + 1) * HEAD_DIM], k[:, ksl],
                            (((1,), (1,)), ((), ())), preferred_element_type=F32)
        p, inv = _softmax_parts(s, sink_ref[layer, h] * LOG2E)
        outs.append(jnp.dot(p.astype(BF16), v[:, ksl], preferred_element_type=F32) * inv)
    o_ref[0] = jnp.concatenate(outs, axis=1).astype(BF16)


def _attn_lat_kernel(sink_ref, qe_ref, qo_ref, k_ref, kx_ref, v_ref, vx_ref, bias_ref, o_ref,
                     s_scr, p_scr, inv_scr, *, tq, layer, n_blocks, total):
    n = pl.program_id(0)
    groups_per_seq = n_blocks // ATTN_GROUP

    def block(lag, sub):
        return (jnp.clip(n - lag, 0, total - 1) % groups_per_seq) * ATTN_GROUP + sub

    def window(lag, sub):
        return pl.multiple_of(jnp.clip(block(lag, sub) - 1, 0, n_blocks - 3) * tq, tq)

    @pl.when(n == 0)
    def _():
        s_scr[...] = jnp.zeros_like(s_scr)
        p_scr[...] = jnp.zeros_like(p_scr)
        inv_scr[...] = jnp.zeros_like(inv_scr)

    lane = lax.broadcasted_iota(jnp.int32, (1, LANES), 1)
    low_head = lane < HEAD_DIM
    slabs = ATTN_WIDTH // LANES

    def step(new, old):
        for sub in range(ATTN_GROUP):
            rows = slice(sub * tq, (sub + 1) * tq)
            vall = jnp.concatenate([v_ref[0, pl.ds(window(2, sub), 3 * tq), :], vx_ref[0]], axis=0)
            for t in range(slabs):
                c0 = (t // (slabs // N_KV_HEADS)) * 2 * LANES
                o = (jnp.dot(p_scr[new, sub, 2 * t], vall[:, c0:c0 + LANES],
                             preferred_element_type=F32)
                     + jnp.dot(p_scr[new, sub, 2 * t + 1], vall[:, c0 + LANES:c0 + 2 * LANES],
                               preferred_element_type=F32))
                o_ref[0, rows, t * LANES:(t + 1) * LANES] = (o * inv_scr[new, sub, t]).astype(BF16)

            kall = jnp.concatenate([k_ref[0, pl.ds(window(0, sub), 3 * tq), :], kx_ref[0]], axis=0)
            for hk in range(N_KV_HEADS):
                t0 = hk * (slabs // N_KV_HEADS)
                for par, q_ref in enumerate((qe_ref, qo_ref)):
                    lhs = jnp.concatenate([q_ref[0, rows, t0 * LANES:(t0 + 1) * LANES],
                                           q_ref[0, rows, (t0 + 1) * LANES:(t0 + 2) * LANES]],
                                          axis=0)
                    kv = (hk + par) % 2
                    s = lax.dot_general(lhs, kall[:, kv * LANES:(kv + 1) * LANES],
                                        (((1,), (1,)), ((), ())), preferred_element_type=F32)
                    s_scr[new, sub, 2 * t0 + par] = s[:tq]
                    s_scr[new, sub, 2 * t0 + 2 + par] = s[tq:]

            j = block(1, sub)
            variant = jnp.where(j == 0, 0, jnp.where(j == n_blocks - 1, 2, 1))
            bias = bias_ref[variant, :, 0:3 * tq]
            invs = []
            for h in range(N_Q_HEADS):
                s = jnp.concatenate([s_scr[old, sub, h, :, 0:3 * tq] + bias,
                                     s_scr[old, sub, h, :, 3 * tq:]], axis=1)
                p, inv = _softmax_parts(s, sink_ref[layer, h] * LOG2E)
                p_scr[old, sub, h] = p.astype(BF16)
                invs.append(inv)
            for t in range(slabs):
                inv_scr[old, sub, t] = jnp.where(low_head, invs[2 * t], invs[2 * t + 1])

    for k in range(2):
        pl.when(n % 2 == k)(functools.partial(step, k, 1 - k))


def _window_bias(tq, n_ctx):
    r = np.arange(tq)[:, None]
    c = np.arange(3 * tq + n_ctx)[None, :]
    variants = []
    for q_start in (0, tq, 2 * tq):
        ok = (c >= 3 * tq) | (np.abs(c - (q_start + r)) <= WINDOW)
        variants.append(np.where(ok, 0.0, NEG_INF))
    return jnp.asarray(np.stack(variants), F32)


def _attention_lat(sink, layer, qe, qo, k2, v4, k2x, v4x):
    b, n, _ = qe.shape
    n_ctx = k2x.shape[1]
    tq = Q_BLOCK
    nb = n // tq
    groups = nb // ATTN_GROUP
    total = b * groups
    n_keys = 3 * tq + n_ctx
    assert tq == WINDOW and nb >= 3 and nb % ATTN_GROUP == 0

    def group_of(step, lag):
        m = jnp.clip(step - lag, 0, total - 1)
        return m // groups, m % groups

    def rows(width, lag):
        return pl.BlockSpec((1, ATTN_GROUP * tq, width), lambda step: (*group_of(step, lag), 0))

    def whole(length, width, lag):
        return pl.BlockSpec((1, length, width), lambda step: (group_of(step, lag)[0], 0, 0))

    bias = _window_bias(tq, n_ctx)
    kw, vw = 2 * KV_WIDTH, 4 * KV_WIDTH
    return pl.pallas_call(
        functools.partial(_attn_lat_kernel, tq=tq, layer=layer, n_blocks=nb, total=total),
        grid=(total + 2,),
        in_specs=[pl.BlockSpec(memory_space=pltpu.SMEM),
                  rows(ATTN_WIDTH, 0), rows(ATTN_WIDTH, 0),
                  whole(n, kw, 0), whole(n_ctx, kw, 0),
                  whole(n, vw, 2), whole(n_ctx, vw, 2),
                  pl.BlockSpec(bias.shape, lambda step: (0, 0, 0), pipeline_mode=pl.Buffered(1))],
        out_specs=rows(ATTN_WIDTH, 2),
        out_shape=jax.ShapeDtypeStruct((b, n, ATTN_WIDTH), BF16),
        scratch_shapes=[pltpu.VMEM((2, ATTN_GROUP, N_Q_HEADS, tq, n_keys), F32),
                        pltpu.VMEM((2, ATTN_GROUP, N_Q_HEADS, tq, n_keys), BF16),
                        pltpu.VMEM((2, ATTN_GROUP, ATTN_WIDTH // LANES, tq, LANES), F32)],
        compiler_params=_params(1),
        name="attn_lat",
    )(sink, qe, qo, k2, k2x, v4, v4x, bias)


def _attention_ctx(sink, layer, q, k, v):
    b, n, _ = q.shape
    spec = lambda w: pl.BlockSpec((1, n, w), lambda i: (i, 0, 0))
    return pl.pallas_call(
        functools.partial(_attn_ctx_kernel, layer=layer),
        grid=(b,),
        in_specs=[pl.BlockSpec(memory_space=pltpu.SMEM), spec(ATTN_WIDTH), spec(KV_WIDTH),
                  spec(KV_WIDTH)],
        out_specs=spec(ATTN_WIDTH),
        out_shape=jax.ShapeDtypeStruct((b, n, ATTN_WIDTH), BF16),
        compiler_params=_params(1),
        name="attn_ctx",
    )(sink, q, k, v)


def _s5_exit_kernel(u_ref, we_ref, e_ref):
    bt, ct, _ = u_ref.shape
    half = e_ref.shape[2] // 2
    for gp in range(S5_PAIRS):
        u = u_ref[:, :, gp * PAIR_IN:(gp + 1) * PAIR_IN].reshape(bt * ct, PAIR_IN)
        r = jnp.dot(u, we_ref[gp], preferred_element_type=F32)
        for i in range(bt):
            rows = slice(i * ct, (i + 1) * ct)
            e_ref[:, i, gp * PAIR_STATE:(gp + 1) * PAIR_STATE] = r[rows, :PAIR_STATE]
            e_ref[:, i, half + gp * PAIR_STATE:half + (gp + 1) * PAIR_STATE] = r[rows, PAIR_STATE:]


def _s5_carry_kernel(e_ref, init_ref, lr_ref, li_ref, p_ref, fin_ref, state):
    @pl.when(pl.program_id(1) == 0)
    def _():
        state[...] = init_ref[0]

    def run(chunks):
        for gp in range(S5_PAIRS):
            re = slice(gp * PAIR_STATE, gp * PAIR_STATE + LANES)
            im = slice(gp * PAIR_STATE + LANES, (gp + 1) * PAIR_STATE)
            lr = lr_ref[0, :, gp * LANES:(gp + 1) * LANES]
            li = li_ref[0, :, gp * LANES:(gp + 1) * LANES]
            sr, si = state[:, re], state[:, im]
            for k in chunks:
                p_ref[k, :, re] = sr
                p_ref[k, :, im] = si
                sr, si = (lr * sr - li * si + e_ref[k, :, re],
                          lr * si + li * sr + e_ref[k, :, im])
            state[:, re] = sr
            state[:, im] = si

    n_chunks = e_ref.shape[0]
    direction = pl.program_id(0)
    pl.when(direction == 0)(lambda: run(tuple(range(n_chunks))))
    pl.when(direction == 1)(lambda: run(tuple(reversed(range(n_chunks)))))
    fin_ref[0] = state[...]


def _s5_out_kernel(u_ref, p_ref, toep_ref, csf_ref, csb_ref, y_ref):
    bt, ct, _ = u_ref.shape
    half = p_ref.shape[2] // 2
    grp = PAIR_IN // 2
    by_batch = lambda cols: jnp.concatenate([p_ref[:, i, cols] for i in range(bt)],
                                            axis=0).astype(BF16)
    for gp in range(S5_PAIRS):
        pf = by_batch(slice(gp * PAIR_STATE, (gp + 1) * PAIR_STATE))
        pb = by_batch(slice(half + gp * PAIR_STATE, half + (gp + 1) * PAIR_STATE))
        st = (jnp.dot(pf, csf_ref[gp], preferred_element_type=F32)
              + jnp.dot(pb, csb_ref[gp], preferred_element_type=F32))
        for a in range(2):
            cols = slice(gp * PAIR_IN + a * grp, gp * PAIR_IN + (a + 1) * grp)
            y = (jnp.dot(u_ref[:, :, cols].reshape(bt * ct, grp), toep_ref[2 * gp + a],
                         preferred_element_type=F32) + st[:, a * grp:(a + 1) * grp])
            y_ref[:, :, cols] = y.reshape(bt, ct, grp)


def _s5_segment(u2, bsz, weights, init, layer):
    toep, we, csf, csb, lr, li = weights
    n_rows, width = u2.shape
    nc = n_rows // bsz
    bt, ct = S5_BATCH_TILE, min(S5_CHUNK_TILE, nc)
    n_state = 2 * S5_PAIRS * PAIR_STATE
    u3 = u2.reshape(bsz, nc, width)
    tokens = pl.BlockSpec((bt, ct, width), lambda i, j: (i, j, 0))
    states = pl.BlockSpec((ct, bt, n_state), lambda i, j: (j, i, 0))
    exits = pl.pallas_call(
        _s5_exit_kernel,
        grid=(bsz // bt, nc // ct),
        in_specs=[tokens, _layer_spec(we, layer)],
        out_specs=states,
        out_shape=jax.ShapeDtypeStruct((nc, bsz, n_state), F32),
        compiler_params=_params(2),
        name="s5_exit",
    )(u3, we)

    nb = nc // S5_BLOCK_CHUNKS
    dir_blk = pl.BlockSpec((S5_BLOCK_CHUNKS, bsz, n_state // 2),
                           lambda d, k: (jnp.where(d == 0, k, nb - 1 - k), 0, d))
    state_spec = pl.BlockSpec((1, bsz, n_state // 2), lambda d, k: (d, 0, 0))
    lam_spec = pl.BlockSpec((None, 1, 1, n_state // 4), lambda d, k: (layer, d, 0, 0))
    entering, leaving = pl.pallas_call(
        _s5_carry_kernel,
        grid=(2, nb),
        in_specs=[dir_blk, state_spec, lam_spec, lam_spec],
        out_specs=[dir_blk, state_spec],
        out_shape=[jax.ShapeDtypeStruct((nc, bsz, n_state), F32),
                   jax.ShapeDtypeStruct((2, bsz, n_state // 2), F32)],
        scratch_shapes=[pltpu.VMEM((bsz, n_state // 2), F32)],
        compiler_params=_params(2),
        name="s5_carry",
    )(exits, init, lr, li)

    y3 = pl.pallas_call(
        _s5_out_kernel,
        grid=(bsz // bt, nc // ct),
        in_specs=[tokens, states, _layer_spec(toep, layer),
                  _layer_spec(csf, layer), _layer_spec(csb, layer)],
        out_specs=tokens,
        out_shape=jax.ShapeDtypeStruct((bsz, nc, width), F32),
        compiler_params=_params(2),
        name="s5_out",
    )(u3, entering, toep, csf, csb)
    return y3.reshape(n_rows, width), leaving


def _pair_diag(m):
    g, r, c = m.shape
    eye = jnp.eye(2, dtype=m.dtype)
    return jnp.einsum('qarc,ab->qarbc', m.reshape(g // 2, 2, r, c), eye).reshape(g // 2, 2 * r, 2 * c)


def _s5_weights(lam_re, lam_im, log_dt, b_re, b_im, c_re, c_im):
    t = S5_T
    lam = lax.complex(lam_re.astype(F32), lam_im.astype(F32))
    lam_dt = lam * jnp.exp(log_dt.astype(F32))[..., None]
    lam_bar = jnp.exp(lam_dt)
    bb = ((lam_bar - 1) / lam)[..., None] * lax.complex(b_re.astype(F32), b_im.astype(F32))
    cm = lax.complex(c_re.astype(F32), c_im.astype(F32))
    steps = jnp.arange(t + 1, dtype=F32)
    pw = jnp.exp(lam_dt[None] * steps[:, None, None, None])

    kern = jnp.real(jnp.einsum('dgop,kdgp,dgpi->dkgoi', cm, pw[:t], bb))
    s_idx = jnp.arange(t)[:, None]
    t_idx = jnp.arange(t)[None, :]
    lag = t_idx - s_idx
    kf = jnp.where((lag >= 0)[..., None, None, None], kern[0][jnp.maximum(lag, 0)], 0.0)
    kb = jnp.where((lag <= 0)[..., None, None, None], kern[1][jnp.maximum(-lag, 0)], 0.0)
    toep = jnp.transpose(kf + kb, (2, 0, 4, 1, 3))
    toep = toep.reshape(SSM_GROUPS, t * SSM_GROUP, t * SSM_GROUP)

    ef = pw[t - 1 - jnp.arange(t), 0][..., None] * bb[0][None]
    eb = pw[jnp.arange(t), 1][..., None] * bb[1][None]
    to_cols = lambda m: jnp.transpose(m, (1, 0, 3, 2)).reshape(SSM_GROUPS, t * SSM_GROUP, SSM_STATE)
    we = jnp.concatenate([_pair_diag(to_cols(part(m))) for m in (ef, eb)
                          for part in (jnp.real, jnp.imag)], axis=-1)

    cf = cm[0][None] * pw[1 + jnp.arange(t), 0][:, :, None, :]
    cb = cm[1][None] * pw[t - jnp.arange(t), 1][:, :, None, :]
    to_rows = lambda m: jnp.transpose(m, (1, 3, 0, 2)).reshape(SSM_GROUPS, SSM_STATE, t * SSM_GROUP)
    cs = [jnp.concatenate([_pair_diag(to_rows(jnp.real(m))), _pair_diag(to_rows(-jnp.imag(m)))],
                          axis=1) for m in (cf, cb)]

    lam_t = pw[t].reshape(2, 1, N_STATE)
    return (toep.astype(BF16), we.astype(BF16), cs[0].astype(BF16), cs[1].astype(BF16),
            jnp.real(lam_t), jnp.imag(lam_t))


def _block_kernel(h_ref, attn_ref, cb_ref, z_ref, zp_ref, zn_ref, y2_ref, u_ref, mod_ref, g_ref,
                  cw_ref, dsk_ref, wglu_ref, bglu_ref, wout_ref, w1_ref, w2_ref, o_ref, y_scr,
                  mix_scr, *, tm, n_batch, n_tiles):
    step_id = pl.program_id(0)

    @pl.when(step_id == 0)
    def _():
        mix_scr[...] = jnp.zeros_like(mix_scr)

    j = jnp.minimum(step_id, n_tiles * n_batch - 1) // n_batch

    def step(new, old):
        mod = mod_ref[0]
        m = jnp.dot(mix_scr[old], wout_ref[...], preferred_element_type=F32)
        h1 = h_ref[0] + mod[2:3, :] * _rms(m, g_ref[1:2, :])
        a2 = (_rms(h1, g_ref[2:3, :]) * (1.0 + mod[4:5, :]) + mod[3:4, :]).astype(BF16)
        f = jnp.zeros((tm, D_MODEL), F32)
        for cidx in range(D_FF // FF_CHUNK):
            cs = slice(cidx * FF_CHUNK, (cidx + 1) * FF_CHUNK)
            t = jnp.maximum(jnp.dot(a2, w1_ref[:, cs], preferred_element_type=F32), 0.0)
            f = f + jnp.dot((t * t).astype(BF16), w2_ref[cs, :], preferred_element_type=F32)
        o_ref[0] = h1 + mod[5:6, :] * _rms(f, g_ref[3:4, :])

        z = z_ref[0]
        has_prev = jnp.where(j > 0, 1.0, 0.0)
        has_next = jnp.where(j < n_tiles - 1, 1.0, 0.0)
        z_before = zp_ref[0][SUBLANES - 1:SUBLANES, :] * has_prev
        z_after = zn_ref[0][0:1, :] * has_next
        row = lax.broadcasted_iota(jnp.int32, (tm, 1), 0)
        z_dn = jnp.where(row == 0, z_before, pltpu.roll(z, 1, axis=0))
        z_up = jnp.where(row == tm - 1, z_after, pltpu.roll(z, tm - 1, axis=0))
        conv = cb_ref[0] * (z_dn * cw_ref[0:1, :] + z * cw_ref[1:2, :] + z_up * cw_ref[2:3, :])
        n_chunks = y2_ref.shape[0]
        halves = S5_T * SSM_GROUP // LANES
        groups = [[y2_ref[:, (gi * halves + hf) * LANES:(gi * halves + hf + 1) * LANES]
                   for hf in range(halves)] for gi in range(SSM_GROUPS)]
        steps = _piece_transpose(groups)
        for s in range(S5_T):
            for hf in range(SSM_WIDTH // LANES):
                y_scr[hf, pl.ds(s, n_chunks, stride=S5_T), :] = steps[s][hf]
        y = jnp.concatenate([y_scr[hf] for hf in range(SSM_WIDTH // LANES)], axis=1)
        y = y + dsk_ref[...] * u_ref[0]
        gl = 0.5 * y * (1.0 + jnp.tanh(math.sqrt(2.0 / math.pi) * (y + 0.044715 * (y * y * y))))
        gate = jnp.dot(gl.astype(BF16), wglu_ref[...], preferred_element_type=F32) + bglu_ref[...]
        ssm = gl * _sigmoid(gate)
        a0, a1 = ATTN_WIDTH, ATTN_WIDTH + CONV_WIDTH
        mix_scr[new, :, 0:a0] = attn_ref[0]
        mix_scr[new, :, a0:a1] = conv.astype(BF16)
        mix_scr[new, :, a1:] = ssm.astype(BF16)

    for k in range(2):
        pl.when(step_id % 2 == k)(functools.partial(step, k, 1 - k))


def _block(h, attn, cb, z, y2, u, mods, mod_row, params, tm, layer):
    b, n, _ = h.shape
    nt = n // tm
    total = nt * b
    per = tm // SUBLANES
    n8 = n // SUBLANES

    def tile(step, lag):
        m = jnp.clip(step - lag, 0, total - 1)
        return m // b, m % b

    def row(w, lag):
        def index(step):
            j, i = tile(step, lag)
            return i, j, 0
        return pl.BlockSpec((1, tm, w), index)

    def halo(offset):
        def index(step):
            j, i = tile(step, 0)
            return i, jnp.clip(j * per + offset, 0, n8 - 1), 0
        return pl.BlockSpec((1, SUBLANES, CONV_WIDTH), index)

    def chunk_index(step):
        j, i = tile(step, 0)
        return i * nt + j, 0

    return pl.pallas_call(
        functools.partial(_block_kernel, tm=tm, n_batch=b, n_tiles=nt),
        grid=(total + 1,),
        in_specs=[row(D_MODEL, 1), row(ATTN_WIDTH, 0), row(CONV_WIDTH, 0), row(CONV_WIDTH, 0),
                  halo(-1), halo(per),
                  pl.BlockSpec((tm // S5_T, S5_T * SSM_WIDTH), chunk_index), row(SSM_WIDTH, 0),
                  pl.BlockSpec((1, N_MOD, D_MODEL), lambda step: (mod_row(tile(step, 1)[1]), 0, 0))]
        + [_layer_spec(p, layer) for p in params],
        out_specs=row(D_MODEL, 1),
        out_shape=jax.ShapeDtypeStruct((b, n, D_MODEL), F32),
        scratch_shapes=[pltpu.VMEM((SSM_WIDTH // LANES, tm, LANES), F32),
                        pltpu.VMEM((2, tm, MIX_WIDTH), BF16)],
        compiler_params=_params(1),
        name="mix_mlp_block",
    )(h, attn, cb, z, z, z, y2, u, mods, *params)


def _rope_tables(n_lat, n_ctx):
    f32 = np.float32
    rows = n_lat // GRID_W
    row = np.repeat(np.arange(rows), GRID_W).astype(f32)[:, None]
    col = np.tile(np.arange(GRID_W), rows).astype(f32)[:, None]
    freqs = (f32(ROPE_BASE) ** (-np.arange(ROPE_PAIRS, dtype=f32) / f32(ROPE_PAIRS))).astype(f32)
    ang_r, ang_c = row * freqs, col * freqs
    cos_h = np.concatenate([np.cos(ang_r)] * 2 + [np.cos(ang_c)] * 2, axis=-1)
    sin_h = np.concatenate([-np.sin(ang_r), np.sin(ang_r), -np.sin(ang_c), np.sin(ang_c)], axis=-1)
    scale = np.concatenate([np.full((ATTN_WIDTH,), LOG2E * HEAD_DIM ** -0.5, f32),
                            np.ones((KV_WIDTH,), f32)])
    n_heads = N_Q_HEADS + N_KV_HEADS
    cos = (np.tile(cos_h, (1, n_heads)) * scale).astype(f32)
    sin = (np.tile(sin_h, (1, n_heads)) * scale).astype(f32)
    cos_c = np.broadcast_to(scale, (n_ctx, ROPE_WIDTH)).astype(f32)
    sin_c = np.zeros((n_ctx, ROPE_WIDTH), f32)
    return tuple(jnp.asarray(t) for t in (cos, sin, cos_c, sin_c))


def kernel(x, c, ctx, c_ctx, w_ada, b_ada, norm_g, w_in, conv_w, attn_sink, ssm_lam_re, ssm_lam_im,
           ssm_log_dt, ssm_b_re, ssm_b_im, ssm_c_re, ssm_c_im, ssm_d, w_glu, b_glu, w_out,
           w_mlp_in, w_mlp_out):
    bsz, n_lat, _ = x.shape
    n_ctx = ctx.shape[1]
    tm_lat = min(ROW_TILE, n_lat)
    tm_ctx = min(ROW_TILE, n_ctx)
    cos, sin, cos_c, sin_c = _rope_tables(n_lat, n_ctx)

    pad = (-(bsz + 1)) % SUBLANES
    cvec = jnp.concatenate([c, c_ctx[None, :], jnp.zeros((pad, D_MODEL), F32)], axis=0)
    rows = cvec.shape[0]
    mods = _ada_mods(cvec, w_ada, b_ada).reshape(DEPTH * rows, N_MOD, D_MODEL)

    s5w = jax.vmap(_s5_weights)(ssm_lam_re, ssm_lam_im, ssm_log_dt, ssm_b_re, ssm_b_im,
                                ssm_c_re, ssm_c_im)
    w_in_b = w_in.astype(BF16)
    block_params = (norm_g, conv_w, ssm_d.reshape(DEPTH, 1, SSM_WIDTH), w_glu.astype(BF16),
                    b_glu.reshape(DEPTH, 1, SSM_WIDTH), w_out.astype(BF16),
                    w_mlp_in.astype(BF16), w_mlp_out.astype(BF16))
    zero_state = jnp.zeros((2, bsz, 2 * N_STATE), F32)

    h, hc = x, ctx
    for l in range(DEPTH):
        with_ctx_out = l < DEPTH - 1
        lat_row = lambda i, base=l * rows: base + i
        ctx_row = lambda i, base=l * rows + bsz: base
        qe, qo, k2, v4, cb, z, u, u2 = _in_proj(h, mods, lat_row, norm_g, w_in_b, cos, sin,
                                                tm_lat, True, l)
        qc, kc, vc, k2c, v4c, cbc, zc, uc, u2c = _in_proj(hc, mods, ctx_row, norm_g, w_in_b,
                                                          cos_c, sin_c, tm_ctx, False, l)

        attn = _attention_lat(attn_sink, l, qe, qo, k2, v4, k2c, v4c)

        y2c, ctx_state = _s5_segment(u2c, bsz, s5w, zero_state, l)
        y2, _ = _s5_segment(u2, bsz, s5w, ctx_state, l)

        h = _block(h, attn, cb, z, y2, u, mods, lat_row, block_params, tm_lat, l)
        if with_ctx_out:
            attn_c = _attention_ctx(attn_sink, l, qc, kc, vc)
            hc = _block(hc, attn_c, cbc, zc, y2c, uc, mods, ctx_row, block_params, tm_ctx, l)
    return h
```
